```python
import jax, jax.numpy as jnp
from jax import lax
import numpy as np

D_MODEL = 1024
BATCH = 2
SEQ = 8192
DEPTH = 2

HEAD_DIM = 64
ATTN_SCALE = HEAD_DIM ** -0.5
ROPE_THETA = 10000.0
RMS_EPS = 1e-5
NEG_INF = -1e30
BAND_BLOCK = 128

SWA_Q_HEADS = 6
SWA_KV_HEADS = 2
SWA_GROUP = SWA_Q_HEADS // SWA_KV_HEADS
SWA_WINDOW = 128

MOBA_HEADS = 4
MOBA_BLOCK = 256
MOBA_TOPK = 3
MOBA_Q_CHUNK = 64

DIL_HEADS = 6
DIL_PATTERNS = ((128, 1), (512, 4), (2048, 16))

N_BRANCHES = 3
WIDTH_A = SWA_Q_HEADS * HEAD_DIM
KV_A = SWA_KV_HEADS * HEAD_DIM
WIDTH_B = MOBA_HEADS * HEAD_DIM
WIDTH_C = DIL_HEADS * HEAD_DIM
_IN_SIZES = (WIDTH_A, KV_A, KV_A, WIDTH_B, WIDTH_B, WIDTH_B,
             WIDTH_C, WIDTH_C, WIDTH_C, N_BRANCHES * D_MODEL)
IN_COLS = sum(_IN_SIZES)
SPLIT_POINTS = tuple(int(s) for s in np.cumsum(_IN_SIZES)[:-1])

D_FF_DENSE = 2816
N_EXPERTS = 8
TOP_K = 2
D_FF_EXPERT = 3584
N_DENSE = (DEPTH + 1) // 2
N_MOE = DEPTH // 2

kernel_name = 'hybrid_swa_moba_dilated_moe_trunk'


def rms_norm(x, gain):
    xf = x.astype(jnp.float32)
    y = xf * lax.rsqrt(jnp.mean(xf * xf, axis=-1, keepdims=True) + RMS_EPS)
    return (y * gain.astype(jnp.float32)).astype(x.dtype)


def rope_tables(t):
    pos = jnp.arange(t, dtype=jnp.float32)
    inv_freq = ROPE_THETA ** (-jnp.arange(0, HEAD_DIM, 2, dtype=jnp.float32) / HEAD_DIM)
    ang = pos[:, None] * inv_freq[None, :]
    return jnp.cos(ang), jnp.sin(ang)


def apply_rope(z, cos, sin):
    zf = z.astype(jnp.float32)
    half = HEAD_DIM // 2
    z1, z2 = zf[..., :half], zf[..., half:]
    c, s = cos[None, :, None, :], sin[None, :, None, :]
    return jnp.concatenate([z1 * c - z2 * s, z2 * c + z1 * s], axis=-1).astype(z.dtype)


def _prev_block(z):
    return jnp.concatenate([jnp.zeros_like(z[:, :1]), z[:, :-1]], axis=1)


def banded_attention(q, k, v, max_dist):
    n, L, hk, g, dh = q.shape
    nb = L // BAND_BLOCK
    qb = q.reshape(n, nb, BAND_BLOCK, hk, g, dh)
    kb = k.reshape(n, nb, BAND_BLOCK, hk, dh)
    vb = v.reshape(n, nb, BAND_BLOCK, hk, dh)
    kk = jnp.concatenate([_prev_block(kb), kb], axis=2)
    vv = jnp.concatenate([_prev_block(vb), vb], axis=2)
    s = jnp.einsum('nbqhgd,nbkhd->nbhgqk', qb, kk,
                   preferred_element_type=jnp.float32) * ATTN_SCALE
    q_off = jnp.arange(BAND_BLOCK)
    k_off = jnp.arange(2 * BAND_BLOCK) - BAND_BLOCK
    diff = q_off[:, None] - k_off[None, :]
    in_band = (diff >= 0) & (diff <= max_dist)
    k_real = (jnp.arange(nb)[:, None] * BAND_BLOCK + k_off[None, :]) >= 0
    mask = in_band[None, :, :] & k_real[:, None, :]
    s = jnp.where(mask[None, :, None, None], s, NEG_INF)
    m = jnp.max(s, axis=-1)
    p = jnp.exp(s - m[..., None])
    l = jnp.sum(p, axis=-1)
    acc = jnp.einsum('nbhgqk,nbkhd->nbhgqd', p.astype(v.dtype), vv,
                     preferred_element_type=jnp.float32)
    acc = acc.transpose(0, 1, 4, 2, 3, 5).reshape(n, L, hk, g, dh)
    m = m.transpose(0, 1, 4, 2, 3).reshape(n, L, hk, g)
    l = l.transpose(0, 1, 4, 2, 3).reshape(n, L, hk, g)
    return acc, m, l


def sliding_window_sink_attention(q, k, v, sinks):
    b, t = q.shape[:2]
    qg = q.reshape(b, t, SWA_KV_HEADS, SWA_GROUP, HEAD_DIM)
    acc, m, l = banded_attention(qg, k, v, SWA_WINDOW - 1)
    sink = sinks.astype(jnp.float32).reshape(SWA_KV_HEADS, SWA_GROUP)
    m_all = jnp.maximum(m, sink)
    keep = jnp.exp(m - m_all)
    den = l * keep + jnp.exp(sink - m_all)
    out = acc * (keep / den)[..., None]
    return out.reshape(b, t, WIDTH_A).astype(q.dtype)


def moba_attention(q, k, v):
    b, t, h, dh = q.shape
    nb = -(-t // MOBA_BLOCK)
    t_pad = nb * MOBA_BLOCK
    pad = ((0, 0), (0, t_pad - t), (0, 0), (0, 0))
    qp, kp, vp = jnp.pad(q, pad), jnp.pad(k, pad), jnp.pad(v, pad)
    kblk = kp.reshape(b, nb, MOBA_BLOCK, h, dh).transpose(0, 3, 1, 2, 4)
    vblk = vp.reshape(b, nb, MOBA_BLOCK, h, dh).transpose(0, 3, 1, 2, 4)
    kmean = jnp.mean(kblk.astype(jnp.float32), axis=3)
    n_chunks = t_pad // MOBA_Q_CHUNK
    q_chunks = qp.reshape(b, n_chunks, MOBA_Q_CHUNK, h, dh).transpose(1, 0, 3, 2, 4)
    n_sel = min(MOBA_TOPK, nb)
    b_ix = jnp.arange(b)[:, None, None, None]
    h_ix = jnp.arange(h)[None, :, None, None]
    key_off = jnp.arange(MOBA_BLOCK)

    def one_chunk(args):
        qc, ci = args
        q_pos = ci * MOBA_Q_CHUNK + jnp.arange(MOBA_Q_CHUNK)
        own = (ci * MOBA_Q_CHUNK) // MOBA_BLOCK
        gate = jnp.einsum('bhqd,bhnd->bhqn', qc.astype(jnp.float32), kmean)
        gate = jnp.where(jnp.arange(nb) < own, gate, NEG_INF)
        _, sel = lax.top_k(gate, n_sel)
        slot_ok = jnp.arange(n_sel) < jnp.minimum(own, n_sel)
        k_sel = kblk[b_ix, h_ix, sel]
        v_sel = vblk[b_ix, h_ix, sel]
        s_sel = jnp.einsum('bhqd,bhqnkd->bhqnk', qc, k_sel,
                           preferred_element_type=jnp.float32) * ATTN_SCALE
        s_sel = jnp.where(slot_ok[:, None], s_sel, NEG_INF)
        s_sel = s_sel.reshape(b, h, MOBA_Q_CHUNK, n_sel * MOBA_BLOCK)
        k_own = lax.dynamic_index_in_dim(kblk, own, axis=2, keepdims=False)
        v_own = lax.dynamic_index_in_dim(vblk, own, axis=2, keepdims=False)
        s_own = jnp.einsum('bhqd,bhkd->bhqk', qc, k_own,
                           preferred_element_type=jnp.float32) * ATTN_SCALE
        causal = (own * MOBA_BLOCK + key_off)[None, :] <= q_pos[:, None]
        s_own = jnp.where(causal, s_own, NEG_INF)
        p = jax.nn.softmax(jnp.concatenate([s_sel, s_own], axis=-1), axis=-1).astype(v.dtype)
        p_sel = p[..., :n_sel * MOBA_BLOCK].reshape(b, h, MOBA_Q_CHUNK, n_sel, MOBA_BLOCK)
        p_own = p[..., n_sel * MOBA_BLOCK:]
        out = (jnp.einsum('bhqnk,bhqnkd->bhqd', p_sel, v_sel, preferred_element_type=jnp.float32)
               + jnp.einsum('bhqk,bhkd->bhqd', p_own, v_own, preferred_element_type=jnp.float32))
        return out.astype(v.dtype)

    out = lax.map(one_chunk, (q_chunks, jnp.arange(n_chunks)))
    out = out.transpose(1, 0, 3, 2, 4).reshape(b, t_pad, h * dh)
    return out[:, :t]


def _to_strided(z, dil, lr):
    b, t, h, dh = z.shape
    z = jnp.pad(z, ((0, 0), (0, lr * dil - t), (0, 0), (0, 0)))
    return z.reshape(b, lr, dil, h, dh).transpose(0, 2, 1, 3, 4).reshape(b * dil, lr, h, dh)


def _from_strided(z, b, dil, lr, t):
    z = z.reshape((b, dil, lr) + z.shape[2:])
    z = jnp.swapaxes(z, 1, 2)
    return z.reshape((b, lr * dil) + z.shape[3:])[:, :t]


def dilated_attention(q, k, v):
    b, t, h, dh = q.shape
    accs, ms, ls = [], [], []
    for window, dil in DIL_PATTERNS:
        lr = -(-t // (dil * BAND_BLOCK)) * BAND_BLOCK
        acc, m, l = banded_attention(_to_strided(q, dil, lr)[:, :, :, None],
                                     _to_strided(k, dil, lr), _to_strided(v, dil, lr),
                                     window // dil)
        accs.append(_from_strided(acc[:, :, :, 0], b, dil, lr, t))
        ms.append(_from_strided(m[..., 0], b, dil, lr, t))
        ls.append(_from_strided(l[..., 0], b, dil, lr, t))
    m_max = ms[0]
    for m in ms[1:]:
        m_max = jnp.maximum(m_max, m)
    num = jnp.zeros_like(accs[0])
    den = jnp.zeros_like(ls[0])
    for acc, m, l in zip(accs, ms, ls):
        e = jnp.exp(m - m_max)
        num = num + acc * e[..., None]
        den = den + l * e
    return (num / den[..., None]).reshape(b, t, WIDTH_C).astype(q.dtype)


def swiglu(h, w_gate, w_up, w_down):
    a = jnp.einsum('btd,df->btf', h, w_gate)
    u = jnp.einsum('btd,df->btf', h, w_up)
    return jnp.einsum('btf,fd->btd', jax.nn.silu(a) * u, w_down)


def moe_swiglu(h, w_router, w_gate, w_up, w_down):
    logits = jnp.einsum('btd,de->bte', h, w_router, preferred_element_type=jnp.float32)
    top_val, top_idx = lax.top_k(logits, TOP_K)
    top_w = jax.nn.softmax(top_val, axis=-1)
    combine = jnp.sum(jax.nn.one_hot(top_idx, N_EXPERTS, dtype=jnp.float32)
                      * top_w[..., None], axis=-2).astype(h.dtype)
    out = jnp.zeros_like(h)
    for e in range(N_EXPERTS):
        out = out + combine[..., e:e + 1] * swiglu(h, w_gate[e], w_up[e], w_down[e])
    return out


def setup_inputs(seed: int = 0) -> dict:
    key = jax.random.key(seed)
    ks = jax.random.split(key, 18)

    def dense(k, shape, fan_in):
        return jax.random.normal(k, shape, jnp.float32) * fan_in ** -0.5

    def gain(k, shape):
        return 1.0 + 0.02 * jax.random.normal(k, shape, jnp.float32)

    return {
        'x': jax.random.normal(ks[0], (BATCH, SEQ, D_MODEL), jnp.float32),
        'norm_mix': gain(ks[1], (DEPTH, D_MODEL)),
        'w_in': dense(ks[2], (DEPTH, D_MODEL, IN_COLS), D_MODEL),
        'attn_sinks': 0.5 * jax.random.normal(ks[3], (DEPTH, SWA_Q_HEADS), jnp.float32),
        'w_br_a': dense(ks[4], (DEPTH, WIDTH_A, D_MODEL), WIDTH_A),
        'w_br_b': dense(ks[5], (DEPTH, WIDTH_B, D_MODEL), WIDTH_B),
        'w_br_c': dense(ks[6], (DEPTH, WIDTH_C, D_MODEL), WIDTH_C),
        'w_out': dense(ks[7], (DEPTH, D_MODEL, D_MODEL), D_MODEL),
        'norm_ffn': gain(ks[8], (DEPTH, D_MODEL)),
        'w_ff_gate': dense(ks[9], (N_DENSE, D_MODEL, D_FF_DENSE), D_MODEL),
        'w_ff_up': dense(ks[10], (N_DENSE, D_MODEL, D_FF_DENSE), D_MODEL),
        'w_ff_down': dense(ks[11], (N_DENSE, D_FF_DENSE, D_MODEL), D_FF_DENSE),
        'w_router': dense(ks[12], (N_MOE, D_MODEL, N_EXPERTS), D_MODEL),
        'w_moe_gate': dense(ks[13], (N_MOE, N_EXPERTS, D_MODEL, D_FF_EXPERT), D_MODEL),
        'w_moe_up': dense(ks[14], (N_MOE, N_EXPERTS, D_MODEL, D_FF_EXPERT), D_MODEL),
        'w_moe_down': dense(ks[15], (N_MOE, N_EXPERTS, D_FF_EXPERT, D_MODEL), D_FF_EXPERT),
        'norm_final': gain(ks[16], (D_MODEL,)),
    }


def reference(x, norm_mix, w_in, attn_sinks, w_br_a, w_br_b, w_br_c, w_out, norm_ffn,
              w_ff_gate, w_ff_up, w_ff_down, w_router, w_moe_gate, w_moe_up, w_moe_down,
              norm_final):
    b, t, _ = x.shape
    cos, sin = rope_tables(t)

    def heads(z):
        return z.reshape(b, t, -1, HEAD_DIM)

    for layer in range(DEPTH):
        h = rms_norm(x, norm_mix[layer])
        proj = jnp.einsum('btd,dc->btc', h, w_in[layer])
        qa, ka, va, qb, kb, vb, qc, kc, vc, gate_logits = jnp.split(proj, SPLIT_POINTS, axis=-1)
        qa, ka, va = apply_rope(heads(qa), cos, sin), apply_rope(heads(ka), cos, sin), heads(va)
        qb, kb, vb = apply_rope(heads(qb), cos, sin), apply_rope(heads(kb), cos, sin), heads(vb)
        qc, kc, vc = apply_rope(heads(qc), cos, sin), apply_rope(heads(kc), cos, sin), heads(vc)

        y_a = jnp.einsum('btc,cd->btd', sliding_window_sink_attention(qa, ka, va, attn_sinks[layer]), w_br_a[layer])
        y_b = jnp.einsum('btc,cd->btd', moba_attention(qb, kb, vb), w_br_b[layer])
        y_c = jnp.einsum('btc,cd->btd', dilated_attention(qc, kc, vc), w_br_c[layer])

        gates = jax.nn.sigmoid(gate_logits.astype(jnp.float32)).astype(x.dtype)
        gates = gates.reshape(b, t, N_BRANCHES, D_MODEL)
        merged = gates[:, :, 0] * y_a + gates[:, :, 1] * y_b + gates[:, :, 2] * y_c
        x = x + jnp.einsum('btd,de->bte', merged, w_out[layer])

        h = rms_norm(x, norm_ffn[layer])
        idx = layer // 2
        if layer % 2 == 0:
            x = x + swiglu(h, w_ff_gate[idx], w_ff_up[idx], w_ff_down[idx])
        else:
            x = x + moe_swiglu(h, w_router[idx], w_moe_gate[idx], w_moe_up[idx], w_moe_down[idx])

    return rms_norm(x, norm_final)
```

```python
import functools

import jax
import jax.numpy as jnp
import numpy as np
from jax import lax
from jax.experimental import pallas as pl
from jax.experimental.pallas import tpu as pltpu

F32 = jnp.float32
BF16 = jnp.bfloat16

HEAD_DIM = 64
ATTN_SCALE = HEAD_DIM ** -0.5
ROPE_THETA = 10000.0
RMS_EPS = 1e-5
NEG_INF = -1e30
BAND = 128
SWA_Q_HEADS, SWA_KV_HEADS = 6, 2
SWA_GROUP = SWA_Q_HEADS // SWA_KV_HEADS
SWA_MAX_DIST = 127
MOBA_HEADS, MOBA_BLOCK, MOBA_TOPK = 4, 256, 3
DIL_HEADS = 6
DIL_PATTERNS = ((128, 1), (512, 4), (2048, 16))
DIL_MAX_DIST = 128
N_EXPERTS, TOP_K = 8, 2

LANES = 128
VMEM_LIMIT = 56 * 1024 * 1024

WIDTH_A = SWA_Q_HEADS * HEAD_DIM
KV_A = SWA_KV_HEADS * HEAD_DIM
WIDTH_B = MOBA_HEADS * HEAD_DIM
WIDTH_C = DIL_HEADS * HEAD_DIM
QKV_COLS = WIDTH_A + 2 * KV_A + 3 * WIDTH_B + 3 * WIDTH_C


def _params(*sem):
    return pltpu.CompilerParams(dimension_semantics=sem, vmem_limit_bytes=VMEM_LIMIT)


def _rms(x, gain):
    ms = jnp.mean(x * x, axis=-1, keepdims=True)
    return x * lax.rsqrt(ms + RMS_EPS) * gain


def _dot(a, b):
    return jnp.dot(a, b, preferred_element_type=F32)


def _dot_t(a, b):
    return lax.dot_general(a, b, (((1,), (1,)), ((), ())), preferred_element_type=F32)


def _sigmoid(z):
    return 1.0 / (1.0 + jnp.exp(-z))


def _lane(shape):
    return lax.broadcasted_iota(jnp.int32, shape, len(shape) - 1)


def _inproj_kernel(x_ref, g_ref, w_ref, cos_ref, sin_ref,
                   qa_ref, ka_ref, va_ref, qb_ref, kb_ref, vb_ref, qc_ref, kc_ref, vc_ref):
    tm = x_ref.shape[0]
    h = _rms(x_ref[...], g_ref[...]).astype(BF16)
    cos = cos_ref[...]
    sin = sin_ref[...]
    first_half = (_lane((tm, LANES)) & (HEAD_DIM // 2)) == 0

    def rope(z):
        rot = jnp.where(first_half,
                        pltpu.roll(z, LANES - HEAD_DIM // 2, 1),
                        pltpu.roll(z, HEAD_DIM // 2, 1))
        return z * cos + rot * sin

    def proj(c0, width):
        return _dot(h, w_ref[:, c0:c0 + width])

    def store(ref, z, roped, scale, dtype):
        for t in range(z.shape[1] // LANES):
            zt = z[:, t * LANES:(t + 1) * LANES]
            if roped:
                zt = rope(zt)
            if scale:
                zt = zt * ATTN_SCALE
            ref[:, t * LANES:(t + 1) * LANES] = zt.astype(dtype)

    c = 0
    store(qa_ref, proj(c, WIDTH_A), True, True, BF16); c += WIDTH_A
    kv = proj(c, 2 * KV_A); c += 2 * KV_A
    store(ka_ref, kv[:, :KV_A], True, False, BF16)
    store(va_ref, kv[:, KV_A:], False, False, BF16)
    store(qb_ref, proj(c, WIDTH_B), True, True, BF16); c += WIDTH_B
    store(kb_ref, proj(c, WIDTH_B), True, False, BF16); c += WIDTH_B
    store(vb_ref, proj(c, WIDTH_B), False, False, BF16); c += WIDTH_B
    store(qc_ref, proj(c, WIDTH_C), True, True, F32); c += WIDTH_C
    store(kc_ref, proj(c, WIDTH_C), True, False, F32); c += WIDTH_C
    store(vc_ref, proj(c, WIDTH_C), False, False, F32)


def _inproj(x2, gain, w_qkv, cos_t, sin_t, seq, tm):
    n, d = x2.shape
    tiles_per_seq = seq // tm
    row = lambda i: (i, 0)
    widths = (WIDTH_A, KV_A, KV_A, WIDTH_B, WIDTH_B, WIDTH_B, WIDTH_C, WIDTH_C, WIDTH_C)
    dtypes = (BF16,) * 6 + (F32,) * 3
    return pl.pallas_call(
        _inproj_kernel,
        grid=(n // tm,),
        in_specs=[
            pl.BlockSpec((tm, d), row),
            pl.BlockSpec((1, d), lambda i: (0, 0)),
            pl.BlockSpec((d, QKV_COLS), lambda i: (0, 0)),
            pl.BlockSpec((tm, LANES), lambda i: (i % tiles_per_seq, 0)),
            pl.BlockSpec((tm, LANES), lambda i: (i % tiles_per_seq, 0)),
        ],
        out_specs=[pl.BlockSpec((tm, w), row) for w in widths],
        out_shape=[jax.ShapeDtypeStruct((n, w), dt) for w, dt in zip(widths, dtypes)],
        compiler_params=_params("parallel"),
        name="inproj",
    )(x2, gain, w_qkv, cos_t, sin_t)


def _band_mask(max_dist, block_start, stack=1):
    qi = lax.broadcasted_iota(jnp.int32, (stack * BAND, 2 * BAND), 0) % BAND
    kj = lax.broadcasted_iota(jnp.int32, (stack * BAND, 2 * BAND), 1) - BAND
    diff = qi - kj
    return (diff >= 0) & (diff <= max_dist) & (kj + block_start >= 0)


def _split_heads(a):
    a32 = a.astype(F32)
    lo = _lane(a.shape) < HEAD_DIM
    return jnp.where(lo, a32, 0.0).astype(BF16), jnp.where(lo, 0.0, a32).astype(BF16)


def _pair_scores(q, k, mask):
    out = []
    for kh in _split_heads(k):
        s = jnp.where(mask, _dot_t(q, kh), NEG_INF)
        m = jnp.max(s, axis=-1, keepdims=True)
        p = jnp.exp(s - m)
        out.append((m, p, jnp.sum(p, axis=-1, keepdims=True)))
    return out


def _pair_pv(p0, p1, v):
    pc = jnp.concatenate([p0.astype(BF16), p1.astype(BF16)], axis=1)
    return _dot(pc, jnp.concatenate(_split_heads(v), axis=0))


def _by_head(lo_val, hi_val, shape):
    return jnp.where(_lane(shape) < HEAD_DIM, lo_val, hi_val)


def _swa_kernel(sink_ref, q_ref, k_ref, kp_ref, v_ref, vp_ref, o_ref):
    nblk = q_ref.shape[0] // BAND
    chunk_start = pl.program_id(1) * q_ref.shape[0]
    n_pairs = WIDTH_A // LANES
    for j in range(nblk):
        rows = slice(j * BAND, (j + 1) * BAND)
        prev = slice((j - 1) * BAND, j * BAND)
        q = q_ref[rows, :]
        qs = jnp.concatenate([q[:, t * LANES:(t + 1) * LANES] for t in range(n_pairs)], axis=0)
        if j == 0:
            k = jnp.concatenate([kp_ref[...], k_ref[rows, :]], axis=0)
            v = jnp.concatenate([vp_ref[...], v_ref[rows, :]], axis=0)
        else:
            k = jnp.concatenate([k_ref[prev, :], k_ref[rows, :]], axis=0)
            v = jnp.concatenate([v_ref[prev, :], v_ref[rows, :]], axis=0)
        mask = _band_mask(SWA_MAX_DIST, chunk_start + j * BAND, n_pairs)
        stats = _pair_scores(qs, k, mask)
        acc = _pair_pv(stats[0][1], stats[1][1], v)
        scales = []
        for g, (m, _, l) in enumerate(stats):
            sink = jnp.concatenate(
                [jnp.full((BAND, 1), sink_ref[SWA_GROUP * g + t], F32) for t in range(n_pairs)],
                axis=0)
            m_all = jnp.maximum(m, sink)
            keep = jnp.exp(m - m_all)
            den = l * keep + jnp.exp(sink - m_all)
            scales.append(keep / den)
        out = acc * _by_head(scales[0], scales[1], acc.shape)
        for t in range(n_pairs):
            o_ref[rows, t * LANES:(t + 1) * LANES] = out[t * BAND:(t + 1) * BAND, :].astype(o_ref.dtype)


def _swa(sinks, qa, ka, va, batch, seq, tq):
    n = qa.shape[0]
    chunks = seq // tq
    blk = tq // BAND
    cur = lambda b, c: (b * chunks + c, 0)
    prev = lambda b, c: (jnp.maximum((b * chunks + c) * blk - 1, 0), 0)
    return pl.pallas_call(
        _swa_kernel,
        grid=(batch, chunks),
        in_specs=[
            pl.BlockSpec(memory_space=pltpu.SMEM),
            pl.BlockSpec((tq, WIDTH_A), cur),
            pl.BlockSpec((tq, KV_A), cur),
            pl.BlockSpec((BAND, KV_A), prev),
            pl.BlockSpec((tq, KV_A), cur),
            pl.BlockSpec((BAND, KV_A), prev),
        ],
        out_specs=pl.BlockSpec((tq, WIDTH_A), cur),
        out_shape=jax.ShapeDtypeStruct((n, WIDTH_A), BF16),
        compiler_params=_params("parallel", "parallel"),
        name="swa",
    )(sinks, qa, ka, ka, va, va)


def _moba_kernel(q_ref, k_ref, v_ref, o_ref, kaug_ref, vm_ref, kmean_ref, acc_ref, m_ref, l_ref):
    i = pl.program_id(2)
    nb = k_ref.shape[0] // MOBA_BLOCK
    blk_shape = (MOBA_BLOCK, LANES)
    lane = _lane(blk_shape)
    lo = lane < HEAD_DIM

    @pl.when(i == 0)
    def _prepare():
        kmean_ref[...] = jnp.zeros_like(kmean_ref)

        def body(n, carry):
            rows = pl.ds(pl.multiple_of(n * MOBA_BLOCK, MOBA_BLOCK), MOBA_BLOCK)
            kb = k_ref[rows, :].astype(F32)
            kmean_ref[pl.ds(n, 1), :] = jnp.sum(kb, axis=0, keepdims=True) * (1.0 / MOBA_BLOCK)
            block_hot = jnp.where((lane == n) | (lane == HEAD_DIM + n), 1.0, 0.0)
            kaug_ref[0, rows, :] = jnp.where(lo, kb, block_hot).astype(BF16)
            kaug_ref[1, rows, :] = jnp.where(lo, block_hot, kb).astype(BF16)
            v0, v1 = _split_heads(v_ref[rows, :])
            vm_ref[0, rows, :] = v0
            vm_ref[1, rows, :] = v1
            return carry

        lax.fori_loop(0, nb, body, 0)

    q = q_ref[...]
    kmean = kmean_ref[...]
    zero_m = jnp.zeros_like(kmean)
    lo_m = _lane(kmean.shape) < HEAD_DIM
    neg = jnp.full(blk_shape, NEG_INF, F32)
    pens = []
    for km in (jnp.where(lo_m, kmean, zero_m), jnp.where(lo_m, zero_m, kmean)):
        gate = _dot_t(q, km.astype(BF16))
        gate = jnp.where(lane < i, gate, -jnp.inf)
        sel = lane == i
        for _ in range(MOBA_TOPK):
            mx = jnp.max(gate, axis=-1, keepdims=True)
            first = jnp.min(jnp.where(gate == mx, lane, LANES), axis=-1, keepdims=True)
            pick = (lane == first) & (mx > -jnp.inf)
            sel = sel | pick
            gate = jnp.where(pick, -jnp.inf, gate)
        pens.append(jnp.where(sel, 0.0, neg))
    q32 = q.astype(F32)
    qa0 = jnp.where(lo, q32, pltpu.roll(pens[0], HEAD_DIM, 1)).astype(BF16)
    qa1 = jnp.where(lo, pens[1], q32).astype(BF16)

    acc_ref[...] = jnp.zeros_like(acc_ref)
    m_ref[...] = jnp.full_like(m_ref, NEG_INF)
    l_ref[...] = jnp.zeros_like(l_ref)

    def attend(j, causal):
        rows = pl.ds(pl.multiple_of(j * MOBA_BLOCK, MOBA_BLOCK), MOBA_BLOCK)
        ps, alphas = [], []
        for hd, qa in enumerate((qa0, qa1)):
            s = _dot_t(qa, kaug_ref[hd, rows, :])
            if causal:
                qi = lax.broadcasted_iota(jnp.int32, s.shape, 0)
                kj = lax.broadcasted_iota(jnp.int32, s.shape, 1)
                s = jnp.where(kj <= qi, s, NEG_INF)
            m_old = m_ref[hd]
            m_new = jnp.maximum(m_old, jnp.max(s, axis=-1, keepdims=True))
            alpha = jnp.exp(m_old - m_new)
            p = jnp.exp(s - m_new)
            l_ref[hd] = alpha * l_ref[hd] + jnp.sum(p, axis=-1, keepdims=True)
            m_ref[hd] = m_new
            ps.append(p.astype(BF16))
            alphas.append(alpha)
        pv = _dot(jnp.concatenate(ps, axis=1),
                  jnp.concatenate([vm_ref[0, rows, :], vm_ref[1, rows, :]], axis=0))
        acc_ref[...] = acc_ref[...] * _by_head(alphas[0], alphas[1], blk_shape) + pv

    def body(j, carry):
        attend(j, False)
        return carry

    lax.fori_loop(0, i, body, 0)
    attend(i, True)
    inv = _by_head(1.0 / l_ref[0], 1.0 / l_ref[1], blk_shape)
    o_ref[...] = (acc_ref[...] * inv).astype(o_ref.dtype)


def _moba(qb, kb, vb, batch, seq):
    n = qb.shape[0]
    nb = seq // MOBA_BLOCK
    pairs = WIDTH_B // LANES
    assert nb <= HEAD_DIM, "block one-hot must fit in the other head's lanes"
    qmap = lambda b, p, i: (b * nb + i, p)
    kvmap = lambda b, p, i: (b, p)
    return pl.pallas_call(
        _moba_kernel,
        grid=(batch, pairs, nb),
        in_specs=[
            pl.BlockSpec((MOBA_BLOCK, LANES), qmap),
            pl.BlockSpec((seq, LANES), kvmap),
            pl.BlockSpec((seq, LANES), kvmap),
        ],
        out_specs=pl.BlockSpec((MOBA_BLOCK, LANES), qmap),
        out_shape=jax.ShapeDtypeStruct((n, WIDTH_B), BF16),
        scratch_shapes=[
            pltpu.VMEM((2, seq, LANES), BF16),
            pltpu.VMEM((2, seq, LANES), BF16),
            pltpu.VMEM((LANES, LANES), F32),
            pltpu.VMEM((MOBA_BLOCK, LANES), F32),
            pltpu.VMEM((2, MOBA_BLOCK, 1), F32),
            pltpu.VMEM((2, MOBA_BLOCK, 1), F32),
        ],
        compiler_params=_params("parallel", "parallel", "arbitrary"),
        name="moba",
    )(qb, kb, vb)


DIL_CHUNK = DIL_PATTERNS[-1][1] * BAND


def _dil_kernel(q_ref, kp_ref, k_ref, vp_ref, v_ref, o_ref, kk_ref, vv_ref, acc_ref, m_ref, l_ref):
    c = pl.program_id(2)
    kk_ref[0:DIL_CHUNK, :] = kp_ref[...]
    kk_ref[DIL_CHUNK:, :] = k_ref[...]
    vv_ref[0:DIL_CHUNK, :] = vp_ref[...]
    vv_ref[DIL_CHUNK:, :] = v_ref[...]
    blocks = DIL_CHUNK // BAND
    shape = (BAND, LANES)

    for pi, (_, d) in enumerate(DIL_PATTERNS):
        per_class = blocks // d

        def body(t, carry, pi=pi, d=d, per_class=per_class):
            r = t // per_class
            j = t % per_class
            q0 = r + BAND * d * j
            k0 = DIL_CHUNK + q0 - BAND * d
            if d == 1:
                qrows = pl.ds(pl.multiple_of(q0, BAND), BAND)
                krows = pl.ds(pl.multiple_of(k0, BAND), 2 * BAND)
            else:
                qrows = pl.ds(q0, BAND, stride=d)
                krows = pl.ds(k0, 2 * BAND, stride=d)
            q = q_ref[qrows, :].astype(BF16)
            k = kk_ref[krows, :].astype(BF16)
            v = vv_ref[krows, :].astype(BF16)
            mask = _band_mask(DIL_MAX_DIST, (c * per_class + j) * BAND)
            (m0, p0, l0), (m1, p1, l1) = _pair_scores(q, k, mask)
            acc_ref[pi, qrows, :] = _pair_pv(p0, p1, v)
            m_ref[pi, qrows, :] = _by_head(m0, m1, shape)
            l_ref[pi, qrows, :] = _by_head(l0, l1, shape)
            return carry

        lax.fori_loop(0, blocks, body, 0)

    def combine(t, carry):
        rows = pl.ds(pl.multiple_of(t * BAND, BAND), BAND)
        ms = [m_ref[pi, rows, :] for pi in range(len(DIL_PATTERNS))]
        m_max = functools.reduce(jnp.maximum, ms)
        num = jnp.zeros(shape, F32)
        den = jnp.zeros(shape, F32)
        for pi, m in enumerate(ms):
            e = jnp.exp(m - m_max)
            num = num + acc_ref[pi, rows, :] * e
            den = den + l_ref[pi, rows, :] * e
        o_ref[rows, :] = (num / den).astype(o_ref.dtype)
        return carry

    lax.fori_loop(0, blocks, combine, 0)


def _dilated(qc, kc, vc, batch, seq):
    n = qc.shape[0]
    assert seq % DIL_CHUNK == 0
    chunks = seq // DIL_CHUNK
    pairs = WIDTH_C // LANES
    cur = lambda b, p, c: (b * chunks + c, p)
    prev = lambda b, p, c: (b * chunks + jnp.maximum(c - 1, 0), p)
    blk = (DIL_CHUNK, LANES)
    n_pat = len(DIL_PATTERNS)
    return pl.pallas_call(
        _dil_kernel,
        grid=(batch, pairs, chunks),
        in_specs=[pl.BlockSpec(blk, cur), pl.BlockSpec(blk, prev), pl.BlockSpec(blk, cur),
                  pl.BlockSpec(blk, prev), pl.BlockSpec(blk, cur)],
        out_specs=pl.BlockSpec(blk, cur),
        out_shape=jax.ShapeDtypeStruct((n, WIDTH_C), BF16),
        scratch_shapes=[
            pltpu.VMEM((2 * DIL_CHUNK, LANES), F32),
            pltpu.VMEM((2 * DIL_CHUNK, LANES), F32),
            pltpu.VMEM((n_pat, DIL_CHUNK, LANES), F32),
            pltpu.VMEM((n_pat, DIL_CHUNK, LANES), F32),
            pltpu.VMEM((n_pat, DIL_CHUNK, LANES), F32),
        ],
        compiler_params=_params("parallel", "parallel", "parallel"),
        name="dilated",
    )(qc, kc, kc, vc, vc)


def _merge_kernel(x_ref, g_ref, oa_ref, ob_ref, oc_ref, wg_ref, wa_ref, wb_ref, wc_ref, wo_ref, y_ref):
    x = x_ref[...]
    d = x.shape[1]
    h = _rms(x, g_ref[...]).astype(BF16)
    merged = None
    for br, (o_ref, w_ref) in enumerate(((oa_ref, wa_ref), (ob_ref, wb_ref), (oc_ref, wc_ref))):
        gate = _sigmoid(_dot(h, wg_ref[:, br * d:(br + 1) * d]))
        y = gate * _dot(o_ref[...], w_ref[...])
        merged = y if merged is None else merged + y
    y_ref[...] = x + _dot(merged.astype(BF16), wo_ref[...])


def _merge(x2, gain, oa, ob, oc, w_gates, w_a, w_b, w_c, w_o, tm):
    n, d = x2.shape
    row = lambda i: (i, 0)
    full = lambda i: (0, 0)
    return pl.pallas_call(
        _merge_kernel,
        grid=(n // tm,),
        in_specs=[
            pl.BlockSpec((tm, d), row),
            pl.BlockSpec((1, d), full),
            pl.BlockSpec((tm, WIDTH_A), row),
            pl.BlockSpec((tm, WIDTH_B), row),
            pl.BlockSpec((tm, WIDTH_C), row),
            pl.BlockSpec(w_gates.shape, full),
            pl.BlockSpec(w_a.shape, full),
            pl.BlockSpec(w_b.shape, full),
            pl.BlockSpec(w_c.shape, full),
            pl.BlockSpec(w_o.shape, full),
        ],
        out_specs=pl.BlockSpec((tm, d), row),
        out_shape=jax.ShapeDtypeStruct((n, d), F32),
        compiler_params=_params("parallel"),
        name="merge",
    )(x2, gain, oa, ob, oc, w_gates, w_a, w_b, w_c, w_o)


def _swiglu(h, wg, wu, wd):
    a = _dot(h, wg)
    u = _dot(h, wu)
    return _dot((a * _sigmoid(a) * u).astype(BF16), wd)


def _ffn_kernel(x_ref, g_ref, wg_ref, wu_ref, wd_ref, y_ref):
    x = x_ref[...]
    h = _rms(x, g_ref[...]).astype(BF16)
    y_ref[...] = x + _swiglu(h, wg_ref[...], wu_ref[...], wd_ref[...])


def _ffn(x2, gain, wg, wu, wd, tm):
    n, d = x2.shape
    row = lambda i: (i, 0)
    full = lambda i: (0, 0)
    return pl.pallas_call(
        _ffn_kernel,
        grid=(n // tm,),
        in_specs=[
            pl.BlockSpec((tm, d), row),
            pl.BlockSpec((1, d), full),
            pl.BlockSpec(wg.shape, full),
            pl.BlockSpec(wu.shape, full),
            pl.BlockSpec(wd.shape, full),
        ],
        out_specs=pl.BlockSpec((tm, d), row),
        out_shape=jax.ShapeDtypeStruct((n, d), F32),
        compiler_params=_params("parallel"),
        name="ffn",
    )(x2, gain, wg, wu, wd)


def _split_bf16(a):
    hi = a.astype(BF16)
    return hi, (a - hi.astype(F32)).astype(BF16)


def _router_kernel(x_ref, g_ref, wr_ref, h_ref, comb_ref):
    h = _rms(x_ref[...], g_ref[...])
    h_ref[...] = h.astype(BF16)
    h_hi, h_lo = _split_bf16(h)
    w_hi, w_lo = _split_bf16(wr_ref[...])
    logits = _dot(h_hi, w_hi) + (_dot(h_hi, w_lo) + _dot(h_lo, w_hi))
    lane = _lane(logits.shape)
    logits = jnp.where(lane < N_EXPERTS, logits, -jnp.inf)
    tops = []
    for _ in range(TOP_K):
        mx = jnp.max(logits, axis=-1, keepdims=True)
        first = jnp.min(jnp.where(logits == mx, lane, LANES), axis=-1, keepdims=True)
        tops.append((mx, first))
        logits = jnp.where(lane == first, -jnp.inf, logits)
    (v1, i1), (v2, i2) = tops
    e2 = jnp.exp(v2 - v1)
    w1 = 1.0 / (1.0 + e2)
    w2 = e2 / (1.0 + e2)
    for e in range(N_EXPERTS):
        col = jnp.where(i1 == e, w1, 0.0) + jnp.where(i2 == e, w2, 0.0)
        comb_ref[e] = jnp.broadcast_to(col, logits.shape)


def _router(x2, gain, w_router_pad, tm):
    n, d = x2.shape
    row = lambda i: (i, 0)
    return pl.pallas_call(
        _router_kernel,
        grid=(n // tm,),
        in_specs=[
            pl.BlockSpec((tm, d), row),
            pl.BlockSpec((1, d), lambda i: (0, 0)),
            pl.BlockSpec((d, LANES), lambda i: (0, 0)),
        ],
        out_specs=[pl.BlockSpec((tm, d), row),
                   pl.BlockSpec((N_EXPERTS, tm, LANES), lambda i: (0, i, 0))],
        out_shape=[jax.ShapeDtypeStruct((n, d), BF16),
                   jax.ShapeDtypeStruct((N_EXPERTS, n, LANES), F32)],
        compiler_params=_params("parallel"),
        name="router",
    )(x2, gain, w_router_pad)


def _experts_kernel(acc_ref, h_ref, comb_ref, wg_ref, wu_ref, wd_ref, y_ref):
    y = _swiglu(h_ref[...], wg_ref[0], wu_ref[0], wd_ref[0])
    y_ref[...] = acc_ref[...] + comb_ref[0][:, :1] * y


def _experts(x2, hb, comb, wg, wu, wd, tm, ff_split):
    n, d = x2.shape
    n_e, _, d_ff = wg.shape
    ff = d_ff // ff_split
    row = lambda e, f, i: (i, 0)
    return pl.pallas_call(
        _experts_kernel,
        grid=(n_e, ff_split, n // tm),
        in_specs=[
            pl.BlockSpec((tm, d), row),
            pl.BlockSpec((tm, d), row),
            pl.BlockSpec((1, tm, LANES), lambda e, f, i: (e, i, 0)),
            pl.BlockSpec((1, d, ff), lambda e, f, i: (e, 0, f)),
            pl.BlockSpec((1, d, ff), lambda e, f, i: (e, 0, f)),
            pl.BlockSpec((1, ff, d), lambda e, f, i: (e, f, 0)),
        ],
        out_specs=pl.BlockSpec((tm, d), row),
        out_shape=jax.ShapeDtypeStruct((n, d), F32),
        input_output_aliases={0: 0},
        compiler_params=_params("arbitrary", "arbitrary", "arbitrary"),
        name="experts",
    )(x2, hb, comb, wg, wu, wd)


def _norm_kernel(x_ref, g_ref, y_ref):
    y_ref[...] = _rms(x_ref[...], g_ref[...])


def _final_norm(x2, gain, tm):
    n, d = x2.shape
    row = lambda i: (i, 0)
    return pl.pallas_call(
        _norm_kernel,
        grid=(n // tm,),
        in_specs=[pl.BlockSpec((tm, d), row), pl.BlockSpec((1, d), lambda i: (0, 0))],
        out_specs=pl.BlockSpec((tm, d), row),
        out_shape=jax.ShapeDtypeStruct((n, d), F32),
        compiler_params=_params("parallel"),
        name="final_norm",
    )(x2, gain)


def _rope_tables(seq):
    pos = jnp.arange(seq, dtype=F32)
    inv_freq = ROPE_THETA ** (-jnp.arange(0, HEAD_DIM, 2, dtype=F32) / HEAD_DIM)
    ang = pos[:, None] * inv_freq[None, :]
    cos, sin = jnp.cos(ang), jnp.sin(ang)
    reps = LANES // HEAD_DIM
    cos_t = jnp.tile(jnp.concatenate([cos, cos], axis=-1), (1, reps))
    sin_t = jnp.tile(jnp.concatenate([-sin, sin], axis=-1), (1, reps))
    return cos_t, sin_t


def _pair_heads_by_group(w, axis):
    shape = w.shape
    split = shape[:axis] + (SWA_KV_HEADS, SWA_GROUP, HEAD_DIM) + shape[axis + 1:]
    return jnp.swapaxes(w.reshape(split), axis, axis + 1).reshape(shape)


def _row_tile(n, want):
    t = min(n, want)
    assert n % t == 0
    return t


def kernel(x, norm_mix, w_in, attn_sinks, w_br_a, w_br_b, w_br_c, w_out, norm_ffn, w_ff_gate, w_ff_up, w_ff_down, w_router, w_moe_gate, w_moe_up, w_moe_down, norm_final):
    batch, seq, d = x.shape
    n = batch * seq
    depth = norm_mix.shape[0]
    tm = _row_tile(seq, 512)
    cos_t, sin_t = _rope_tables(seq)
    x2 = x.reshape(n, d).astype(F32)

    for layer in range(depth):
        w_l = w_in[layer]
        w_qa = _pair_heads_by_group(w_l[:, :WIDTH_A], 1)
        w_qkv = jnp.concatenate([w_qa, w_l[:, WIDTH_A:QKV_COLS]], axis=1).astype(BF16)
        w_gates = w_l[:, QKV_COLS:].astype(BF16)
        gain = norm_mix[layer].reshape(1, d)

        qa, ka, va, qb, kb, vb, qc, kc, vc = _inproj(x2, gain, w_qkv, cos_t, sin_t, seq, tm)
        oa = _swa(attn_sinks[layer], qa, ka, va, batch, seq, tm)
        ob = _moba(qb, kb, vb, batch, seq)
        oc = _dilated(qc, kc, vc, batch, seq)
        x2 = _merge(x2, gain, oa, ob, oc, w_gates,
                    _pair_heads_by_group(w_br_a[layer], 0).astype(BF16),
                    w_br_b[layer].astype(BF16), w_br_c[layer].astype(BF16),
                    w_out[layer].astype(BF16), tm)

        gain_f = norm_ffn[layer].reshape(1, d)
        idx = layer // 2
        if layer % 2 == 0:
            x2 = _ffn(x2, gain_f, w_ff_gate[idx].astype(BF16), w_ff_up[idx].astype(BF16),
                      w_ff_down[idx].astype(BF16), tm)
        else:
            w_r = jnp.pad(w_router[idx], ((0, 0), (0, LANES - N_EXPERTS)))
            hb, comb = _router(x2, gain_f, w_r, tm)
            x2 = _experts(x2, hb, comb, w_moe_gate[idx].astype(BF16), w_moe_up[idx].astype(BF16),
                          w_moe_down[idx].astype(BF16), tm, 2)

    return _final_norm(x2, norm_final.reshape(1, d), tm).reshape(batch, seq, d)
```

```python
import functools

import jax
import jax.numpy as jnp
import numpy as np
from jax import lax
from jax.experimental import pallas as pl
from jax.experimental.pallas import tpu as pltpu

F32 = jnp.float32
BF16 = jnp.bfloat16

HEAD_DIM = 64
ATTN_SCALE = HEAD_DIM ** -0.5
ROPE_THETA = 10000.0
RMS_EPS = 1e-5
NEG_INF = -1e30
BAND = 128
SWA_Q_HEADS, SWA_KV_HEADS = 6, 2
SWA_GROUP = SWA_Q_HEADS // SWA_KV_HEADS
SWA_MAX_DIST = 127
MOBA_HEADS, MOBA_BLOCK, MOBA_TOPK = 4, 256, 3
DIL_HEADS = 6
DIL_PATTERNS = ((128, 1), (512, 4), (2048, 16))
DIL_MAX_DIST = 128
N_EXPERTS, TOP_K = 8, 2

LANES = 128
VMEM_LIMIT = 56 * 1024 * 1024

WIDTH_A = SWA_Q_HEADS * HEAD_DIM
KV_A = SWA_KV_HEADS * HEAD_DIM
WIDTH_B = MOBA_HEADS * HEAD_DIM
WIDTH_C = DIL_HEADS * HEAD_DIM
QKV_COLS = WIDTH_A + 2 * KV_A + 3 * WIDTH_B + 3 * WIDTH_C


def _params(*sem):
    return pltpu.CompilerParams(dimension_semantics=sem, vmem_limit_bytes=VMEM_LIMIT)


def _rms(x, gain):
    ms = jnp.mean(x * x, axis=-1, keepdims=True)
    return x * lax.rsqrt(ms + RMS_EPS) * gain


def _dot(a, b):
    return jnp.dot(a, b, preferred_element_type=F32)


def _dot_t(a, b):
    return lax.dot_general(a, b, (((1,), (1,)), ((), ())), preferred_element_type=F32)


def _sigmoid(z):
    return 1.0 / (1.0 + jnp.exp(-z))


def _lane(shape):
    return lax.broadcasted_iota(jnp.int32, shape, len(shape) - 1)


def _inproj_kernel(x_ref, g_ref, w_ref, cos_ref, sin_ref,
                   qa_ref, ka_ref, va_ref, qb_ref, kb_ref, vb_ref, qc_ref, kc_ref, vc_ref):
    tm = x_ref.shape[0]
    h = _rms(x_ref[...], g_ref[...]).astype(BF16)
    cos = cos_ref[...]
    sin = sin_ref[...]
    first_half = (_lane((tm, LANES)) & (HEAD_DIM // 2)) == 0

    def rope(z):
        rot = jnp.where(first_half,
                        pltpu.roll(z, LANES - HEAD_DIM // 2, 1),
                        pltpu.roll(z, HEAD_DIM // 2, 1))
        return z * cos + rot * sin

    def proj(c0, width):
        return _dot(h, w_ref[:, c0:c0 + width])

    def store(ref, z, roped, scale, dtype):
        for t in range(z.shape[1] // LANES):
            zt = z[:, t * LANES:(t + 1) * LANES]
            if roped:
                zt = rope(zt)
            if scale:
                zt = zt * ATTN_SCALE
            ref[:, t * LANES:(t + 1) * LANES] = zt.astype(dtype)

    c = 0
    store(qa_ref, proj(c, WIDTH_A), True, True, BF16); c += WIDTH_A
    kv = proj(c, 2 * KV_A); c += 2 * KV_A
    store(ka_ref, kv[:, :KV_A], True, False, BF16)
    store(va_ref, kv[:, KV_A:], False, False, BF16)
    store(qb_ref, proj(c, WIDTH_B), True, True, BF16); c += WIDTH_B
    store(kb_ref, proj(c, WIDTH_B), True, False, BF16); c += WIDTH_B
    store(vb_ref, proj(c, WIDTH_B), False, False, BF16); c += WIDTH_B
    store(qc_ref, proj(c, WIDTH_C), True, True, F32); c += WIDTH_C
    store(kc_ref, proj(c, WIDTH_C), True, False, F32); c += WIDTH_C
    store(vc_ref, proj(c, WIDTH_C), False, False, F32)


def _inproj(x2, gain, w_qkv, cos_t, sin_t, seq, tm):
    n, d = x2.shape
    tiles_per_seq = seq // tm
    row = lambda i: (i, 0)
    widths = (WIDTH_A, KV_A, KV_A, WIDTH_B, WIDTH_B, WIDTH_B, WIDTH_C, WIDTH_C, WIDTH_C)
    dtypes = (BF16,) * 6 + (F32,) * 3
    return pl.pallas_call(
        _inproj_kernel,
        grid=(n // tm,),
        in_specs=[
            pl.BlockSpec((tm, d), row),
            pl.BlockSpec((1, d), lambda i: (0, 0)),
            pl.BlockSpec((d, QKV_COLS), lambda i: (0, 0)),
            pl.BlockSpec((tm, LANES), lambda i: (i % tiles_per_seq, 0)),
            pl.BlockSpec((tm, LANES), lambda i: (i % tiles_per_seq, 0)),
        ],
        out_specs=[pl.BlockSpec((tm, w), row) for w in widths],
        out_shape=[jax.ShapeDtypeStruct((n, w), dt) for w, dt in zip(widths, dtypes)],
        compiler_params=_params("parallel"),
        name="inproj",
    )(x2, gain, w_qkv, cos_t, sin_t)


def _band_mask(max_dist, block_start, stack=1):
    qi = lax.broadcasted_iota(jnp.int32, (stack * BAND, 2 * BAND), 0) % BAND
    kj = lax.broadcasted_iota(jnp.int32, (stack * BAND, 2 * BAND), 1) - BAND
    diff = qi - kj
    return (diff >= 0) & (diff <= max_dist) & (kj + block_start >= 0)


def _split_heads(a):
    a32 = a.astype(F32)
    lo = _lane(a.shape) < HEAD_DIM
    return jnp.where(lo, a32, 0.0).astype(BF16), jnp.where(lo, 0.0, a32).astype(BF16)


def _pair_dots(q, k):
    return [_dot_t(q, kh) for kh in _split_heads(k)]


def _pair_softmax(scores, mask):
    out = []
    for s in scores:
        s = jnp.where(mask, s, NEG_INF)
        m = jnp.max(s, axis=-1, keepdims=True)
        p = jnp.exp(s - m)
        out.append((m, p, jnp.sum(p, axis=-1, keepdims=True)))
    return out


def _pair_pv(p0, p1, v):
    pc = jnp.concatenate([p0.astype(BF16), p1.astype(BF16)], axis=1)
    return _dot(pc, jnp.concatenate(_split_heads(v), axis=0))


def _by_head(lo_val, hi_val, shape):
    return jnp.where(_lane(shape) < HEAD_DIM, lo_val, hi_val)


def _swa_kernel(sink_ref, q_ref, k_ref, kp_ref, v_ref, vp_ref, o_ref):
    nblk = q_ref.shape[0] // BAND
    chunk_start = pl.program_id(1) * q_ref.shape[0]
    n_pairs = WIDTH_A // LANES
    def load(j):
        rows = slice(j * BAND, (j + 1) * BAND)
        prev = slice((j - 1) * BAND, j * BAND)
        q = q_ref[rows, :]
        qs = jnp.concatenate([q[:, t * LANES:(t + 1) * LANES] for t in range(n_pairs)], axis=0)
        if j == 0:
            k = jnp.concatenate([kp_ref[...], k_ref[rows, :]], axis=0)
            v = jnp.concatenate([vp_ref[...], v_ref[rows, :]], axis=0)
        else:
            k = jnp.concatenate([k_ref[prev, :], k_ref[rows, :]], axis=0)
            v = jnp.concatenate([v_ref[prev, :], v_ref[rows, :]], axis=0)
        return rows, v, _pair_dots(qs, k)

    blocks = [load(j) for j in range(nblk)]
    for j, (rows, v, scores) in enumerate(blocks):
        mask = _band_mask(SWA_MAX_DIST, chunk_start + j * BAND, n_pairs)
        stats = _pair_softmax(scores, mask)
        acc = _pair_pv(stats[0][1], stats[1][1], v)
        scales = []
        for g, (m, _, l) in enumerate(stats):
            sink = jnp.concatenate(
                [jnp.full((BAND, 1), sink_ref[SWA_GROUP * g + t], F32) for t in range(n_pairs)],
                axis=0)
            m_all = jnp.maximum(m, sink)
            keep = jnp.exp(m - m_all)
            den = l * keep + jnp.exp(sink - m_all)
            scales.append(keep / den)
        out = acc * _by_head(scales[0], scales[1], acc.shape)
        for t in range(n_pairs):
            o_ref[rows, t * LANES:(t + 1) * LANES] = out[t * BAND:(t + 1) * BAND, :].astype(o_ref.dtype)


def _swa(sinks, qa, ka, va, batch, seq, tq):
    n = qa.shape[0]
    chunks = seq // tq
    blk = tq // BAND
    cur = lambda b, c: (b * chunks + c, 0)
    prev = lambda b, c: (jnp.maximum((b * chunks + c) * blk - 1, 0), 0)
    return pl.pallas_call(
        _swa_kernel,
        grid=(batch, chunks),
        in_specs=[
            pl.BlockSpec(memory_space=pltpu.SMEM),
            pl.BlockSpec((tq, WIDTH_A), cur),
            pl.BlockSpec((tq, KV_A), cur),
            pl.BlockSpec((BAND, KV_A), prev),
            pl.BlockSpec((tq, KV_A), cur),
            pl.BlockSpec((BAND, KV_A), prev),
        ],
        out_specs=pl.BlockSpec((tq, WIDTH_A), cur),
        out_shape=jax.ShapeDtypeStruct((n, WIDTH_A), BF16),
        compiler_params=_params("parallel", "parallel"),
        name="swa",
    )(sinks, qa, ka, ka, va, va)


MOBA_UNROLL = 4


def _moba_kernel(q_ref, k_ref, v_ref, o_ref, kaug_ref, vt_ref, kmean_ref, acc_ref, m_ref, l_ref):
    i = pl.program_id(2)
    nb = k_ref.shape[0] // MOBA_BLOCK
    blk_shape = (MOBA_BLOCK, LANES)
    lane = _lane(blk_shape)
    lo = lane < HEAD_DIM

    @pl.when(i == 0)
    def _prepare():
        kmean_ref[...] = jnp.zeros_like(kmean_ref)

        def body(n, carry):
            rows = pl.ds(pl.multiple_of(n * MOBA_BLOCK, MOBA_BLOCK), MOBA_BLOCK)
            kb = k_ref[rows, :].astype(F32)
            kmean_ref[pl.ds(n, 1), :] = jnp.sum(kb, axis=0, keepdims=True) * (1.0 / MOBA_BLOCK)
            block_hot = jnp.where((lane == n) | (lane == HEAD_DIM + n), 1.0, 0.0)
            kaug_ref[0, rows, :] = jnp.where(lo, kb, block_hot).astype(BF16)
            kaug_ref[1, rows, :] = jnp.where(lo, block_hot, kb).astype(BF16)
            vt_ref[n] = v_ref[rows, :].astype(F32).T.astype(BF16)
            return carry

        lax.fori_loop(0, nb, body, 0)

    q = q_ref[...]
    kmean = kmean_ref[...]
    lo_m = _lane(kmean.shape) < HEAD_DIM
    sel_shape = (HEAD_DIM, MOBA_BLOCK)
    blk = lax.broadcasted_iota(jnp.int32, sel_shape, 0)
    blk_f = blk.astype(F32)
    pens = []
    for km in (jnp.where(lo_m, kmean, 0.0), jnp.where(lo_m, 0.0, kmean)):
        gate = _dot_t(km.astype(BF16), q)
        gate = jnp.where(blk < i, gate, -jnp.inf)
        sel = blk == i
        for _ in range(MOBA_TOPK):
            mx = jnp.max(gate, axis=0, keepdims=True)
            first = jnp.min(jnp.where(gate == mx, blk_f, float(HEAD_DIM)), axis=0, keepdims=True)
            pick = (blk_f == first) & (mx > -jnp.inf)
            sel = sel | pick
            gate = jnp.where(pick, -jnp.inf, gate)
        pens.append(jnp.where(sel, 0.0, NEG_INF))
    zeros = jnp.zeros(sel_shape, F32)
    q32 = q.astype(F32)
    qa0 = jnp.where(lo, q32, jnp.concatenate([zeros, pens[0]], axis=0).T).astype(BF16)
    qa1 = jnp.where(lo, jnp.concatenate([pens[1], zeros], axis=0).T, q32).astype(BF16)

    acc_ref[...] = jnp.zeros_like(acc_ref)
    m_ref[...] = jnp.full_like(m_ref, NEG_INF)
    l_ref[...] = jnp.zeros_like(l_ref)

    def attend(js, causal):
        rows = [pl.ds(pl.multiple_of(j * MOBA_BLOCK, MOBA_BLOCK), MOBA_BLOCK) for j in js]
        scores = [[_dot_t(kaug_ref[hd, r, :], qa) for hd, qa in enumerate((qa0, qa1))]
                  for r in rows]
        for j, blk_scores in zip(js, scores):
            vt = vt_ref[j]
            ps, alphas = [], []
            for hd, s in enumerate(blk_scores):
                if causal:
                    kk = lax.broadcasted_iota(jnp.int32, s.shape, 0)
                    qq = lax.broadcasted_iota(jnp.int32, s.shape, 1)
                    s = jnp.where(kk <= qq, s, NEG_INF)
                m_old = m_ref[hd]
                m_new = jnp.maximum(m_old, jnp.max(s, axis=0, keepdims=True))
                alpha = jnp.exp(m_old - m_new)
                p = jnp.exp(s - m_new)
                l_ref[hd] = alpha * l_ref[hd] + jnp.sum(p, axis=0, keepdims=True)
                m_ref[hd] = m_new
                ps.append(p.astype(BF16))
                alphas.append(alpha)
            for hd in range(2):
                pv = _dot(vt[hd * HEAD_DIM:(hd + 1) * HEAD_DIM, :], ps[hd])
                acc_ref[hd] = acc_ref[hd] * alphas[hd] + pv

    def body(t, carry):
        attend([t * MOBA_UNROLL + u for u in range(MOBA_UNROLL)], False)
        return carry

    lax.fori_loop(0, i // MOBA_UNROLL, body, 0)
    for u in range(MOBA_UNROLL - 1, 0, -1):
        @pl.when(i % MOBA_UNROLL >= u)
        def _(u=u):
            attend([i - u], False)
    attend([i], True)
    out_t = jnp.concatenate([acc_ref[hd] * (1.0 / l_ref[hd]) for hd in range(2)], axis=0)
    o_ref[...] = out_t.T.astype(o_ref.dtype)


def _moba(qb, kb, vb, batch, seq):
    n = qb.shape[0]
    nb = seq // MOBA_BLOCK
    pairs = WIDTH_B // LANES
    assert nb <= HEAD_DIM, "block one-hot must fit in the other head's lanes"
    qmap = lambda b, p, i: (b * nb + i, p)
    kvmap = lambda b, p, i: (b, p)
    return pl.pallas_call(
        _moba_kernel,
        grid=(batch, pairs, nb),
        in_specs=[
            pl.BlockSpec((MOBA_BLOCK, LANES), qmap),
            pl.BlockSpec((seq, LANES), kvmap),
            pl.BlockSpec((seq, LANES), kvmap),
        ],
        out_specs=pl.BlockSpec((MOBA_BLOCK, LANES), qmap),
        out_shape=jax.ShapeDtypeStruct((n, WIDTH_B), BF16),
        scratch_shapes=[
            pltpu.VMEM((2, seq, LANES), BF16),
            pltpu.VMEM((nb, LANES, MOBA_BLOCK), BF16),
            pltpu.VMEM((HEAD_DIM, LANES), F32),
            pltpu.VMEM((2, HEAD_DIM, MOBA_BLOCK), F32),
            pltpu.VMEM((2, 1, MOBA_BLOCK), F32),
            pltpu.VMEM((2, 1, MOBA_BLOCK), F32),
        ],
        compiler_params=_params("parallel", "parallel", "arbitrary"),
        name="moba",
    )(qb, kb, vb)


DIL_CHUNK = DIL_PATTERNS[-1][1] * BAND
DIL_UNROLL = 4


def _dil_kernel(q_ref, kp_ref, k_ref, vp_ref, v_ref, o_ref, kk_ref, vv_ref, acc_ref, m_ref, l_ref):
    c = pl.program_id(2)
    kk_ref[0:DIL_CHUNK, :] = kp_ref[...]
    kk_ref[DIL_CHUNK:, :] = k_ref[...]
    vv_ref[0:DIL_CHUNK, :] = vp_ref[...]
    vv_ref[DIL_CHUNK:, :] = v_ref[...]
    blocks = DIL_CHUNK // BAND
    shape = (BAND, LANES)

    for pi, (_, d) in enumerate(DIL_PATTERNS):
        per_class = blocks // d

        def load(t, d=d, per_class=per_class):
            r = t // per_class
            j = t % per_class
            q0 = r + BAND * d * j
            k0 = DIL_CHUNK + q0 - BAND * d
            if d == 1:
                qrows = pl.ds(pl.multiple_of(q0, BAND), BAND)
                krows = pl.ds(pl.multiple_of(k0, BAND), 2 * BAND)
            else:
                qrows = pl.ds(q0, BAND, stride=d)
                krows = pl.ds(k0, 2 * BAND, stride=d)
            scores = _pair_dots(q_ref[qrows, :].astype(BF16), kk_ref[krows, :])
            return qrows, krows, (c * per_class + j) * BAND, scores

        def body(t, carry, pi=pi):
            group = [load(t * DIL_UNROLL + u) for u in range(DIL_UNROLL)]
            for qrows, krows, block_start, scores in group:
                mask = _band_mask(DIL_MAX_DIST, block_start)
                (m0, p0, l0), (m1, p1, l1) = _pair_softmax(scores, mask)
                acc_ref[pi, qrows, :] = _pair_pv(p0, p1, vv_ref[krows, :])
                m_ref[pi, qrows, :] = _by_head(m0, m1, shape)
                l_ref[pi, qrows, :] = _by_head(l0, l1, shape)
            return carry

        lax.fori_loop(0, blocks // DIL_UNROLL, body, 0)

    def combine(t, carry):
        rows = pl.ds(pl.multiple_of(t * BAND, BAND), BAND)
        ms = [m_ref[pi, rows, :] for pi in range(len(DIL_PATTERNS))]
        m_max = functools.reduce(jnp.maximum, ms)
        num = jnp.zeros(shape, F32)
        den = jnp.zeros(shape, F32)
        for pi, m in enumerate(ms):
            e = jnp.exp(m - m_max)
            num = num + acc_ref[pi, rows, :] * e
            den = den + l_ref[pi, rows, :] * e
        o_ref[rows, :] = (num / den).astype(o_ref.dtype)
        return carry

    lax.fori_loop(0, blocks, combine, 0)


def _dilated(qc, kc, vc, batch, seq):
    n = qc.shape[0]
    assert seq % DIL_CHUNK == 0
    chunks = seq // DIL_CHUNK
    pairs = WIDTH_C // LANES
    cur = lambda b, p, c: (b * chunks + c, p)
    prev = lambda b, p, c: (b * chunks + jnp.maximum(c - 1, 0), p)
    blk = (DIL_CHUNK, LANES)
    n_pat = len(DIL_PATTERNS)
    return pl.pallas_call(
        _dil_kernel,
        grid=(batch, pairs, chunks),
        in_specs=[pl.BlockSpec(blk, cur), pl.BlockSpec(blk, prev), pl.BlockSpec(blk, cur),
                  pl.BlockSpec(blk, prev), pl.BlockSpec(blk, cur)],
        out_specs=pl.BlockSpec(blk, cur),
        out_shape=jax.ShapeDtypeStruct((n, WIDTH_C), BF16),
        scratch_shapes=[
            pltpu.VMEM((2 * DIL_CHUNK, LANES), F32),
            pltpu.VMEM((2 * DIL_CHUNK, LANES), F32),
            pltpu.VMEM((n_pat, DIL_CHUNK, LANES), F32),
            pltpu.VMEM((n_pat, DIL_CHUNK, LANES), F32),
            pltpu.VMEM((n_pat, DIL_CHUNK, LANES), F32),
        ],
        compiler_params=_params("parallel", "parallel", "parallel"),
        name="dilated",
    )(qc, kc, kc, vc, vc)


def _merge_kernel(x_ref, g_ref, oa_ref, ob_ref, oc_ref, wg_ref, wa_ref, wb_ref, wc_ref, wo_ref, y_ref):
    x = x_ref[...]
    d = x.shape[1]
    h = _rms(x, g_ref[...]).astype(BF16)
    merged = None
    for br, (o_ref, w_ref) in enumerate(((oa_ref, wa_ref), (ob_ref, wb_ref), (oc_ref, wc_ref))):
        gate = _sigmoid(_dot(h, wg_ref[:, br * d:(br + 1) * d]))
        y = gate * _dot(o_ref[...], w_ref[...])
        merged = y if merged is None else merged + y
    y_ref[...] = x + _dot(merged.astype(BF16), wo_ref[...])


def _merge(x2, gain, oa, ob, oc, w_gates, w_a, w_b, w_c, w_o, tm):
    n, d = x2.shape
    row = lambda i: (i, 0)
    full = lambda i: (0, 0)
    return pl.pallas_call(
        _merge_kernel,
        grid=(n // tm,),
        in_specs=[
            pl.BlockSpec((tm, d), row),
            pl.BlockSpec((1, d), full),
            pl.BlockSpec((tm, WIDTH_A), row),
            pl.BlockSpec((tm, WIDTH_B), row),
            pl.BlockSpec((tm, WIDTH_C), row),
            pl.BlockSpec(w_gates.shape, full),
            pl.BlockSpec(w_a.shape, full),
            pl.BlockSpec(w_b.shape, full),
            pl.BlockSpec(w_c.shape, full),
            pl.BlockSpec(w_o.shape, full),
        ],
        out_specs=pl.BlockSpec((tm, d), row),
        out_shape=jax.ShapeDtypeStruct((n, d), F32),
        compiler_params=_params("parallel"),
        name="merge",
    )(x2, gain, oa, ob, oc, w_gates, w_a, w_b, w_c, w_o)


def _swiglu(h, wg, wu, wd):
    a = _dot(h, wg)
    u = _dot(h, wu)
    return _dot((a * _sigmoid(a) * u).astype(BF16), wd)


def _ffn_kernel(x_ref, g_ref, wg_ref, wu_ref, wd_ref, y_ref):
    x = x_ref[...]
    h = _rms(x, g_ref[...]).astype(BF16)
    y_ref[...] = x + _swiglu(h, wg_ref[...], wu_ref[...], wd_ref[...])


def _ffn(x2, gain, wg, wu, wd, tm):
    n, d = x2.shape
    row = lambda i: (i, 0)
    full = lambda i: (0, 0)
    return pl.pallas_call(
        _ffn_kernel,
        grid=(n // tm,),
        in_specs=[
            pl.BlockSpec((tm, d), row),
            pl.BlockSpec((1, d), full),
            pl.BlockSpec(wg.shape, full),
            pl.BlockSpec(wu.shape, full),
            pl.BlockSpec(wd.shape, full),
        ],
        out_specs=pl.BlockSpec((tm, d), row),
        out_shape=jax.ShapeDtypeStruct((n, d), F32),
        compiler_params=_params("parallel"),
        name="ffn",
    )(x2, gain, wg, wu, wd)


MOE_GROUP = 2048
MOE_TOKEN_BLOCK = 512
MOE_CHUNK = 256

def _split_bf16(a):
    hi = a.astype(BF16)
    return hi, (a - hi.astype(F32)).astype(BF16)


def _router_kernel(x_ref, g_ref, wr_ref, h_ref, comb_ref, rank_ref, rankt_ref, count_ref):
    tm = x_ref.shape[0]
    tile_in_group = pl.program_id(0) % (MOE_GROUP // tm)
    h = _rms(x_ref[...], g_ref[...])
    h_ref[...] = h.astype(BF16)
    h_hi, h_lo = _split_bf16(h)
    w_hi, w_lo = _split_bf16(wr_ref[...])
    logits = _dot(h_hi, w_hi) + (_dot(h_hi, w_lo) + _dot(h_lo, w_hi))
    lane = _lane(logits.shape)
    logits = jnp.where(lane < N_EXPERTS, logits, -jnp.inf)
    tops = []
    for _ in range(TOP_K):
        mx = jnp.max(logits, axis=-1, keepdims=True)
        first = jnp.min(jnp.where(logits == mx, lane, LANES), axis=-1, keepdims=True)
        tops.append((mx, first))
        logits = jnp.where(lane == first, -jnp.inf, logits)
    (v1, i1), (v2, i2) = tops
    e2 = jnp.exp(v2 - v1)
    w1 = 1.0 / (1.0 + e2)
    w2 = e2 / (1.0 + e2)
    sel = jnp.where(lane == i1, 1.0, 0.0) + jnp.where(lane == i2, 1.0, 0.0)
    comb_ref[...] = jnp.where(lane == i1, w1, 0.0) + jnp.where(lane == i2, w2, 0.0)

    @pl.when(tile_in_group == 0)
    def _():
        count_ref[...] = jnp.zeros_like(count_ref)

    r = lax.broadcasted_iota(jnp.int32, (tm, tm), 0)
    c = lax.broadcasted_iota(jnp.int32, (tm, tm), 1)
    before = _dot(jnp.where(c < r, 1.0, 0.0).astype(BF16), sel.astype(BF16)) + count_ref[...]
    rank = jnp.where(sel > 0.0, before, -1.0)
    count_ref[...] = count_ref[...] + jnp.sum(sel, axis=0, keepdims=True)
    rank_ref[...] = rank
    rankt_ref[0] = rank.T[:N_EXPERTS, :]


def _router(x2, gain, w_router_pad, tm):
    n, d = x2.shape
    assert n % MOE_GROUP == 0 and MOE_GROUP % tm == 0
    per_group = MOE_GROUP // tm
    row = lambda i: (i, 0)
    return pl.pallas_call(
        _router_kernel,
        grid=(n // tm,),
        in_specs=[
            pl.BlockSpec((tm, d), row),
            pl.BlockSpec((1, d), lambda i: (0, 0)),
            pl.BlockSpec((d, LANES), lambda i: (0, 0)),
        ],
        out_specs=[pl.BlockSpec((tm, d), row),
                   pl.BlockSpec((tm, LANES), row),
                   pl.BlockSpec((tm, LANES), row),
                   pl.BlockSpec((1, N_EXPERTS, tm), lambda i: (i // per_group, 0, i % per_group))],
        out_shape=[jax.ShapeDtypeStruct((n, d), BF16),
                   jax.ShapeDtypeStruct((n, LANES), F32),
                   jax.ShapeDtypeStruct((n, LANES), F32),
                   jax.ShapeDtypeStruct((n // MOE_GROUP, N_EXPERTS, MOE_GROUP), F32)],
        scratch_shapes=[pltpu.VMEM((1, LANES), F32)],
        compiler_params=_params("arbitrary"),
        name="router",
    )(x2, gain, w_router_pad)


def _experts_kernel(h_ref, comb_ref, rank_ref, rankt_ref, wg_ref, wu_ref, wd_ref, y_ref,
                    xs_ref, acc_ref, col_ref, start_ref):
    e = pl.program_id(1)
    f = pl.program_id(2)
    group, d = h_ref.shape
    n_blocks = group // MOE_TOKEN_BLOCK
    blocks = [slice(b * MOE_TOKEN_BLOCK, (b + 1) * MOE_TOKEN_BLOCK) for b in range(n_blocks)]

    def chunk_rows(m):
        return pl.ds(pl.multiple_of(m * MOE_CHUNK, MOE_CHUNK), MOE_CHUNK)

    def n_chunks():
        return (start_ref[n_blocks] + MOE_CHUNK - 1) // MOE_CHUNK

    def overlaps(m, b):
        return (start_ref[b] < (m + 1) * MOE_CHUNK) & (start_ref[b + 1] > m * MOE_CHUNK)

    @pl.when((e == 0) & (f == 0))
    def _():
        y_ref[...] = jnp.zeros_like(y_ref)

    @pl.when(f == 0)
    def _dispatch():
        pick = _lane((group, LANES)) == e
        rank_col = jnp.sum(jnp.where(pick, rank_ref[...], 0.0), axis=1, keepdims=True)
        col_ref[0] = rank_col
        col_ref[1] = jnp.sum(jnp.where(pick, comb_ref[...], 0.0), axis=1, keepdims=True)
        start_ref[0] = 0
        for b in range(n_blocks):
            routed = jnp.sum(jnp.where(rank_col[blocks[b]] >= 0.0, 1.0, 0.0))
            start_ref[b + 1] = start_ref[b] + routed.astype(jnp.int32)
        rank_row = rankt_ref[0, pl.ds(e, 1), :]

        def pack(m, carry):
            dest = (lax.broadcasted_iota(jnp.int32, (MOE_CHUNK, MOE_TOKEN_BLOCK), 0)
                    + m * MOE_CHUNK).astype(F32)
            xs_ref[chunk_rows(m), :] = jnp.zeros((MOE_CHUNK, d), BF16)
            acc_ref[chunk_rows(m), :] = jnp.zeros((MOE_CHUNK, d), F32)
            for b in range(n_blocks):
                @pl.when(overlaps(m, b))
                def _():
                    onehot = jnp.where(rank_row[:, blocks[b]] == dest, 1.0, 0.0).astype(BF16)
                    rows = _dot(onehot, h_ref[blocks[b], :])
                    xs_ref[chunk_rows(m), :] = (xs_ref[chunk_rows(m), :].astype(F32) + rows).astype(BF16)
            return carry

        lax.fori_loop(0, n_chunks(), pack, 0)

    def expert(m, carry):
        xm = xs_ref[chunk_rows(m), :]
        acc_ref[chunk_rows(m), :] += _swiglu(xm, wg_ref[0], wu_ref[0], wd_ref[0])
        return carry

    lax.fori_loop(0, n_chunks(), expert, 0)

    @pl.when(f == pl.num_programs(2) - 1)
    def _combine():
        def unpack(m, carry):
            y_hi, y_lo = _split_bf16(acc_ref[chunk_rows(m), :])
            dest = (_lane((MOE_TOKEN_BLOCK, MOE_CHUNK)) + m * MOE_CHUNK).astype(F32)
            for b in range(n_blocks):
                @pl.when(overlaps(m, b))
                def _():
                    onehot = jnp.where(col_ref[0, blocks[b], :] == dest, 1.0, 0.0).astype(BF16)
                    y_ref[blocks[b], :] += col_ref[1, blocks[b], :] * (_dot(onehot, y_hi) + _dot(onehot, y_lo))
            return carry

        lax.fori_loop(0, n_chunks(), unpack, 0)


def _experts(hb, comb, rank, rank_t, wg, wu, wd, ff_split):
    n, d = hb.shape
    n_e, _, d_ff = wg.shape
    ff = d_ff // ff_split
    group = lambda s, e, f: (s, 0)
    once = pl.Buffered(1)
    return pl.pallas_call(
        _experts_kernel,
        grid=(n // MOE_GROUP, n_e, ff_split),
        in_specs=[
            pl.BlockSpec((MOE_GROUP, d), group, pipeline_mode=once),
            pl.BlockSpec((MOE_GROUP, LANES), group, pipeline_mode=once),
            pl.BlockSpec((MOE_GROUP, LANES), group, pipeline_mode=once),
            pl.BlockSpec((1, N_EXPERTS, MOE_GROUP), lambda s, e, f: (s, 0, 0)),
            pl.BlockSpec((1, d, ff), lambda s, e, f: (e, 0, f)),
            pl.BlockSpec((1, d, ff), lambda s, e, f: (e, 0, f)),
            pl.BlockSpec((1, ff, d), lambda s, e, f: (e, f, 0)),
        ],
        out_specs=pl.BlockSpec((MOE_GROUP, d), group, pipeline_mode=once),
        out_shape=jax.ShapeDtypeStruct((n, d), F32),
        scratch_shapes=[
            pltpu.VMEM((MOE_GROUP, d), BF16),
            pltpu.VMEM((MOE_GROUP, d), F32),
            pltpu.VMEM((2, MOE_GROUP, 1), F32),
            pltpu.SMEM((MOE_GROUP // MOE_TOKEN_BLOCK + 1,), jnp.int32),
        ],
        compiler_params=_params("parallel", "arbitrary", "arbitrary"),
        name="experts",
    )(hb, comb, rank, rank_t, wg, wu, wd)


def _add_kernel(x_ref, y_ref, o_ref):
    o_ref[...] = x_ref[...] + y_ref[...]


def _add_norm_kernel(x_ref, y_ref, g_ref, o_ref):
    o_ref[...] = _rms(x_ref[...] + y_ref[...], g_ref[...])


def _norm_kernel(x_ref, g_ref, o_ref):
    o_ref[...] = _rms(x_ref[...], g_ref[...])


def _rowwise(body, name, tm, x2, *rest):
    n, d = x2.shape
    row = pl.BlockSpec((tm, d), lambda i: (i, 0))
    specs = [row if a.shape == (n, d) else pl.BlockSpec((1, d), lambda i: (0, 0)) for a in (x2,) + rest]
    return pl.pallas_call(
        body,
        grid=(n // tm,),
        in_specs=specs,
        out_specs=row,
        out_shape=jax.ShapeDtypeStruct((n, d), F32),
        compiler_params=_params("parallel"),
        name=name,
    )(x2, *rest)


def _rope_tables(seq):
    pos = jnp.arange(seq, dtype=F32)
    inv_freq = ROPE_THETA ** (-jnp.arange(0, HEAD_DIM, 2, dtype=F32) / HEAD_DIM)
    ang = pos[:, None] * inv_freq[None, :]
    cos, sin = jnp.cos(ang), jnp.sin(ang)
    reps = LANES // HEAD_DIM
    cos_t = jnp.tile(jnp.concatenate([cos, cos], axis=-1), (1, reps))
    sin_t = jnp.tile(jnp.concatenate([-sin, sin], axis=-1), (1, reps))
    return cos_t, sin_t


def _pair_heads_by_group(w, axis):
    shape = w.shape
    split = shape[:axis] + (SWA_KV_HEADS, SWA_GROUP, HEAD_DIM) + shape[axis + 1:]
    return jnp.swapaxes(w.reshape(split), axis, axis + 1).reshape(shape)


def _row_tile(n, want):
    t = min(n, want)
    assert n % t == 0
    return t


def kernel(x, norm_mix, w_in, attn_sinks, w_br_a, w_br_b, w_br_c, w_out, norm_ffn, w_ff_gate, w_ff_up, w_ff_down, w_router, w_moe_gate, w_moe_up, w_moe_down, norm_final):
    batch, seq, d = x.shape
    n = batch * seq
    depth = norm_mix.shape[0]
    tm = _row_tile(seq, 512)
    cos_t, sin_t = _rope_tables(seq)
    x2 = x.reshape(n, d).astype(F32)

    for layer in range(depth):
        w_l = w_in[layer]
        w_qa = _pair_heads_by_group(w_l[:, :WIDTH_A], 1)
        w_qkv = jnp.concatenate([w_qa, w_l[:, WIDTH_A:QKV_COLS]], axis=1).astype(BF16)
        w_gates = w_l[:, QKV_COLS:].astype(BF16)
        gain = norm_mix[layer].reshape(1, d)

        qa, ka, va, qb, kb, vb, qc, kc, vc = _inproj(x2, gain, w_qkv, cos_t, sin_t, seq, tm)
        oa = _swa(attn_sinks[layer], qa, ka, va, batch, seq, tm)
        ob = _moba(qb, kb, vb, batch, seq)
        oc = _dilated(qc, kc, vc, batch, seq)
        x2 = _merge(x2, gain, oa, ob, oc, w_gates,
                    _pair_heads_by_group(w_br_a[layer], 0).astype(BF16),
                    w_br_b[layer].astype(BF16), w_br_c[layer].astype(BF16),
                    w_out[layer].astype(BF16), tm)

        gain_f = norm_ffn[layer].reshape(1, d)
        idx = layer // 2
        if layer % 2 == 0:
            x2 = _ffn(x2, gain_f, w_ff_gate[idx].astype(BF16), w_ff_up[idx].astype(BF16),
                      w_ff_down[idx].astype(BF16), tm)
        else:
            w_r = jnp.pad(w_router[idx], ((0, 0), (0, LANES - N_EXPERTS)))
            hb, comb, rank, rank_t = _router(x2, gain_f, w_r, tm)
            y = _experts(hb, comb, rank, rank_t, w_moe_gate[idx].astype(BF16),
                         w_moe_up[idx].astype(BF16), w_moe_down[idx].astype(BF16), 2)
            if layer == depth - 1:
                out = _rowwise(_add_norm_kernel, "add_norm", tm, x2, y, norm_final.reshape(1, d))
                return out.reshape(batch, seq, d)
            x2 = _rowwise(_add_kernel, "add", tm, x2, y)

    return _rowwise(_norm_kernel, "final_norm", tm, x2, norm_final.reshape(1, d)).reshape(batch, seq, d)
```

```python
import functools

import jax
import jax.numpy as jnp
import numpy as np
from jax import lax
from jax.experimental import pallas as pl
from jax.experimental.pallas import tpu as pltpu

F32 = jnp.float32
BF16 = jnp.bfloat16

HEAD_DIM = 64
ATTN_SCALE = HEAD_DIM ** -0.5
ROPE_THETA = 10000.0
RMS_EPS = 1e-5
NEG_INF = -1e30
BAND = 128
SWA_Q_HEADS, SWA_KV_HEADS = 6, 2
SWA_GROUP = SWA_Q_HEADS // SWA_KV_HEADS
SWA_MAX_DIST = 127
MOBA_HEADS, MOBA_BLOCK, MOBA_TOPK = 4, 256, 3
DIL_HEADS = 6
DIL_PATTERNS = ((128, 1), (512, 4), (2048, 16))
DIL_MAX_DIST = 128
N_EXPERTS, TOP_K = 8, 2

LANES = 128
VMEM_LIMIT = 56 * 1024 * 1024

WIDTH_A = SWA_Q_HEADS * HEAD_DIM
KV_A = SWA_KV_HEADS * HEAD_DIM
WIDTH_B = MOBA_HEADS * HEAD_DIM
WIDTH_C = DIL_HEADS * HEAD_DIM
QKV_COLS = WIDTH_A + 2 * KV_A + 3 * WIDTH_B + 3 * WIDTH_C


def _params(*sem):
    return pltpu.CompilerParams(dimension_semantics=sem, vmem_limit_bytes=VMEM_LIMIT)


def _rms(x, gain):
    ms = jnp.mean(x * x, axis=-1, keepdims=True)
    return x * lax.rsqrt(ms + RMS_EPS) * gain


def _dot(a, b):
    return jnp.dot(a, b, preferred_element_type=F32)


def _dot_t(a, b):
    return lax.dot_general(a, b, (((1,), (1,)), ((), ())), preferred_element_type=F32)


def _sigmoid(z):
    return 1.0 / (1.0 + jnp.exp(-z))


def _lane(shape):
    return lax.broadcasted_iota(jnp.int32, shape, len(shape) - 1)


def _inproj_kernel(x_ref, g_ref, w_ref, cos_ref, sin_ref,
                   qa_ref, ka_ref, va_ref, qb_ref, kb_ref, vb_ref, qc_ref, kc_ref, vc_ref):
    tm = x_ref.shape[0]
    h = _rms(x_ref[...], g_ref[...]).astype(BF16)
    cos = cos_ref[...]
    sin = sin_ref[...]
    first_half = (_lane((tm, LANES)) & (HEAD_DIM // 2)) == 0

    def rope(z):
        rot = jnp.where(first_half,
                        pltpu.roll(z, LANES - HEAD_DIM // 2, 1),
                        pltpu.roll(z, HEAD_DIM // 2, 1))
        return z * cos + rot * sin

    def proj(c0, width):
        return _dot(h, w_ref[:, c0:c0 + width])

    def store(ref, z, roped, scale, dtype):
        for t in range(z.shape[1] // LANES):
            zt = z[:, t * LANES:(t + 1) * LANES]
            if roped:
                zt = rope(zt)
            if scale:
                zt = zt * ATTN_SCALE
            ref[:, t * LANES:(t + 1) * LANES] = zt.astype(dtype)

    c = 0
    store(qa_ref, proj(c, WIDTH_A), True, True, BF16); c += WIDTH_A
    kv = proj(c, 2 * KV_A); c += 2 * KV_A
    store(ka_ref, kv[:, :KV_A], True, False, BF16)
    store(va_ref, kv[:, KV_A:], False, False, BF16)
    store(qb_ref, proj(c, WIDTH_B), True, True, BF16); c += WIDTH_B
    store(kb_ref, proj(c, WIDTH_B), True, False, BF16); c += WIDTH_B
    store(vb_ref, proj(c, WIDTH_B), False, False, BF16); c += WIDTH_B
    store(qc_ref, proj(c, WIDTH_C), True, True, F32); c += WIDTH_C
    store(kc_ref, proj(c, WIDTH_C), True, False, F32); c += WIDTH_C
    store(vc_ref, proj(c, WIDTH_C), False, False, F32)


def _inproj(x2, gain, w_qkv, cos_t, sin_t, seq, tm):
    n, d = x2.shape
    tiles_per_seq = seq // tm
    row = lambda i: (i, 0)
    widths = (WIDTH_A, KV_A, KV_A, WIDTH_B, WIDTH_B, WIDTH_B, WIDTH_C, WIDTH_C, WIDTH_C)
    dtypes = (BF16,) * 6 + (F32,) * 3
    return pl.pallas_call(
        _inproj_kernel,
        grid=(n // tm,),
        in_specs=[
            pl.BlockSpec((tm, d), row),
            pl.BlockSpec((1, d), lambda i: (0, 0)),
            pl.BlockSpec((d, QKV_COLS), lambda i: (0, 0)),
            pl.BlockSpec((tm, LANES), lambda i: (i % tiles_per_seq, 0)),
            pl.BlockSpec((tm, LANES), lambda i: (i % tiles_per_seq, 0)),
        ],
        out_specs=[pl.BlockSpec((tm, w), row) for w in widths],
        out_shape=[jax.ShapeDtypeStruct((n, w), dt) for w, dt in zip(widths, dtypes)],
        compiler_params=_params("parallel"),
        name="inproj",
    )(x2, gain, w_qkv, cos_t, sin_t)


def _band_mask(max_dist, block_start, stack=1):
    qi = lax.broadcasted_iota(jnp.int32, (stack * BAND, 2 * BAND), 0) % BAND
    kj = lax.broadcasted_iota(jnp.int32, (stack * BAND, 2 * BAND), 1) - BAND
    diff = qi - kj
    return (diff >= 0) & (diff <= max_dist) & (kj + block_start >= 0)


def _split_heads(a):
    a32 = a.astype(F32)
    lo = _lane(a.shape) < HEAD_DIM
    return jnp.where(lo, a32, 0.0).astype(BF16), jnp.where(lo, 0.0, a32).astype(BF16)


def _pair_dots(q, k):
    return [_dot_t(q, kh) for kh in _split_heads(k)]


def _pair_softmax(scores, mask):
    out = []
    for s in scores:
        s = jnp.where(mask, s, NEG_INF)
        m = jnp.max(s, axis=-1, keepdims=True)
        p = jnp.exp(s - m)
        out.append((m, p, jnp.sum(p, axis=-1, keepdims=True)))
    return out


def _pair_pv(p0, p1, v):
    pc = jnp.concatenate([p0.astype(BF16), p1.astype(BF16)], axis=1)
    return _dot(pc, jnp.concatenate(_split_heads(v), axis=0))


def _by_head(lo_val, hi_val, shape):
    return jnp.where(_lane(shape) < HEAD_DIM, lo_val, hi_val)


def _swa_kernel(sink_ref, q_ref, k_ref, kp_ref, v_ref, vp_ref, o_ref):
    nblk = q_ref.shape[0] // BAND
    chunk_start = pl.program_id(1) * q_ref.shape[0]
    n_pairs = WIDTH_A // LANES
    def load(j):
        rows = slice(j * BAND, (j + 1) * BAND)
        prev = slice((j - 1) * BAND, j * BAND)
        q = q_ref[rows, :]
        qs = jnp.concatenate([q[:, t * LANES:(t + 1) * LANES] for t in range(n_pairs)], axis=0)
        if j == 0:
            k = jnp.concatenate([kp_ref[...], k_ref[rows, :]], axis=0)
            v = jnp.concatenate([vp_ref[...], v_ref[rows, :]], axis=0)
        else:
            k = jnp.concatenate([k_ref[prev, :], k_ref[rows, :]], axis=0)
            v = jnp.concatenate([v_ref[prev, :], v_ref[rows, :]], axis=0)
        return rows, v, _pair_dots(qs, k)

    blocks = [load(j) for j in range(nblk)]
    for j, (rows, v, scores) in enumerate(blocks):
        mask = _band_mask(SWA_MAX_DIST, chunk_start + j * BAND, n_pairs)
        stats = _pair_softmax(scores, mask)
        acc = _pair_pv(stats[0][1], stats[1][1], v)
        scales = []
        for g, (m, _, l) in enumerate(stats):
            sink = jnp.concatenate(
                [jnp.full((BAND, 1), sink_ref[SWA_GROUP * g + t], F32) for t in range(n_pairs)],
                axis=0)
            m_all = jnp.maximum(m, sink)
            keep = jnp.exp(m - m_all)
            den = l * keep + jnp.exp(sink - m_all)
            scales.append(keep / den)
        out = acc * _by_head(scales[0], scales[1], acc.shape)
        for t in range(n_pairs):
            o_ref[rows, t * LANES:(t + 1) * LANES] = out[t * BAND:(t + 1) * BAND, :].astype(o_ref.dtype)


def _swa(sinks, qa, ka, va, batch, seq, tq):
    n = qa.shape[0]
    chunks = seq // tq
    blk = tq // BAND
    cur = lambda b, c: (b * chunks + c, 0)
    prev = lambda b, c: (jnp.maximum((b * chunks + c) * blk - 1, 0), 0)
    return pl.pallas_call(
        _swa_kernel,
        grid=(batch, chunks),
        in_specs=[
            pl.BlockSpec(memory_space=pltpu.SMEM),
            pl.BlockSpec((tq, WIDTH_A), cur),
            pl.BlockSpec((tq, KV_A), cur),
            pl.BlockSpec((BAND, KV_A), prev),
            pl.BlockSpec((tq, KV_A), cur),
            pl.BlockSpec((BAND, KV_A), prev),
        ],
        out_specs=pl.BlockSpec((tq, WIDTH_A), cur),
        out_shape=jax.ShapeDtypeStruct((n, WIDTH_A), BF16),
        compiler_params=_params("parallel", "parallel"),
        name="swa",
    )(sinks, qa, ka, ka, va, va)


MOBA_UNROLL = 4


def _moba_kernel(q_ref, k_ref, v_ref, o_ref, kaug_ref, vt_ref, kmean_ref, acc_ref, m_ref, l_ref):
    i = pl.program_id(2)
    nb = k_ref.shape[0] // MOBA_BLOCK
    blk_shape = (MOBA_BLOCK, LANES)
    lane = _lane(blk_shape)
    lo = lane < HEAD_DIM

    @pl.when(i == 0)
    def _prepare():
        kmean_ref[...] = jnp.zeros_like(kmean_ref)

        def body(n, carry):
            rows = pl.ds(pl.multiple_of(n * MOBA_BLOCK, MOBA_BLOCK), MOBA_BLOCK)
            kb = k_ref[rows, :].astype(F32)
            kmean_ref[pl.ds(n, 1), :] = jnp.sum(kb, axis=0, keepdims=True) * (1.0 / MOBA_BLOCK)
            block_hot = jnp.where((lane == n) | (lane == HEAD_DIM + n), 1.0, 0.0)
            kaug_ref[0, rows, :] = jnp.where(lo, kb, block_hot).astype(BF16)
            kaug_ref[1, rows, :] = jnp.where(lo, block_hot, kb).astype(BF16)
            vt_ref[n] = v_ref[rows, :].astype(F32).T.astype(BF16)
            return carry

        lax.fori_loop(0, nb, body, 0)

    q = q_ref[...]
    kmean = kmean_ref[...]
    lo_m = _lane(kmean.shape) < HEAD_DIM
    sel_shape = (HEAD_DIM, MOBA_BLOCK)
    blk = lax.broadcasted_iota(jnp.int32, sel_shape, 0)
    blk_f = blk.astype(F32)
    pens = []
    for km in (jnp.where(lo_m, kmean, 0.0), jnp.where(lo_m, 0.0, kmean)):
        gate = _dot_t(km.astype(BF16), q)
        gate = jnp.where(blk < i, gate, -jnp.inf)
        sel = blk == i
        for _ in range(MOBA_TOPK):
            mx = jnp.max(gate, axis=0, keepdims=True)
            first = jnp.min(jnp.where(gate == mx, blk_f, float(HEAD_DIM)), axis=0, keepdims=True)
            pick = (blk_f == first) & (mx > -jnp.inf)
            sel = sel | pick
            gate = jnp.where(pick, -jnp.inf, gate)
        pens.append(jnp.where(sel, 0.0, NEG_INF))
    zeros = jnp.zeros(sel_shape, F32)
    q32 = q.astype(F32)
    qa0 = jnp.where(lo, q32, jnp.concatenate([zeros, pens[0]], axis=0).T).astype(BF16)
    qa1 = jnp.where(lo, jnp.concatenate([pens[1], zeros], axis=0).T, q32).astype(BF16)

    acc_ref[...] = jnp.zeros_like(acc_ref)
    m_ref[...] = jnp.full_like(m_ref, NEG_INF)
    l_ref[...] = jnp.zeros_like(l_ref)

    def attend(js, own_last):
        rows = [pl.ds(pl.multiple_of(j * MOBA_BLOCK, MOBA_BLOCK), MOBA_BLOCK) for j in js]
        scores = [[_dot_t(kaug_ref[hd, r, :], qa) for hd, qa in enumerate((qa0, qa1))]
                  for r in rows]
        for n, (j, blk_scores) in enumerate(zip(js, scores)):
            vt = vt_ref[j]
            ps, alphas = [], []
            for hd, s in enumerate(blk_scores):
                if own_last and n == len(js) - 1:
                    kk = lax.broadcasted_iota(jnp.int32, s.shape, 0)
                    qq = lax.broadcasted_iota(jnp.int32, s.shape, 1)
                    s = jnp.where(kk <= qq, s, NEG_INF)
                m_old = m_ref[hd]
                m_new = jnp.maximum(m_old, jnp.max(s, axis=0, keepdims=True))
                alpha = jnp.exp(m_old - m_new)
                p = jnp.exp(s - m_new)
                l_ref[hd] = alpha * l_ref[hd] + jnp.sum(p, axis=0, keepdims=True)
                m_ref[hd] = m_new
                ps.append(p.astype(BF16))
                alphas.append(alpha)
            for hd in range(2):
                pv = _dot(vt[hd * HEAD_DIM:(hd + 1) * HEAD_DIM, :], ps[hd])
                acc_ref[hd] = acc_ref[hd] * alphas[hd] + pv

    def body(t, carry):
        attend([t * MOBA_UNROLL + u for u in range(MOBA_UNROLL)], False)
        return carry

    lax.fori_loop(0, i // MOBA_UNROLL, body, 0)
    for u in range(MOBA_UNROLL):
        @pl.when(i % MOBA_UNROLL == u)
        def _(u=u):
            attend([i - u + t for t in range(u + 1)], True)
    out_t = jnp.concatenate([acc_ref[hd] * (1.0 / l_ref[hd]) for hd in range(2)], axis=0)
    o_ref[...] = out_t.T.astype(o_ref.dtype)


def _moba(qb, kb, vb, batch, seq):
    n = qb.shape[0]
    nb = seq // MOBA_BLOCK
    pairs = WIDTH_B // LANES
    assert nb <= HEAD_DIM, "block one-hot must fit in the other head's lanes"
    qmap = lambda b, p, i: (b * nb + i, p)
    kvmap = lambda b, p, i: (b, p)
    return pl.pallas_call(
        _moba_kernel,
        grid=(batch, pairs, nb),
        in_specs=[
            pl.BlockSpec((MOBA_BLOCK, LANES), qmap),
            pl.BlockSpec((seq, LANES), kvmap),
            pl.BlockSpec((seq, LANES), kvmap),
        ],
        out_specs=pl.BlockSpec((MOBA_BLOCK, LANES), qmap),
        out_shape=jax.ShapeDtypeStruct((n, WIDTH_B), BF16),
        scratch_shapes=[
            pltpu.VMEM((2, seq, LANES), BF16),
            pltpu.VMEM((nb, LANES, MOBA_BLOCK), BF16),
            pltpu.VMEM((HEAD_DIM, LANES), F32),
            pltpu.VMEM((2, HEAD_DIM, MOBA_BLOCK), F32),
            pltpu.VMEM((2, 1, MOBA_BLOCK), F32),
            pltpu.VMEM((2, 1, MOBA_BLOCK), F32),
        ],
        compiler_params=_params("parallel", "parallel", "arbitrary"),
        name="moba",
    )(qb, kb, vb)


DIL_CHUNK = DIL_PATTERNS[-1][1] * BAND
DIL_UNROLL = 4


def _dil_kernel(q_ref, kp_ref, k_ref, vp_ref, v_ref, o_ref, kk_ref, vv_ref, acc_ref, m_ref, l_ref):
    c = pl.program_id(2)
    kk_ref[0:DIL_CHUNK, :] = kp_ref[...]
    kk_ref[DIL_CHUNK:, :] = k_ref[...]
    vv_ref[0:DIL_CHUNK, :] = vp_ref[...]
    vv_ref[DIL_CHUNK:, :] = v_ref[...]
    blocks = DIL_CHUNK // BAND
    shape = (BAND, LANES)

    for pi, (_, d) in enumerate(DIL_PATTERNS):
        per_class = blocks // d

        def load(t, d=d, per_class=per_class):
            r = t // per_class
            j = t % per_class
            q0 = r + BAND * d * j
            k0 = DIL_CHUNK + q0 - BAND * d
            if d == 1:
                qrows = pl.ds(pl.multiple_of(q0, BAND), BAND)
                krows = pl.ds(pl.multiple_of(k0, BAND), 2 * BAND)
            else:
                qrows = pl.ds(q0, BAND, stride=d)
                krows = pl.ds(k0, 2 * BAND, stride=d)
            scores = _pair_dots(q_ref[qrows, :].astype(BF16), kk_ref[krows, :])
            return qrows, krows, (c * per_class + j) * BAND, scores

        def body(t, carry, pi=pi):
            group = [load(t * DIL_UNROLL + u) for u in range(DIL_UNROLL)]
            for qrows, krows, block_start, scores in group:
                mask = _band_mask(DIL_MAX_DIST, block_start)
                (m0, p0, l0), (m1, p1, l1) = _pair_softmax(scores, mask)
                acc_ref[pi, qrows, :] = _pair_pv(p0, p1, vv_ref[krows, :])
                m_ref[pi, qrows, :] = _by_head(m0, m1, shape)
                l_ref[pi, qrows, :] = _by_head(l0, l1, shape)
            return carry

        lax.fori_loop(0, blocks // DIL_UNROLL, body, 0)

    def combine(t, carry):
        rows = pl.ds(pl.multiple_of(t * BAND, BAND), BAND)
        ms = [m_ref[pi, rows, :] for pi in range(len(DIL_PATTERNS))]
        m_max = functools.reduce(jnp.maximum, ms)
        num = jnp.zeros(shape, F32)
        den = jnp.zeros(shape, F32)
        for pi, m in enumerate(ms):
            e = jnp.exp(m - m_max)
            num = num + acc_ref[pi, rows, :] * e
            den = den + l_ref[pi, rows, :] * e
        o_ref[rows, :] = (num / den).astype(o_ref.dtype)
        return carry

    lax.fori_loop(0, blocks, combine, 0)


def _dilated(qc, kc, vc, batch, seq):
    n = qc.shape[0]
    assert seq % DIL_CHUNK == 0
    chunks = seq // DIL_CHUNK
    pairs = WIDTH_C // LANES
    cur = lambda b, p, c: (b * chunks + c, p)
    prev = lambda b, p, c: (b * chunks + jnp.maximum(c - 1, 0), p)
    blk = (DIL_CHUNK, LANES)
    n_pat = len(DIL_PATTERNS)
    return pl.pallas_call(
        _dil_kernel,
        grid=(batch, pairs, chunks),
        in_specs=[pl.BlockSpec(blk, cur), pl.BlockSpec(blk, prev), pl.BlockSpec(blk, cur),
                  pl.BlockSpec(blk, prev), pl.BlockSpec(blk, cur)],
        out_specs=pl.BlockSpec(blk, cur),
        out_shape=jax.ShapeDtypeStruct((n, WIDTH_C), BF16),
        scratch_shapes=[
            pltpu.VMEM((2 * DIL_CHUNK, LANES), F32),
            pltpu.VMEM((2 * DIL_CHUNK, LANES), F32),
            pltpu.VMEM((n_pat, DIL_CHUNK, LANES), F32),
            pltpu.VMEM((n_pat, DIL_CHUNK, LANES), F32),
            pltpu.VMEM((n_pat, DIL_CHUNK, LANES), F32),
        ],
        compiler_params=_params("parallel", "parallel", "parallel"),
        name="dilated",
    )(qc, kc, kc, vc, vc)


def _merge_kernel(x_ref, g_ref, oa_ref, ob_ref, oc_ref, wg_ref, wa_ref, wb_ref, wc_ref, wo_ref, y_ref):
    x = x_ref[...]
    d = x.shape[1]
    h = _rms(x, g_ref[...]).astype(BF16)
    merged = None
    for br, (o_ref, w_ref) in enumerate(((oa_ref, wa_ref), (ob_ref, wb_ref), (oc_ref, wc_ref))):
        gate = _sigmoid(_dot(h, wg_ref[:, br * d:(br + 1) * d]))
        y = gate * _dot(o_ref[...], w_ref[...])
        merged = y if merged is None else merged + y
    y_ref[...] = x + _dot(merged.astype(BF16), wo_ref[...])


def _merge(x2, gain, oa, ob, oc, w_gates, w_a, w_b, w_c, w_o, tm):
    n, d = x2.shape
    row = lambda i: (i, 0)
    full = lambda i: (0, 0)
    return pl.pallas_call(
        _merge_kernel,
        grid=(n // tm,),
        in_specs=[
            pl.BlockSpec((tm, d), row),
            pl.BlockSpec((1, d), full),
            pl.BlockSpec((tm, WIDTH_A), row),
            pl.BlockSpec((tm, WIDTH_B), row),
            pl.BlockSpec((tm, WIDTH_C), row),
            pl.BlockSpec(w_gates.shape, full),
            pl.BlockSpec(w_a.shape, full),
            pl.BlockSpec(w_b.shape, full),
            pl.BlockSpec(w_c.shape, full),
            pl.BlockSpec(w_o.shape, full),
        ],
        out_specs=pl.BlockSpec((tm, d), row),
        out_shape=jax.ShapeDtypeStruct((n, d), F32),
        compiler_params=_params("parallel"),
        name="merge",
    )(x2, gain, oa, ob, oc, w_gates, w_a, w_b, w_c, w_o)


def _swiglu(h, wg, wu, wd):
    a = _dot(h, wg)
    u = _dot(h, wu)
    return _dot((a * _sigmoid(a) * u).astype(BF16), wd)


def _ffn_kernel(x_ref, g_ref, wg_ref, wu_ref, wd_ref, y_ref):
    x = x_ref[...]
    h = _rms(x, g_ref[...]).astype(BF16)
    y_ref[...] = x + _swiglu(h, wg_ref[...], wu_ref[...], wd_ref[...])


def _ffn(x2, gain, wg, wu, wd, tm):
    n, d = x2.shape
    row = lambda i: (i, 0)
    full = lambda i: (0, 0)
    return pl.pallas_call(
        _ffn_kernel,
        grid=(n // tm,),
        in_specs=[
            pl.BlockSpec((tm, d), row),
            pl.BlockSpec((1, d), full),
            pl.BlockSpec(wg.shape, full),
            pl.BlockSpec(wu.shape, full),
            pl.BlockSpec(wd.shape, full),
        ],
        out_specs=pl.BlockSpec((tm, d), row),
        out_shape=jax.ShapeDtypeStruct((n, d), F32),
        compiler_params=_params("parallel"),
        name="ffn",
    )(x2, gain, wg, wu, wd)


MOE_GROUP = 2048
MOE_PACK_BLOCK = 256
MOE_CHUNK = 128

def _split_bf16(a):
    hi = a.astype(BF16)
    return hi, (a - hi.astype(F32)).astype(BF16)


def _router_kernel(x_ref, g_ref, wr_ref, h_ref, comb_ref, rank_ref, rankt_ref, count_ref):
    tm = x_ref.shape[0]
    tile_in_group = pl.program_id(0) % (MOE_GROUP // tm)
    h = _rms(x_ref[...], g_ref[...])
    h_ref[...] = h.astype(BF16)
    h_hi, h_lo = _split_bf16(h)
    w_hi, w_lo = _split_bf16(wr_ref[...])
    logits = _dot(h_hi, w_hi) + (_dot(h_hi, w_lo) + _dot(h_lo, w_hi))
    lane = _lane(logits.shape)
    logits = jnp.where(lane < N_EXPERTS, logits, -jnp.inf)
    tops = []
    for _ in range(TOP_K):
        mx = jnp.max(logits, axis=-1, keepdims=True)
        first = jnp.min(jnp.where(logits == mx, lane, LANES), axis=-1, keepdims=True)
        tops.append((mx, first))
        logits = jnp.where(lane == first, -jnp.inf, logits)
    (v1, i1), (v2, i2) = tops
    e2 = jnp.exp(v2 - v1)
    w1 = 1.0 / (1.0 + e2)
    w2 = e2 / (1.0 + e2)
    sel = jnp.where(lane == i1, 1.0, 0.0) + jnp.where(lane == i2, 1.0, 0.0)
    comb_ref[...] = jnp.where(lane == i1, w1, 0.0) + jnp.where(lane == i2, w2, 0.0)

    @pl.when(tile_in_group == 0)
    def _():
        count_ref[...] = jnp.zeros_like(count_ref)

    r = lax.broadcasted_iota(jnp.int32, (tm, tm), 0)
    c = lax.broadcasted_iota(jnp.int32, (tm, tm), 1)
    before = _dot(jnp.where(c < r, 1.0, 0.0).astype(BF16), sel.astype(BF16)) + count_ref[...]
    rank = jnp.where(sel > 0.0, before, -1.0)
    count_ref[...] = count_ref[...] + jnp.sum(sel, axis=0, keepdims=True)
    rank_ref[...] = rank
    rankt_ref[0] = rank.T[:N_EXPERTS, :]


def _router(x2, gain, w_router_pad, tm):
    n, d = x2.shape
    assert n % MOE_GROUP == 0 and MOE_GROUP % tm == 0
    per_group = MOE_GROUP // tm
    row = lambda i: (i, 0)
    return pl.pallas_call(
        _router_kernel,
        grid=(n // tm,),
        in_specs=[
            pl.BlockSpec((tm, d), row),
            pl.BlockSpec((1, d), lambda i: (0, 0)),
            pl.BlockSpec((d, LANES), lambda i: (0, 0)),
        ],
        out_specs=[pl.BlockSpec((tm, d), row),
                   pl.BlockSpec((tm, LANES), row),
                   pl.BlockSpec((tm, LANES), row),
                   pl.BlockSpec((1, N_EXPERTS, tm), lambda i: (i // per_group, 0, i % per_group))],
        out_shape=[jax.ShapeDtypeStruct((n, d), BF16),
                   jax.ShapeDtypeStruct((n, LANES), F32),
                   jax.ShapeDtypeStruct((n, LANES), F32),
                   jax.ShapeDtypeStruct((n // MOE_GROUP, N_EXPERTS, MOE_GROUP), F32)],
        scratch_shapes=[pltpu.VMEM((1, LANES), F32)],
        compiler_params=_params("arbitrary"),
        name="router",
    )(x2, gain, w_router_pad)


def _experts_kernel(h_ref, comb_ref, rank_ref, rankt_ref, wg_ref, wu_ref, wd_ref, y_ref,
                    xs_ref, acc_ref, col_ref, start_ref):
    e = pl.program_id(1)
    f = pl.program_id(2)
    group, d = h_ref.shape
    n_blocks = group // MOE_CHUNK
    per_pack = MOE_PACK_BLOCK // MOE_CHUNK
    blocks = [slice(b * MOE_CHUNK, (b + 1) * MOE_CHUNK) for b in range(n_blocks)]
    pack_blocks = [slice(b * MOE_PACK_BLOCK, (b + 1) * MOE_PACK_BLOCK) for b in range(group // MOE_PACK_BLOCK)]

    def chunk_rows(m, n=1):
        return pl.ds(pl.multiple_of(m * MOE_CHUNK, MOE_CHUNK), n * MOE_CHUNK)

    def n_chunks():
        return (start_ref[n_blocks] + MOE_CHUNK - 1) // MOE_CHUNK

    def overlaps(m, b):
        return ((start_ref[b * per_pack] < (m + 1) * MOE_CHUNK)
                & (start_ref[(b + 1) * per_pack] > m * MOE_CHUNK))

    @pl.when((e == 0) & (f == 0))
    def _():
        y_ref[...] = jnp.zeros_like(y_ref)

    @pl.when(f == 0)
    def _dispatch():
        pick = _lane((group, LANES)) == e
        rank_col = jnp.sum(jnp.where(pick, rank_ref[...], 0.0), axis=1, keepdims=True)
        col_ref[0] = rank_col
        col_ref[1] = jnp.sum(jnp.where(pick, comb_ref[...], 0.0), axis=1, keepdims=True)
        rank_row = rankt_ref[0, pl.ds(e, 1), :]
        start_ref[0] = 0
        for b in range(n_blocks):
            last = jnp.max(rank_row[:, blocks[b]]).astype(jnp.int32) + 1
            start_ref[b + 1] = jnp.maximum(start_ref[b], last)

        def pack(m, carry):
            dest = (lax.broadcasted_iota(jnp.int32, (MOE_CHUNK, MOE_PACK_BLOCK), 0)
                    + m * MOE_CHUNK).astype(F32)
            xs_ref[chunk_rows(m), :] = jnp.zeros((MOE_CHUNK, d), BF16)
            acc_ref[chunk_rows(m), :] = jnp.zeros((MOE_CHUNK, d), F32)
            for b, tokens in enumerate(pack_blocks):
                @pl.when(overlaps(m, b))
                def _():
                    onehot = jnp.where(rank_row[:, tokens] == dest, 1.0, 0.0).astype(BF16)
                    rows = _dot(onehot, h_ref[tokens, :])
                    xs_ref[chunk_rows(m), :] = (xs_ref[chunk_rows(m), :].astype(F32) + rows).astype(BF16)
            return carry

        lax.fori_loop(0, n_chunks(), pack, 0)
        acc_ref[chunk_rows(n_chunks(), 2), :] = jnp.zeros((2 * MOE_CHUNK, d), F32)

    def expert(m, carry):
        xm = xs_ref[chunk_rows(m), :]
        acc_ref[chunk_rows(m), :] += _swiglu(xm, wg_ref[0], wu_ref[0], wd_ref[0])
        return carry

    lax.fori_loop(0, n_chunks(), expert, 0)

    @pl.when(f == pl.num_programs(2) - 1)
    def _combine():
        for b, tokens in enumerate(blocks):
            first = start_ref[b] // MOE_CHUNK
            y_hi, y_lo = _split_bf16(acc_ref[chunk_rows(first, 2), :])
            dest = (_lane((MOE_CHUNK, 2 * MOE_CHUNK)) + first * MOE_CHUNK).astype(F32)
            onehot = jnp.where(col_ref[0, tokens, :] == dest, 1.0, 0.0).astype(BF16)
            y_ref[tokens, :] += col_ref[1, tokens, :] * (_dot(onehot, y_hi) + _dot(onehot, y_lo))


def _experts(hb, comb, rank, rank_t, wg, wu, wd, ff_split):
    n, d = hb.shape
    n_e, _, d_ff = wg.shape
    ff = d_ff // ff_split
    group = lambda s, e, f: (s, 0)
    once = pl.Buffered(1)
    return pl.pallas_call(
        _experts_kernel,
        grid=(n // MOE_GROUP, n_e, ff_split),
        in_specs=[
            pl.BlockSpec((MOE_GROUP, d), group, pipeline_mode=once),
            pl.BlockSpec((MOE_GROUP, LANES), group, pipeline_mode=once),
            pl.BlockSpec((MOE_GROUP, LANES), group, pipeline_mode=once),
            pl.BlockSpec((1, N_EXPERTS, MOE_GROUP), lambda s, e, f: (s, 0, 0)),
            pl.BlockSpec((1, d, ff), lambda s, e, f: (e, 0, f)),
            pl.BlockSpec((1, d, ff), lambda s, e, f: (e, 0, f)),
            pl.BlockSpec((1, ff, d), lambda s, e, f: (e, f, 0)),
        ],
        out_specs=pl.BlockSpec((MOE_GROUP, d), group, pipeline_mode=once),
        out_shape=jax.ShapeDtypeStruct((n, d), F32),
        scratch_shapes=[
            pltpu.VMEM((MOE_GROUP, d), BF16),
            pltpu.VMEM((MOE_GROUP + 2 * MOE_CHUNK, d), F32),
            pltpu.VMEM((2, MOE_GROUP, 1), F32),
            pltpu.SMEM((MOE_GROUP // MOE_CHUNK + 1,), jnp.int32),
        ],
        compiler_params=_params("parallel", "arbitrary", "arbitrary"),
        name="experts",
    )(hb, comb, rank, rank_t, wg, wu, wd)


def _add_kernel(x_ref, y_ref, o_ref):
    o_ref[...] = x_ref[...] + y_ref[...]


def _add_norm_kernel(x_ref, y_ref, g_ref, o_ref):
    o_ref[...] = _rms(x_ref[...] + y_ref[...], g_ref[...])


def _norm_kernel(x_ref, g_ref, o_ref):
    o_ref[...] = _rms(x_ref[...], g_ref[...])


def _rowwise(body, name, tm, x2, *rest):
    n, d = x2.shape
    row = pl.BlockSpec((tm, d), lambda i: (i, 0))
    specs = [row if a.shape == (n, d) else pl.BlockSpec((1, d), lambda i: (0, 0)) for a in (x2,) + rest]
    return pl.pallas_call(
        body,
        grid=(n // tm,),
        in_specs=specs,
        out_specs=row,
        out_shape=jax.ShapeDtypeStruct((n, d), F32),
        compiler_params=_params("parallel"),
        name=name,
    )(x2, *rest)


def _rope_tables(seq):
    pos = jnp.arange(seq, dtype=F32)
    inv_freq = ROPE_THETA ** (-jnp.arange(0, HEAD_DIM, 2, dtype=F32) / HEAD_DIM)
    ang = pos[:, None] * inv_freq[None, :]
    cos, sin = jnp.cos(ang), jnp.sin(ang)
    reps = LANES // HEAD_DIM
    cos_t = jnp.tile(jnp.concatenate([cos, cos], axis=-1), (1, reps))
    sin_t = jnp.tile(jnp.concatenate([-sin, sin], axis=-1), (1, reps))
    return cos_t, sin_t


def _pair_heads_by_group(w, axis):
    shape = w.shape
    split = shape[:axis] + (SWA_KV_HEADS, SWA_GROUP, HEAD_DIM) + shape[axis + 1:]
    return jnp.swapaxes(w.reshape(split), axis, axis + 1).reshape(shape)


def _row_tile(n, want):
    t = min(n, want)
    assert n % t == 0
    return t


def kernel(x, norm_mix, w_in, attn_sinks, w_br_a, w_br_b, w_br_c, w_out, norm_ffn, w_ff_gate, w_ff_up, w_ff_down, w_router, w_moe_gate, w_moe_up, w_moe_down, norm_final):
    batch, seq, d = x.shape
    n = batch * seq
    depth = norm_mix.shape[0]
    tm = _row_tile(seq, 512)
    cos_t, sin_t = _rope_tables(seq)
    x2 = x.reshape(n, d).astype(F32)

    for layer in range(depth):
        w_l = w_in[layer]
        w_qa = _pair_heads_by_group(w_l[:, :WIDTH_A], 1)
        w_qkv = jnp.concatenate([w_qa, w_l[:, WIDTH_A:QKV_COLS]], axis=1).astype(BF16)
        w_gates = w_l[:, QKV_COLS:].astype(BF16)
        gain = norm_mix[layer].reshape(1, d)

        qa, ka, va, qb, kb, vb, qc, kc, vc = _inproj(x2, gain, w_qkv, cos_t, sin_t, seq, tm)
        oa = _swa(attn_sinks[layer], qa, ka, va, batch, seq, tm)
        ob = _moba(qb, kb, vb, batch, seq)
        oc = _dilated(qc, kc, vc, batch, seq)
        x2 = _merge(x2, gain, oa, ob, oc, w_gates,
                    _pair_heads_by_group(w_br_a[layer], 0).astype(BF16),
                    w_br_b[layer].astype(BF16), w_br_c[layer].astype(BF16),
                    w_out[layer].astype(BF16), tm)

        gain_f = norm_ffn[layer].reshape(1, d)
        idx = layer // 2
        if layer % 2 == 0:
            x2 = _ffn(x2, gain_f, w_ff_gate[idx].astype(BF16), w_ff_up[idx].astype(BF16),
                      w_ff_down[idx].astype(BF16), tm)
        else:
            w_r = jnp.pad(w_router[idx], ((0, 0), (0, LANES - N_EXPERTS)))
            hb, comb, rank, rank_t = _router(x2, gain_f, w_r, tm)
            y = _experts(hb, comb, rank, rank_t, w_moe_gate[idx].astype(BF16),
                         w_moe_up[idx].astype(BF16), w_moe_down[idx].astype(BF16), 2)
            if layer == depth - 1:
                out = _rowwise(_add_norm_kernel, "add_norm", tm, x2, y, norm_final.reshape(1, d))
                return out.reshape(batch, seq, d)
            x2 = _rowwise(_add_kernel, "add", tm, x2, y)

    return _rowwise(_norm_kernel, "final_norm", tm, x2, norm_final.reshape(1, d)).reshape(batch, seq, d)
```

```python
import functools

import jax
import jax.numpy as jnp
import numpy as np
from jax import lax
from jax.experimental import pallas as pl
from jax.experimental.pallas import tpu as pltpu

F32 = jnp.float32
BF16 = jnp.bfloat16

HEAD_DIM = 64
ATTN_SCALE = HEAD_DIM ** -0.5
ROPE_THETA = 10000.0
RMS_EPS = 1e-5
NEG_INF = -1e30
BAND = 128
SWA_Q_HEADS, SWA_KV_HEADS = 6, 2
SWA_GROUP = SWA_Q_HEADS // SWA_KV_HEADS
SWA_MAX_DIST = 127
MOBA_HEADS, MOBA_BLOCK, MOBA_TOPK = 4, 256, 3
DIL_HEADS = 6
DIL_PATTERNS = ((128, 1), (512, 4), (2048, 16))
DIL_MAX_DIST = 128
N_EXPERTS, TOP_K = 8, 2

LANES = 128
VMEM_LIMIT = 56 * 1024 * 1024

WIDTH_A = SWA_Q_HEADS * HEAD_DIM
KV_A = SWA_KV_HEADS * HEAD_DIM
WIDTH_B = MOBA_HEADS * HEAD_DIM
WIDTH_C = DIL_HEADS * HEAD_DIM
QKV_COLS = WIDTH_A + 2 * KV_A + 3 * WIDTH_B + 3 * WIDTH_C


def _params(*sem):
    return pltpu.CompilerParams(dimension_semantics=sem, vmem_limit_bytes=VMEM_LIMIT)


def _rms(x, gain):
    ms = jnp.mean(x * x, axis=-1, keepdims=True)
    return x * lax.rsqrt(ms + RMS_EPS) * gain


def _dot(a, b):
    return jnp.dot(a, b, preferred_element_type=F32)


def _dot_t(a, b):
    return lax.dot_general(a, b, (((1,), (1,)), ((), ())), preferred_element_type=F32)


def _sigmoid(z):
    return 1.0 / (1.0 + jnp.exp(-z))


def _lane(shape):
    return lax.broadcasted_iota(jnp.int32, shape, len(shape) - 1)


def _inproj_kernel(x_ref, g_ref, w_ref, cos_ref, sin_ref,
                   qa_ref, ka_ref, va_ref, qb_ref, kb_ref, vb_ref, qc_ref, kc_ref, vc_ref):
    tm = x_ref.shape[0]
    h = _rms(x_ref[...], g_ref[...]).astype(BF16)
    cos = cos_ref[...]
    sin = sin_ref[...]
    first_half = (_lane((tm, LANES)) & (HEAD_DIM // 2)) == 0

    def rope(z):
        rot = jnp.where(first_half,
                        pltpu.roll(z, LANES - HEAD_DIM // 2, 1),
                        pltpu.roll(z, HEAD_DIM // 2, 1))
        return z * cos + rot * sin

    def proj(c0, width):
        return _dot(h, w_ref[:, c0:c0 + width])

    def store(ref, z, roped, scale, dtype):
        for t in range(z.shape[1] // LANES):
            zt = z[:, t * LANES:(t + 1) * LANES]
            if roped:
                zt = rope(zt)
            if scale:
                zt = zt * ATTN_SCALE
            ref[:, t * LANES:(t + 1) * LANES] = zt.astype(dtype)

    c = 0
    store(qa_ref, proj(c, WIDTH_A), True, True, BF16); c += WIDTH_A
    kv = proj(c, 2 * KV_A); c += 2 * KV_A
    store(ka_ref, kv[:, :KV_A], True, False, BF16)
    store(va_ref, kv[:, KV_A:], False, False, BF16)
    store(qb_ref, proj(c, WIDTH_B), True, True, BF16); c += WIDTH_B
    store(kb_ref, proj(c, WIDTH_B), True, False, BF16); c += WIDTH_B
    store(vb_ref, proj(c, WIDTH_B), False, False, BF16); c += WIDTH_B
    store(qc_ref, proj(c, WIDTH_C), True, True, F32); c += WIDTH_C
    store(kc_ref, proj(c, WIDTH_C), True, False, F32); c += WIDTH_C
    store(vc_ref, proj(c, WIDTH_C), False, False, F32)


def _inproj(x2, gain, w_qkv, cos_t, sin_t, seq, tm):
    n, d = x2.shape
    tiles_per_seq = seq // tm
    row = lambda i: (i, 0)
    widths = (WIDTH_A, KV_A, KV_A, WIDTH_B, WIDTH_B, WIDTH_B, WIDTH_C, WIDTH_C, WIDTH_C)
    dtypes = (BF16,) * 6 + (F32,) * 3
    return pl.pallas_call(
        _inproj_kernel,
        grid=(n // tm,),
        in_specs=[
            pl.BlockSpec((tm, d), row),
            pl.BlockSpec((1, d), lambda i: (0, 0)),
            pl.BlockSpec((d, QKV_COLS), lambda i: (0, 0)),
            pl.BlockSpec((tm, LANES), lambda i: (i % tiles_per_seq, 0)),
            pl.BlockSpec((tm, LANES), lambda i: (i % tiles_per_seq, 0)),
        ],
        out_specs=[pl.BlockSpec((tm, w), row) for w in widths],
        out_shape=[jax.ShapeDtypeStruct((n, w), dt) for w, dt in zip(widths, dtypes)],
        compiler_params=_params("parallel"),
        name="inproj",
    )(x2, gain, w_qkv, cos_t, sin_t)


def _band_mask(max_dist, block_start, stack=1):
    qi = lax.broadcasted_iota(jnp.int32, (stack * BAND, 2 * BAND), 0) % BAND
    kj = lax.broadcasted_iota(jnp.int32, (stack * BAND, 2 * BAND), 1) - BAND
    diff = qi - kj
    return (diff >= 0) & (diff <= max_dist) & (kj + block_start >= 0)


def _split_heads(a):
    a32 = a.astype(F32)
    lo = _lane(a.shape) < HEAD_DIM
    return jnp.where(lo, a32, 0.0).astype(BF16), jnp.where(lo, 0.0, a32).astype(BF16)


def _pair_dots(q, k):
    return [_dot_t(q, kh) for kh in _split_heads(k)]


def _pair_softmax(scores, mask):
    out = []
    for s in scores:
        s = jnp.where(mask, s, NEG_INF)
        m = jnp.max(s, axis=-1, keepdims=True)
        p = jnp.exp(s - m)
        out.append((m, p, jnp.sum(p, axis=-1, keepdims=True)))
    return out


def _pair_pv(p0, p1, v):
    pc = jnp.concatenate([p0.astype(BF16), p1.astype(BF16)], axis=1)
    return _dot(pc, jnp.concatenate(_split_heads(v), axis=0))


def _by_head(lo_val, hi_val, shape):
    return jnp.where(_lane(shape) < HEAD_DIM, lo_val, hi_val)


def _swa_kernel(sink_ref, q_ref, k_ref, kp_ref, v_ref, vp_ref, o_ref):
    nblk = q_ref.shape[0] // BAND
    chunk_start = pl.program_id(1) * q_ref.shape[0]
    n_pairs = WIDTH_A // LANES
    def load(j):
        rows = slice(j * BAND, (j + 1) * BAND)
        prev = slice((j - 1) * BAND, j * BAND)
        q = q_ref[rows, :]
        qs = jnp.concatenate([q[:, t * LANES:(t + 1) * LANES] for t in range(n_pairs)], axis=0)
        if j == 0:
            k = jnp.concatenate([kp_ref[...], k_ref[rows, :]], axis=0)
            v = jnp.concatenate([vp_ref[...], v_ref[rows, :]], axis=0)
        else:
            k = jnp.concatenate([k_ref[prev, :], k_ref[rows, :]], axis=0)
            v = jnp.concatenate([v_ref[prev, :], v_ref[rows, :]], axis=0)
        return rows, v, _pair_dots(qs, k)

    blocks = [load(j) for j in range(nblk)]
    for j, (rows, v, scores) in enumerate(blocks):
        mask = _band_mask(SWA_MAX_DIST, chunk_start + j * BAND, n_pairs)
        stats = _pair_softmax(scores, mask)
        acc = _pair_pv(stats[0][1], stats[1][1], v)
        scales = []
        for g, (m, _, l) in enumerate(stats):
            sink = jnp.concatenate(
                [jnp.full((BAND, 1), sink_ref[SWA_GROUP * g + t], F32) for t in range(n_pairs)],
                axis=0)
            m_all = jnp.maximum(m, sink)
            keep = jnp.exp(m - m_all)
            den = l * keep + jnp.exp(sink - m_all)
            scales.append(keep / den)
        out = acc * _by_head(scales[0], scales[1], acc.shape)
        for t in range(n_pairs):
            o_ref[rows, t * LANES:(t + 1) * LANES] = out[t * BAND:(t + 1) * BAND, :].astype(o_ref.dtype)


def _swa(sinks, qa, ka, va, batch, seq, tq):
    n = qa.shape[0]
    chunks = seq // tq
    blk = tq // BAND
    cur = lambda b, c: (b * chunks + c, 0)
    prev = lambda b, c: (jnp.maximum((b * chunks + c) * blk - 1, 0), 0)
    return pl.pallas_call(
        _swa_kernel,
        grid=(batch, chunks),
        in_specs=[
            pl.BlockSpec(memory_space=pltpu.SMEM),
            pl.BlockSpec((tq, WIDTH_A), cur),
            pl.BlockSpec((tq, KV_A), cur),
            pl.BlockSpec((BAND, KV_A), prev),
            pl.BlockSpec((tq, KV_A), cur),
            pl.BlockSpec((BAND, KV_A), prev),
        ],
        out_specs=pl.BlockSpec((tq, WIDTH_A), cur),
        out_shape=jax.ShapeDtypeStruct((n, WIDTH_A), BF16),
        compiler_params=_params("parallel", "parallel"),
        name="swa",
    )(sinks, qa, ka, ka, va, va)


MOBA_UNROLL = 4


def _moba_kernel(q_ref, k_ref, v_ref, o_ref, kaug_ref, vt_ref, kmean_ref, acc_ref, m_ref, l_ref, s_ref):
    i = pl.program_id(2)
    nb = k_ref.shape[0] // MOBA_BLOCK
    blk_shape = (MOBA_BLOCK, LANES)
    lane = _lane(blk_shape)
    lo = lane < HEAD_DIM

    @pl.when(i == 0)
    def _prepare():
        kmean_ref[...] = jnp.zeros_like(kmean_ref)

        def body(n, carry):
            rows = pl.ds(pl.multiple_of(n * MOBA_BLOCK, MOBA_BLOCK), MOBA_BLOCK)
            kb = k_ref[rows, :].astype(F32)
            kmean_ref[pl.ds(n, 1), :] = jnp.sum(kb, axis=0, keepdims=True) * (1.0 / MOBA_BLOCK)
            block_hot = jnp.where((lane == n) | (lane == HEAD_DIM + n), 1.0, 0.0)
            kaug_ref[0, rows, :] = jnp.where(lo, kb, block_hot).astype(BF16)
            kaug_ref[1, rows, :] = jnp.where(lo, block_hot, kb).astype(BF16)
            vt_ref[n] = v_ref[rows, :].astype(F32).T.astype(BF16)
            return carry

        lax.fori_loop(0, nb, body, 0)

    q = q_ref[...]
    kmean = kmean_ref[...]
    lo_m = _lane(kmean.shape) < HEAD_DIM
    sel_shape = (HEAD_DIM, MOBA_BLOCK)
    blk = lax.broadcasted_iota(jnp.int32, sel_shape, 0)
    blk_f = blk.astype(F32)
    pens = []
    for km in (jnp.where(lo_m, kmean, 0.0), jnp.where(lo_m, 0.0, kmean)):
        gate = _dot_t(km.astype(BF16), q)
        gate = jnp.where(blk < i, gate, -jnp.inf)
        sel = blk == i
        for _ in range(MOBA_TOPK):
            mx = jnp.max(gate, axis=0, keepdims=True)
            first = jnp.min(jnp.where(gate == mx, blk_f, float(HEAD_DIM)), axis=0, keepdims=True)
            pick = (blk_f == first) & (mx > -jnp.inf)
            sel = sel | pick
            gate = jnp.where(pick, -jnp.inf, gate)
        pens.append(jnp.where(sel, 0.0, NEG_INF))
    zeros = jnp.zeros(sel_shape, F32)
    q32 = q.astype(F32)
    qa0 = jnp.where(lo, q32, jnp.concatenate([zeros, pens[0]], axis=0).T).astype(BF16)
    qa1 = jnp.where(lo, jnp.concatenate([pens[1], zeros], axis=0).T, q32).astype(BF16)

    acc_ref[...] = jnp.zeros_like(acc_ref)
    m_ref[...] = jnp.full_like(m_ref, NEG_INF)
    l_ref[...] = jnp.zeros_like(l_ref)

    n_trips = i // MOBA_UNROLL + 1

    def scores(t, n):
        j = jnp.minimum(t * MOBA_UNROLL + n, i)
        rows = pl.ds(pl.multiple_of(j * MOBA_BLOCK, MOBA_BLOCK), MOBA_BLOCK)
        return [_dot_t(kaug_ref[hd, rows, :], qa) for hd, qa in enumerate((qa0, qa1))]

    def update(j, blk_scores, masked):
        vt = vt_ref[jnp.minimum(j, i)]
        ps, alphas = [], []
        for hd, s in enumerate(blk_scores):
            if masked:
                kk = lax.broadcasted_iota(jnp.int32, s.shape, 0)
                qq = lax.broadcasted_iota(jnp.int32, s.shape, 1)
                s = jnp.where(kk <= qq, s, NEG_INF)
            m_old = m_ref[hd]
            m_new = jnp.maximum(m_old, jnp.max(s, axis=0, keepdims=True))
            alpha = jnp.exp(m_old - m_new)
            p = jnp.exp(s - m_new)
            l_ref[hd] = alpha * l_ref[hd] + jnp.sum(p, axis=0, keepdims=True)
            m_ref[hd] = m_new
            ps.append(p.astype(BF16))
            alphas.append(alpha)
        for hd in range(2):
            pv = _dot(vt[hd * HEAD_DIM:(hd + 1) * HEAD_DIM, :], ps[hd])
            acc_ref[hd] = acc_ref[hd] * alphas[hd] + pv

    for n in range(MOBA_UNROLL):
        for hd, s in enumerate(scores(0, n)):
            s_ref[n, hd] = s

    def body(t, carry):
        for n in range(MOBA_UNROLL):
            cur = [s_ref[n, hd] for hd in range(2)]
            ahead = scores(t + 1, n)
            update(t * MOBA_UNROLL + n, cur, False)
            for hd in range(2):
                s_ref[n, hd] = ahead[hd]
        return carry

    lax.fori_loop(0, n_trips - 1, body, 0)
    for last in range(MOBA_UNROLL):
        @pl.when(i % MOBA_UNROLL == last)
        def _(last=last):
            for n in range(last + 1):
                update(i - last + n, [s_ref[n, hd] for hd in range(2)], n == last)
    out_t = jnp.concatenate([acc_ref[hd] * (1.0 / l_ref[hd]) for hd in range(2)], axis=0)
    o_ref[...] = out_t.T.astype(o_ref.dtype)


def _moba(qb, kb, vb, batch, seq):
    n = qb.shape[0]
    nb = seq // MOBA_BLOCK
    pairs = WIDTH_B // LANES
    assert nb <= HEAD_DIM, "block one-hot must fit in the other head's lanes"
    qmap = lambda b, p, i: (b * nb + i, p)
    kvmap = lambda b, p, i: (b, p)
    return pl.pallas_call(
        _moba_kernel,
        grid=(batch, pairs, nb),
        in_specs=[
            pl.BlockSpec((MOBA_BLOCK, LANES), qmap),
            pl.BlockSpec((seq, LANES), kvmap),
            pl.BlockSpec((seq, LANES), kvmap),
        ],
        out_specs=pl.BlockSpec((MOBA_BLOCK, LANES), qmap),
        out_shape=jax.ShapeDtypeStruct((n, WIDTH_B), BF16),
        scratch_shapes=[
            pltpu.VMEM((2, seq, LANES), BF16),
            pltpu.VMEM((nb, LANES, MOBA_BLOCK), BF16),
            pltpu.VMEM((HEAD_DIM, LANES), F32),
            pltpu.VMEM((2, HEAD_DIM, MOBA_BLOCK), F32),
            pltpu.VMEM((2, 1, MOBA_BLOCK), F32),
            pltpu.VMEM((2, 1, MOBA_BLOCK), F32),
            pltpu.VMEM((MOBA_UNROLL, 2, MOBA_BLOCK, MOBA_BLOCK), F32),
        ],
        compiler_params=_params("parallel", "parallel", "arbitrary"),
        name="moba",
    )(qb, kb, vb)


DIL_CHUNK = DIL_PATTERNS[-1][1] * BAND
DIL_UNROLL = 4


def _dil_kernel(q_ref, kp_ref, k_ref, vp_ref, v_ref, o_ref, kk_ref, vv_ref, acc_ref, m_ref, l_ref):
    c = pl.program_id(2)
    kk_ref[0:DIL_CHUNK, :] = kp_ref[...]
    kk_ref[DIL_CHUNK:, :] = k_ref[...]
    vv_ref[0:DIL_CHUNK, :] = vp_ref[...]
    vv_ref[DIL_CHUNK:, :] = v_ref[...]
    blocks = DIL_CHUNK // BAND
    shape = (BAND, LANES)

    for pi, (_, d) in enumerate(DIL_PATTERNS):
        per_class = blocks // d

        def load(t, d=d, per_class=per_class):
            r = t // per_class
            j = t % per_class
            q0 = r + BAND * d * j
            k0 = DIL_CHUNK + q0 - BAND * d
            if d == 1:
                qrows = pl.ds(pl.multiple_of(q0, BAND), BAND)
                krows = pl.ds(pl.multiple_of(k0, BAND), 2 * BAND)
            else:
                qrows = pl.ds(q0, BAND, stride=d)
                krows = pl.ds(k0, 2 * BAND, stride=d)
            scores = _pair_dots(q_ref[qrows, :].astype(BF16), kk_ref[krows, :])
            return qrows, krows, (c * per_class + j) * BAND, scores

        def body(t, carry, pi=pi, load=load):
            group = [load(t * DIL_UNROLL + u) for u in range(DIL_UNROLL)]
            for qrows, krows, block_start, scores in group:
                mask = _band_mask(DIL_MAX_DIST, block_start)
                (m0, p0, l0), (m1, p1, l1) = _pair_softmax(scores, mask)
                acc_ref[pi, qrows, :] = _pair_pv(p0, p1, vv_ref[krows, :])
                m_ref[pi, qrows, :] = _by_head(m0, m1, shape)
                l_ref[pi, qrows, :] = _by_head(l0, l1, shape)
            return carry

        lax.fori_loop(0, blocks // DIL_UNROLL, body, 0)

    def combine(t, carry):
        rows = pl.ds(pl.multiple_of(t * BAND, BAND), BAND)
        ms = [m_ref[pi, rows, :] for pi in range(len(DIL_PATTERNS))]
        m_max = functools.reduce(jnp.maximum, ms)
        num = jnp.zeros(shape, F32)
        den = jnp.zeros(shape, F32)
        for pi, m in enumerate(ms):
            e = jnp.exp(m - m_max)
            num = num + acc_ref[pi, rows, :] * e
            den = den + l_ref[pi, rows, :] * e
        o_ref[rows, :] = (num / den).astype(o_ref.dtype)
        return carry

    lax.fori_loop(0, blocks, combine, 0)


def _dilated(qc, kc, vc, batch, seq):
    n = qc.shape[0]
    assert seq % DIL_CHUNK == 0
    chunks = seq // DIL_CHUNK
    pairs = WIDTH_C // LANES
    cur = lambda b, p, c: (b * chunks + c, p)
    prev = lambda b, p, c: (b * chunks + jnp.maximum(c - 1, 0), p)
    blk = (DIL_CHUNK, LANES)
    n_pat = len(DIL_PATTERNS)
    return pl.pallas_call(
        _dil_kernel,
        grid=(batch, pairs, chunks),
        in_specs=[pl.BlockSpec(blk, cur), pl.BlockSpec(blk, prev), pl.BlockSpec(blk, cur),
                  pl.BlockSpec(blk, prev), pl.BlockSpec(blk, cur)],
        out_specs=pl.BlockSpec(blk, cur),
        out_shape=jax.ShapeDtypeStruct((n, WIDTH_C), BF16),
        scratch_shapes=[
            pltpu.VMEM((2 * DIL_CHUNK, LANES), F32),
            pltpu.VMEM((2 * DIL_CHUNK, LANES), F32),
            pltpu.VMEM((n_pat, DIL_CHUNK, LANES), F32),
            pltpu.VMEM((n_pat, DIL_CHUNK, LANES), F32),
            pltpu.VMEM((n_pat, DIL_CHUNK, LANES), F32),
        ],
        compiler_params=_params("parallel", "parallel", "parallel"),
        name="dilated",
    )(qc, kc, kc, vc, vc)


def _merge_kernel(x_ref, g_ref, oa_ref, ob_ref, oc_ref, wg_ref, wa_ref, wb_ref, wc_ref, wo_ref, y_ref):
    x = x_ref[...]
    d = x.shape[1]
    h = _rms(x, g_ref[...]).astype(BF16)
    merged = None
    for br, (o_ref, w_ref) in enumerate(((oa_ref, wa_ref), (ob_ref, wb_ref), (oc_ref, wc_ref))):
        gate = _sigmoid(_dot(h, wg_ref[:, br * d:(br + 1) * d]))
        y = gate * _dot(o_ref[...], w_ref[...])
        merged = y if merged is None else merged + y
    y_ref[...] = x + _dot(merged.astype(BF16), wo_ref[...])


def _merge(x2, gain, oa, ob, oc, w_gates, w_a, w_b, w_c, w_o, tm):
    n, d = x2.shape
    row = lambda i: (i, 0)
    full = lambda i: (0, 0)
    return pl.pallas_call(
        _merge_kernel,
        grid=(n // tm,),
        in_specs=[
            pl.BlockSpec((tm, d), row),
            pl.BlockSpec((1, d), full),
            pl.BlockSpec((tm, WIDTH_A), row),
            pl.BlockSpec((tm, WIDTH_B), row),
            pl.BlockSpec((tm, WIDTH_C), row),
            pl.BlockSpec(w_gates.shape, full),
            pl.BlockSpec(w_a.shape, full),
            pl.BlockSpec(w_b.shape, full),
            pl.BlockSpec(w_c.shape, full),
            pl.BlockSpec(w_o.shape, full),
        ],
        out_specs=pl.BlockSpec((tm, d), row),
        out_shape=jax.ShapeDtypeStruct((n, d), F32),
        compiler_params=_params("parallel"),
        name="merge",
    )(x2, gain, oa, ob, oc, w_gates, w_a, w_b, w_c, w_o)


def _swiglu(h, wg, wu, wd):
    a = _dot(h, wg)
    u = _dot(h, wu)
    return _dot((a * _sigmoid(a) * u).astype(BF16), wd)


def _ffn_kernel(x_ref, g_ref, wg_ref, wu_ref, wd_ref, y_ref):
    x = x_ref[...]
    h = _rms(x, g_ref[...]).astype(BF16)
    y_ref[...] = x + _swiglu(h, wg_ref[...], wu_ref[...], wd_ref[...])


def _ffn(x2, gain, wg, wu, wd, tm):
    n, d = x2.shape
    row = lambda i: (i, 0)
    full = lambda i: (0, 0)
    return pl.pallas_call(
        _ffn_kernel,
        grid=(n // tm,),
        in_specs=[
            pl.BlockSpec((tm, d), row),
            pl.BlockSpec((1, d), full),
            pl.BlockSpec(wg.shape, full),
            pl.BlockSpec(wu.shape, full),
            pl.BlockSpec(wd.shape, full),
        ],
        out_specs=pl.BlockSpec((tm, d), row),
        out_shape=jax.ShapeDtypeStruct((n, d), F32),
        compiler_params=_params("parallel"),
        name="ffn",
    )(x2, gain, wg, wu, wd)


MOE_GROUP = 2048
MOE_PACK_BLOCK = 256
MOE_CHUNK = 128

def _split_bf16(a):
    hi = a.astype(BF16)
    return hi, (a - hi.astype(F32)).astype(BF16)


def _router_kernel(x_ref, g_ref, wr_ref, h_ref, comb_ref, rank_ref, rankt_ref, count_ref):
    tm = x_ref.shape[0]
    tile_in_group = pl.program_id(0) % (MOE_GROUP // tm)
    h = _rms(x_ref[...], g_ref[...])
    h_ref[...] = h.astype(BF16)
    h_hi, h_lo = _split_bf16(h)
    w_hi, w_lo = _split_bf16(wr_ref[...])
    logits = _dot(h_hi, w_hi) + (_dot(h_hi, w_lo) + _dot(h_lo, w_hi))
    lane = _lane(logits.shape)
    logits = jnp.where(lane < N_EXPERTS, logits, -jnp.inf)
    tops = []
    for _ in range(TOP_K):
        mx = jnp.max(logits, axis=-1, keepdims=True)
        first = jnp.min(jnp.where(logits == mx, lane, LANES), axis=-1, keepdims=True)
        tops.append((mx, first))
        logits = jnp.where(lane == first, -jnp.inf, logits)
    (v1, i1), (v2, i2) = tops
    e2 = jnp.exp(v2 - v1)
    w1 = 1.0 / (1.0 + e2)
    w2 = e2 / (1.0 + e2)
    sel = jnp.where(lane == i1, 1.0, 0.0) + jnp.where(lane == i2, 1.0, 0.0)
    comb_ref[...] = jnp.where(lane == i1, w1, 0.0) + jnp.where(lane == i2, w2, 0.0)

    @pl.when(tile_in_group == 0)
    def _():
        count_ref[...] = jnp.zeros_like(count_ref)

    r = lax.broadcasted_iota(jnp.int32, (tm, tm), 0)
    c = lax.broadcasted_iota(jnp.int32, (tm, tm), 1)
    before = _dot(jnp.where(c < r, 1.0, 0.0).astype(BF16), sel.astype(BF16)) + count_ref[...]
    rank = jnp.where(sel > 0.0, before, -1.0)
    count_ref[...] = count_ref[...] + jnp.sum(sel, axis=0, keepdims=True)
    rank_ref[...] = rank
    rankt_ref[0] = rank.T[:N_EXPERTS, :]


def _router(x2, gain, w_router_pad, tm):
    n, d = x2.shape
    assert n % MOE_GROUP == 0 and MOE_GROUP % tm == 0
    per_group = MOE_GROUP // tm
    row = lambda i: (i, 0)
    return pl.pallas_call(
        _router_kernel,
        grid=(n // tm,),
        in_specs=[
            pl.BlockSpec((tm, d), row),
            pl.BlockSpec((1, d), lambda i: (0, 0)),
            pl.BlockSpec((d, LANES), lambda i: (0, 0)),
        ],
        out_specs=[pl.BlockSpec((tm, d), row),
                   pl.BlockSpec((tm, LANES), row),
                   pl.BlockSpec((tm, LANES), row),
                   pl.BlockSpec((1, N_EXPERTS, tm), lambda i: (i // per_group, 0, i % per_group))],
        out_shape=[jax.ShapeDtypeStruct((n, d), BF16),
                   jax.ShapeDtypeStruct((n, LANES), F32),
                   jax.ShapeDtypeStruct((n, LANES), F32),
                   jax.ShapeDtypeStruct((n // MOE_GROUP, N_EXPERTS, MOE_GROUP), F32)],
        scratch_shapes=[pltpu.VMEM((1, LANES), F32)],
        compiler_params=_params("arbitrary"),
        name="router",
    )(x2, gain, w_router_pad)


def _experts_kernel(h_ref, comb_ref, rank_ref, rankt_ref, wg_ref, wu_ref, wd_ref, y_ref,
                    xs_ref, acc_ref, col_ref, start_ref):
    e = pl.program_id(1)
    f = pl.program_id(2)
    group, d = h_ref.shape
    n_blocks = group // MOE_CHUNK
    per_pack = MOE_PACK_BLOCK // MOE_CHUNK
    blocks = [slice(b * MOE_CHUNK, (b + 1) * MOE_CHUNK) for b in range(n_blocks)]
    pack_blocks = [slice(b * MOE_PACK_BLOCK, (b + 1) * MOE_PACK_BLOCK) for b in range(group // MOE_PACK_BLOCK)]

    def chunk_rows(m, n=1):
        return pl.ds(pl.multiple_of(m * MOE_CHUNK, MOE_CHUNK), n * MOE_CHUNK)

    def n_chunks():
        return (start_ref[n_blocks] + MOE_CHUNK - 1) // MOE_CHUNK

    def overlaps(m, b):
        return ((start_ref[b * per_pack] < (m + 1) * MOE_CHUNK)
                & (start_ref[(b + 1) * per_pack] > m * MOE_CHUNK))

    @pl.when((e == 0) & (f == 0))
    def _():
        y_ref[...] = jnp.zeros_like(y_ref)

    @pl.when(f == 0)
    def _dispatch():
        pick = _lane((group, LANES)) == e
        rank_col = jnp.sum(jnp.where(pick, rank_ref[...], 0.0), axis=1, keepdims=True)
        col_ref[0] = rank_col
        col_ref[1] = jnp.sum(jnp.where(pick, comb_ref[...], 0.0), axis=1, keepdims=True)
        rank_row = rankt_ref[0, pl.ds(e, 1), :]
        start_ref[0] = 0
        for b in range(n_blocks):
            last = jnp.max(rank_row[:, blocks[b]]).astype(jnp.int32) + 1
            start_ref[b + 1] = jnp.maximum(start_ref[b], last)

        def pack(m, carry):
            dest = (lax.broadcasted_iota(jnp.int32, (MOE_CHUNK, MOE_PACK_BLOCK), 0)
                    + m * MOE_CHUNK).astype(F32)
            xs_ref[chunk_rows(m), :] = jnp.zeros((MOE_CHUNK, d), BF16)
            acc_ref[chunk_rows(m), :] = jnp.zeros((MOE_CHUNK, d), F32)
            for b, tokens in enumerate(pack_blocks):
                @pl.when(overlaps(m, b))
                def _():
                    onehot = jnp.where(rank_row[:, tokens] == dest, 1.0, 0.0).astype(BF16)
                    rows = _dot(onehot, h_ref[tokens, :])
                    xs_ref[chunk_rows(m), :] = (xs_ref[chunk_rows(m), :].astype(F32) + rows).astype(BF16)
            return carry

        lax.fori_loop(0, n_chunks(), pack, 0)
        acc_ref[chunk_rows(n_chunks(), 2), :] = jnp.zeros((2 * MOE_CHUNK, d), F32)

    def expert(m, carry):
        xm = xs_ref[chunk_rows(m), :]
        acc_ref[chunk_rows(m), :] += _swiglu(xm, wg_ref[0], wu_ref[0], wd_ref[0])
        return carry

    lax.fori_loop(0, n_chunks(), expert, 0)

    @pl.when(f == pl.num_programs(2) - 1)
    def _combine():
        for b, tokens in enumerate(blocks):
            first = start_ref[b] // MOE_CHUNK
            y_hi, y_lo = _split_bf16(acc_ref[chunk_rows(first, 2), :])
            dest = (_lane((MOE_CHUNK, 2 * MOE_CHUNK)) + first * MOE_CHUNK).astype(F32)
            onehot = jnp.where(col_ref[0, tokens, :] == dest, 1.0, 0.0).astype(BF16)
            y_ref[tokens, :] += col_ref[1, tokens, :] * (_dot(onehot, y_hi) + _dot(onehot, y_lo))


def _experts(hb, comb, rank, rank_t, wg, wu, wd, ff_split):
    n, d = hb.shape
    n_e, _, d_ff = wg.shape
    ff = d_ff // ff_split
    group = lambda s, e, f: (s, 0)
    once = pl.Buffered(1)
    return pl.pallas_call(
        _experts_kernel,
        grid=(n // MOE_GROUP, n_e, ff_split),
        in_specs=[
            pl.BlockSpec((MOE_GROUP, d), group, pipeline_mode=once),
            pl.BlockSpec((MOE_GROUP, LANES), group, pipeline_mode=once),
            pl.BlockSpec((MOE_GROUP, LANES), group, pipeline_mode=once),
            pl.BlockSpec((1, N_EXPERTS, MOE_GROUP), lambda s, e, f: (s, 0, 0)),
            pl.BlockSpec((1, d, ff), lambda s, e, f: (e, 0, f)),
            pl.BlockSpec((1, d, ff), lambda s, e, f: (e, 0, f)),
            pl.BlockSpec((1, ff, d), lambda s, e, f: (e, f, 0)),
        ],
        out_specs=pl.BlockSpec((MOE_GROUP, d), group, pipeline_mode=once),
        out_shape=jax.ShapeDtypeStruct((n, d), F32),
        scratch_shapes=[
            pltpu.VMEM((MOE_GROUP, d), BF16),
            pltpu.VMEM((MOE_GROUP + 2 * MOE_CHUNK, d), F32),
            pltpu.VMEM((2, MOE_GROUP, 1), F32),
            pltpu.SMEM((MOE_GROUP // MOE_CHUNK + 1,), jnp.int32),
        ],
        compiler_params=_params("parallel", "arbitrary", "arbitrary"),
        name="experts",
    )(hb, comb, rank, rank_t, wg, wu, wd)


def _add_kernel(x_ref, y_ref, o_ref):
    o_ref[...] = x_ref[...] + y_ref[...]


def _add_norm_kernel(x_ref, y_ref, g_ref, o_ref):
    o_ref[...] = _rms(x_ref[...] + y_ref[...], g_ref[...])


def _norm_kernel(x_ref, g_ref, o_ref):
    o_ref[...] = _rms(x_ref[...], g_ref[...])


def _rowwise(body, name, tm, x2, *rest):
    n, d = x2.shape
    row = pl.BlockSpec((tm, d), lambda i: (i, 0))
    specs = [row if a.shape == (n, d) else pl.BlockSpec((1, d), lambda i: (0, 0)) for a in (x2,) + rest]
    return pl.pallas_call(
        body,
        grid=(n // tm,),
        in_specs=specs,
        out_specs=row,
        out_shape=jax.ShapeDtypeStruct((n, d), F32),
        compiler_params=_params("parallel"),
        name=name,
    )(x2, *rest)


def _rope_tables(seq):
    pos = jnp.arange(seq, dtype=F32)
    inv_freq = ROPE_THETA ** (-jnp.arange(0, HEAD_DIM, 2, dtype=F32) / HEAD_DIM)
    ang = pos[:, None] * inv_freq[None, :]
    cos, sin = jnp.cos(ang), jnp.sin(ang)
    reps = LANES // HEAD_DIM
    cos_t = jnp.tile(jnp.concatenate([cos, cos], axis=-1), (1, reps))
    sin_t = jnp.tile(jnp.concatenate([-sin, sin], axis=-1), (1, reps))
    return cos_t, sin_t


def _pair_heads_by_group(w, axis):
    shape = w.shape
    split = shape[:axis] + (SWA_KV_HEADS, SWA_GROUP, HEAD_DIM) + shape[axis + 1:]
    return jnp.swapaxes(w.reshape(split), axis, axis + 1).reshape(shape)


def _row_tile(n, want):
    t = min(n, want)
    assert n % t == 0
    return t


def kernel(x, norm_mix, w_in, attn_sinks, w_br_a, w_br_b, w_br_c, w_out, norm_ffn, w_ff_gate, w_ff_up, w_ff_down, w_router, w_moe_gate, w_moe_up, w_moe_down, norm_final):
    batch, seq, d = x.shape
    n = batch * seq
    depth = norm_mix.shape[0]
    tm = _row_tile(seq, 512)
    cos_t, sin_t = _rope_tables(seq)
    x2 = x.reshape(n, d).astype(F32)

    for layer in range(depth):
        w_l = w_in[layer]
        w_qa = _pair_heads_by_group(w_l[:, :WIDTH_A], 1)
        w_qkv = jnp.concatenate([w_qa, w_l[:, WIDTH_A:QKV_COLS]], axis=1).astype(BF16)
        w_gates = w_l[:, QKV_COLS:].astype(BF16)
        gain = norm_mix[layer].reshape(1, d)

        qa, ka, va, qb, kb, vb, qc, kc, vc = _inproj(x2, gain, w_qkv, cos_t, sin_t, seq, tm)
        oa = _swa(attn_sinks[layer], qa, ka, va, batch, seq, tm)
        ob = _moba(qb, kb, vb, batch, seq)
        oc = _dilated(qc, kc, vc, batch, seq)
        x2 = _merge(x2, gain, oa, ob, oc, w_gates,
                    _pair_heads_by_group(w_br_a[layer], 0).astype(BF16),
                    w_br_b[layer].astype(BF16), w_br_c[layer].astype(BF16),
                    w_out[layer].astype(BF16), tm)

        gain_f = norm_ffn[layer].reshape(1, d)
        idx = layer // 2
        if layer % 2 == 0:
            x2 = _ffn(x2, gain_f, w_ff_gate[idx].astype(BF16), w_ff_up[idx].astype(BF16),
                      w_ff_down[idx].astype(BF16), tm)
        else:
            w_r = jnp.pad(w_router[idx], ((0, 0), (0, LANES - N_EXPERTS)))
            hb, comb, rank, rank_t = _router(x2, gain_f, w_r, tm)
            y = _experts(hb, comb, rank, rank_t, w_moe_gate[idx].astype(BF16),
                         w_moe_up[idx].astype(BF16), w_moe_down[idx].astype(BF16), 2)
            if layer == depth - 1:
                out = _rowwise(_add_norm_kernel, "add_norm", tm, x2, y, norm_final.reshape(1, d))
                return out.reshape(batch, seq, d)
            x2 = _rowwise(_add_kernel, "add", tm, x2, y)

    return _rowwise(_norm_kernel, "final_norm", tm, x2, norm_final.reshape(1, d)).reshape(batch, seq, d)
```

```python
import functools

import jax
import jax.numpy as jnp
import numpy as np
from jax import lax
from jax.experimental import pallas as pl
from jax.experimental.pallas import tpu as pltpu

F32 = jnp.float32
BF16 = jnp.bfloat16

HEAD_DIM = 64
ATTN_SCALE = HEAD_DIM ** -0.5
LOG2_E = 1.4426950408889634
ROPE_THETA = 10000.0
RMS_EPS = 1e-5
NEG_INF = -1e30
BAND = 128
SWA_Q_HEADS, SWA_KV_HEADS = 6, 2
SWA_GROUP = SWA_Q_HEADS // SWA_KV_HEADS
SWA_MAX_DIST = 127
MOBA_HEADS, MOBA_BLOCK, MOBA_TOPK = 4, 256, 3
DIL_HEADS = 6
DIL_PATTERNS = ((128, 1), (512, 4), (2048, 16))
DIL_MAX_DIST = 128
N_EXPERTS, TOP_K = 8, 2

LANES = 128
VMEM_LIMIT = 56 * 1024 * 1024

WIDTH_A = SWA_Q_HEADS * HEAD_DIM
KV_A = SWA_KV_HEADS * HEAD_DIM
WIDTH_B = MOBA_HEADS * HEAD_DIM
WIDTH_C = DIL_HEADS * HEAD_DIM
QKV_COLS = WIDTH_A + 2 * KV_A + 3 * WIDTH_B + 3 * WIDTH_C


def _params(*sem):
    return pltpu.CompilerParams(dimension_semantics=sem, vmem_limit_bytes=VMEM_LIMIT)


def _rms(x, gain):
    ms = jnp.mean(x * x, axis=-1, keepdims=True)
    return x * lax.rsqrt(ms + RMS_EPS) * gain


def _dot(a, b):
    return jnp.dot(a, b, preferred_element_type=F32)


def _dot_t(a, b):
    return lax.dot_general(a, b, (((1,), (1,)), ((), ())), preferred_element_type=F32)


def _sigmoid(z):
    return 1.0 / (1.0 + jnp.exp(-z))


def _lane(shape):
    return lax.broadcasted_iota(jnp.int32, shape, len(shape) - 1)


def _inproj_kernel(x_ref, g_ref, w_ref, cos_ref, sin_ref,
                   qa_ref, ka_ref, va_ref, qb_ref, kb_ref, vb_ref, qc_ref, kc_ref, vc_ref):
    tm = x_ref.shape[0]
    h = _rms(x_ref[...], g_ref[...]).astype(BF16)
    cos = cos_ref[...]
    sin = sin_ref[...]
    first_half = (_lane((tm, LANES)) & (HEAD_DIM // 2)) == 0

    def rope(z):
        rot = jnp.where(first_half,
                        pltpu.roll(z, LANES - HEAD_DIM // 2, 1),
                        pltpu.roll(z, HEAD_DIM // 2, 1))
        return z * cos + rot * sin

    def proj(c0, width):
        return _dot(h, w_ref[:, c0:c0 + width])

    def store(ref, z, roped, scale, dtype):
        for t in range(z.shape[1] // LANES):
            zt = z[:, t * LANES:(t + 1) * LANES]
            if roped:
                zt = rope(zt)
            if scale is not None:
                zt = zt * scale
            ref[:, t * LANES:(t + 1) * LANES] = zt.astype(dtype)

    c = 0
    store(qa_ref, proj(c, WIDTH_A), True, ATTN_SCALE, BF16); c += WIDTH_A
    kv = proj(c, 2 * KV_A); c += 2 * KV_A
    store(ka_ref, kv[:, :KV_A], True, None, BF16)
    store(va_ref, kv[:, KV_A:], False, None, BF16)
    store(qb_ref, proj(c, WIDTH_B), True, ATTN_SCALE * LOG2_E, BF16); c += WIDTH_B
    store(kb_ref, proj(c, WIDTH_B), True, None, BF16); c += WIDTH_B
    store(vb_ref, proj(c, WIDTH_B), False, None, BF16); c += WIDTH_B
    store(qc_ref, proj(c, WIDTH_C), True, ATTN_SCALE, F32); c += WIDTH_C
    store(kc_ref, proj(c, WIDTH_C), True, None, F32); c += WIDTH_C
    store(vc_ref, proj(c, WIDTH_C), False, None, F32)


def _inproj(x2, gain, w_qkv, cos_t, sin_t, seq, tm):
    n, d = x2.shape
    tiles_per_seq = seq // tm
    row = lambda i: (i, 0)
    widths = (WIDTH_A, KV_A, KV_A, WIDTH_B, WIDTH_B, WIDTH_B, WIDTH_C, WIDTH_C, WIDTH_C)
    dtypes = (BF16,) * 6 + (F32,) * 3
    return pl.pallas_call(
        _inproj_kernel,
        grid=(n // tm,),
        in_specs=[
            pl.BlockSpec((tm, d), row),
            pl.BlockSpec((1, d), lambda i: (0, 0)),
            pl.BlockSpec((d, QKV_COLS), lambda i: (0, 0)),
            pl.BlockSpec((tm, LANES), lambda i: (i % tiles_per_seq, 0)),
            pl.BlockSpec((tm, LANES), lambda i: (i % tiles_per_seq, 0)),
        ],
        out_specs=[pl.BlockSpec((tm, w), row) for w in widths],
        out_shape=[jax.ShapeDtypeStruct((n, w), dt) for w, dt in zip(widths, dtypes)],
        compiler_params=_params("parallel"),
        name="inproj",
    )(x2, gain, w_qkv, cos_t, sin_t)


def _band_mask(max_dist, block_start, stack=1):
    qi = lax.broadcasted_iota(jnp.int32, (stack * BAND, 2 * BAND), 0) % BAND
    kj = lax.broadcasted_iota(jnp.int32, (stack * BAND, 2 * BAND), 1) - BAND
    diff = qi - kj
    return (diff >= 0) & (diff <= max_dist) & (kj + block_start >= 0)


def _split_heads(a):
    a32 = a.astype(F32)
    lo = _lane(a.shape) < HEAD_DIM
    return jnp.where(lo, a32, 0.0).astype(BF16), jnp.where(lo, 0.0, a32).astype(BF16)


def _pair_dots(q, k):
    return [_dot_t(q, kh) for kh in _split_heads(k)]


def _pair_softmax(scores, mask):
    out = []
    for s in scores:
        s = jnp.where(mask, s, NEG_INF)
        m = jnp.max(s, axis=-1, keepdims=True)
        p = jnp.exp(s - m)
        out.append((m, p, jnp.sum(p, axis=-1, keepdims=True)))
    return out


def _pair_pv(p0, p1, v):
    pc = jnp.concatenate([p0.astype(BF16), p1.astype(BF16)], axis=1)
    return _dot(pc, jnp.concatenate(_split_heads(v), axis=0))


def _by_head(lo_val, hi_val, shape):
    return jnp.where(_lane(shape) < HEAD_DIM, lo_val, hi_val)


def _swa_kernel(sink_ref, q_ref, k_ref, kp_ref, v_ref, vp_ref, o_ref):
    nblk = q_ref.shape[0] // BAND
    chunk_start = pl.program_id(1) * q_ref.shape[0]
    n_pairs = WIDTH_A // LANES
    def load(j):
        rows = slice(j * BAND, (j + 1) * BAND)
        prev = slice((j - 1) * BAND, j * BAND)
        q = q_ref[rows, :]
        qs = jnp.concatenate([q[:, t * LANES:(t + 1) * LANES] for t in range(n_pairs)], axis=0)
        if j == 0:
            k = jnp.concatenate([kp_ref[...], k_ref[rows, :]], axis=0)
            v = jnp.concatenate([vp_ref[...], v_ref[rows, :]], axis=0)
        else:
            k = jnp.concatenate([k_ref[prev, :], k_ref[rows, :]], axis=0)
            v = jnp.concatenate([v_ref[prev, :], v_ref[rows, :]], axis=0)
        return rows, v, _pair_dots(qs, k)

    blocks = [load(j) for j in range(nblk)]
    for j, (rows, v, scores) in enumerate(blocks):
        mask = _band_mask(SWA_MAX_DIST, chunk_start + j * BAND, n_pairs)
        stats = _pair_softmax(scores, mask)
        acc = _pair_pv(stats[0][1], stats[1][1], v)
        scales = []
        for g, (m, _, l) in enumerate(stats):
            sink = jnp.concatenate(
                [jnp.full((BAND, 1), sink_ref[SWA_GROUP * g + t], F32) for t in range(n_pairs)],
                axis=0)
            m_all = jnp.maximum(m, sink)
            keep = jnp.exp(m - m_all)
            den = l * keep + jnp.exp(sink - m_all)
            scales.append(keep / den)
        out = acc * _by_head(scales[0], scales[1], acc.shape)
        for t in range(n_pairs):
            o_ref[rows, t * LANES:(t + 1) * LANES] = out[t * BAND:(t + 1) * BAND, :].astype(o_ref.dtype)


def _swa(sinks, qa, ka, va, batch, seq, tq):
    n = qa.shape[0]
    chunks = seq // tq
    blk = tq // BAND
    cur = lambda b, c: (b * chunks + c, 0)
    prev = lambda b, c: (jnp.maximum((b * chunks + c) * blk - 1, 0), 0)
    return pl.pallas_call(
        _swa_kernel,
        grid=(batch, chunks),
        in_specs=[
            pl.BlockSpec(memory_space=pltpu.SMEM),
            pl.BlockSpec((tq, WIDTH_A), cur),
            pl.BlockSpec((tq, KV_A), cur),
            pl.BlockSpec((BAND, KV_A), prev),
            pl.BlockSpec((tq, KV_A), cur),
            pl.BlockSpec((BAND, KV_A), prev),
        ],
        out_specs=pl.BlockSpec((tq, WIDTH_A), cur),
        out_shape=jax.ShapeDtypeStruct((n, WIDTH_A), BF16),
        compiler_params=_params("parallel", "parallel"),
        name="swa",
    )(sinks, qa, ka, ka, va, va)


MOBA_UNROLL = 4
MOBA_VT_ROWS = HEAD_DIM + 16


def _moba_kernel(q_ref, qn_ref, k_ref, v_ref, o_ref,
                 kaug_ref, vt_ref, kmean_ref, acc_ref, m_ref, l_ref, s_ref, qa_ref):
    i = pl.program_id(2)
    nb = k_ref.shape[0] // MOBA_BLOCK
    blk_shape = (MOBA_BLOCK, LANES)
    lane = _lane(blk_shape)
    lo = lane < HEAD_DIM

    @pl.when(i == 0)
    def _prepare():
        kmean_ref[...] = jnp.zeros_like(kmean_ref)

        def body(n, carry):
            rows = pl.ds(pl.multiple_of(n * MOBA_BLOCK, MOBA_BLOCK), MOBA_BLOCK)
            kb = k_ref[rows, :].astype(F32)
            kmean_ref[pl.ds(n, 1), :] = jnp.sum(kb, axis=0, keepdims=True) * (1.0 / MOBA_BLOCK)
            block_hot = jnp.where((lane == n) | (lane == HEAD_DIM + n), 1.0, 0.0)
            kaug_ref[0, rows, :] = jnp.where(lo, kb, block_hot).astype(BF16)
            kaug_ref[1, rows, :] = jnp.where(lo, block_hot, kb).astype(BF16)
            vt = v_ref[rows, :].astype(F32).T
            ones = jnp.ones((MOBA_VT_ROWS - HEAD_DIM, MOBA_BLOCK), F32)
            vt_ref[n] = jnp.concatenate(
                [vt[:HEAD_DIM], ones, vt[HEAD_DIM:], ones], axis=0).astype(BF16)
            return carry

        lax.fori_loop(0, nb, body, 0)

    def select(q, own):
        kmean = kmean_ref[...]
        lo_m = _lane(kmean.shape) < HEAD_DIM
        sel_shape = (HEAD_DIM, MOBA_BLOCK)
        blk = lax.broadcasted_iota(jnp.int32, sel_shape, 0)
        blk_f = blk.astype(F32)
        pens = []
        for km in (jnp.where(lo_m, kmean, 0.0), jnp.where(lo_m, 0.0, kmean)):
            gate = _dot_t(km.astype(BF16), q)
            gate = jnp.where(blk < own, gate, -jnp.inf)
            sel = blk == own
            for _ in range(MOBA_TOPK):
                mx = jnp.max(gate, axis=0, keepdims=True)
                first = jnp.min(jnp.where(gate == mx, blk_f, float(HEAD_DIM)), axis=0, keepdims=True)
                pick = (blk_f == first) & (mx > -jnp.inf)
                sel = sel | pick
                gate = jnp.where(pick, -jnp.inf, gate)
            pens.append(jnp.where(sel, 0.0, NEG_INF))
        zeros = jnp.zeros(sel_shape, F32)
        q32 = q.astype(F32)
        return (jnp.where(lo, q32, jnp.concatenate([zeros, pens[0]], axis=0).T).astype(BF16),
                jnp.where(lo, jnp.concatenate([pens[1], zeros], axis=0).T, q32).astype(BF16))

    @pl.when(i == 0)
    def _first_block():
        qa_ref[0, 0], qa_ref[0, 1] = select(q_ref[...], 0)

    ahead0, ahead1 = select(qn_ref[...], i + 1)
    qa0 = qa_ref[i % 2, 0]
    qa1 = qa_ref[i % 2, 1]
    qa_ref[(i + 1) % 2, 0] = ahead0
    qa_ref[(i + 1) % 2, 1] = ahead1

    acc_ref[...] = jnp.zeros_like(acc_ref)
    m_ref[...] = jnp.full_like(m_ref, NEG_INF)
    l_ref[...] = jnp.zeros_like(l_ref)

    n_trips = i // MOBA_UNROLL + 1

    def scores(t, n):
        j = jnp.minimum(t * MOBA_UNROLL + n, i)
        rows = pl.ds(pl.multiple_of(j * MOBA_BLOCK, MOBA_BLOCK), MOBA_BLOCK)
        return [_dot_t(kaug_ref[hd, rows, :], qa) for hd, qa in enumerate((qa0, qa1))]

    def update(j, blk_scores, masked):
        vt = vt_ref[jnp.minimum(j, i)]
        ps, alphas = [], []
        for hd, s in enumerate(blk_scores):
            if masked:
                kk = lax.broadcasted_iota(jnp.int32, s.shape, 0)
                qq = lax.broadcasted_iota(jnp.int32, s.shape, 1)
                s = jnp.where(kk <= qq, s, NEG_INF)
            m_old = m_ref[hd]
            m_new = jnp.maximum(m_old, jnp.max(s, axis=0, keepdims=True))
            alphas.append(jnp.exp2(m_old - m_new))
            ps.append(jnp.exp2(s - m_new).astype(BF16))
            m_ref[hd] = m_new
        for hd in range(2):
            pv = _dot(vt[hd * MOBA_VT_ROWS:(hd + 1) * MOBA_VT_ROWS, :], ps[hd])
            acc_ref[hd] = acc_ref[hd] * alphas[hd] + pv[:HEAD_DIM, :]
            l_ref[hd] = l_ref[hd] * alphas[hd] + pv[HEAD_DIM:HEAD_DIM + 1, :]

    for n in range(MOBA_UNROLL):
        for hd, s in enumerate(scores(0, n)):
            s_ref[n, hd] = s

    def body(t, carry):
        for n in range(MOBA_UNROLL):
            cur = [s_ref[n, hd] for hd in range(2)]
            ahead = scores(t + 1, n)
            update(t * MOBA_UNROLL + n, cur, False)
            for hd in range(2):
                s_ref[n, hd] = ahead[hd]
        return carry

    lax.fori_loop(0, n_trips - 1, body, 0)
    for last in range(MOBA_UNROLL):
        @pl.when(i % MOBA_UNROLL == last)
        def _(last=last):
            for n in range(last + 1):
                update(i - last + n, [s_ref[n, hd] for hd in range(2)], n == last)
    out_t = jnp.concatenate([acc_ref[hd] * (1.0 / l_ref[hd]) for hd in range(2)], axis=0)
    o_ref[...] = out_t.T.astype(o_ref.dtype)


def _moba(qb, kb, vb, batch, seq):
    n = qb.shape[0]
    nb = seq // MOBA_BLOCK
    pairs = WIDTH_B // LANES
    assert nb <= HEAD_DIM, "block one-hot must fit in the other head's lanes"
    qmap = lambda b, p, i: (b * nb + i, p)
    qnext = lambda b, p, i: (b * nb + jnp.minimum(i + 1, nb - 1), p)
    kvmap = lambda b, p, i: (b, p)
    return pl.pallas_call(
        _moba_kernel,
        grid=(batch, pairs, nb),
        in_specs=[
            pl.BlockSpec((MOBA_BLOCK, LANES), qmap),
            pl.BlockSpec((MOBA_BLOCK, LANES), qnext),
            pl.BlockSpec((seq, LANES), kvmap),
            pl.BlockSpec((seq, LANES), kvmap),
        ],
        out_specs=pl.BlockSpec((MOBA_BLOCK, LANES), qmap),
        out_shape=jax.ShapeDtypeStruct((n, WIDTH_B), BF16),
        scratch_shapes=[
            pltpu.VMEM((2, seq, LANES), BF16),
            pltpu.VMEM((nb, 2 * MOBA_VT_ROWS, MOBA_BLOCK), BF16),
            pltpu.VMEM((HEAD_DIM, LANES), F32),
            pltpu.VMEM((2, HEAD_DIM, MOBA_BLOCK), F32),
            pltpu.VMEM((2, 1, MOBA_BLOCK), F32),
            pltpu.VMEM((2, 1, MOBA_BLOCK), F32),
            pltpu.VMEM((MOBA_UNROLL, 2, MOBA_BLOCK, MOBA_BLOCK), F32),
            pltpu.VMEM((2, 2, MOBA_BLOCK, LANES), BF16),
        ],
        compiler_params=_params("parallel", "parallel", "arbitrary"),
        name="moba",
    )(qb, qb, kb, vb)


DIL_CHUNK = DIL_PATTERNS[-1][1] * BAND
DIL_UNROLL = 4


def _dil_kernel(q_ref, kp_ref, k_ref, vp_ref, v_ref, o_ref, kk_ref, vv_ref, acc_ref, m_ref, l_ref):
    c = pl.program_id(2)
    kk_ref[0:DIL_CHUNK, :] = kp_ref[...]
    kk_ref[DIL_CHUNK:, :] = k_ref[...]
    vv_ref[0:DIL_CHUNK, :] = vp_ref[...]
    vv_ref[DIL_CHUNK:, :] = v_ref[...]
    blocks = DIL_CHUNK // BAND
    shape = (BAND, LANES)

    for pi, (_, d) in enumerate(DIL_PATTERNS):
        per_class = blocks // d

        def load(t, d=d, per_class=per_class):
            r = t // per_class
            j = t % per_class
            q0 = r + BAND * d * j
            k0 = DIL_CHUNK + q0 - BAND * d
            if d == 1:
                qrows = pl.ds(pl.multiple_of(q0, BAND), BAND)
                krows = pl.ds(pl.multiple_of(k0, BAND), 2 * BAND)
            else:
                qrows = pl.ds(q0, BAND, stride=d)
                krows = pl.ds(k0, 2 * BAND, stride=d)
            scores = _pair_dots(q_ref[qrows, :].astype(BF16), kk_ref[krows, :])
            return qrows, krows, (c * per_class + j) * BAND, scores

        def body(t, carry, pi=pi, load=load):
            group = [load(t * DIL_UNROLL + u) for u in range(DIL_UNROLL)]
            for qrows, krows, block_start, scores in group:
                mask = _band_mask(DIL_MAX_DIST, block_start)
                (m0, p0, l0), (m1, p1, l1) = _pair_softmax(scores, mask)
                acc_ref[pi, qrows, :] = _pair_pv(p0, p1, vv_ref[krows, :])
                m_ref[pi, qrows, :] = _by_head(m0, m1, shape)
                l_ref[pi, qrows, :] = _by_head(l0, l1, shape)
            return carry

        lax.fori_loop(0, blocks // DIL_UNROLL, body, 0)

    def combine(t, carry):
        rows = pl.ds(pl.multiple_of(t * BAND, BAND), BAND)
        ms = [m_ref[pi, rows, :] for pi in range(len(DIL_PATTERNS))]
        m_max = functools.reduce(jnp.maximum, ms)
        num = jnp.zeros(shape, F32)
        den = jnp.zeros(shape, F32)
        for pi, m in enumerate(ms):
            e = jnp.exp(m - m_max)
            num = num + acc_ref[pi, rows, :] * e
            den = den + l_ref[pi, rows, :] * e
        o_ref[rows, :] = (num / den).astype(o_ref.dtype)
        return carry

    lax.fori_loop(0, blocks, combine, 0)


def _dilated(qc, kc, vc, batch, seq):
    n = qc.shape[0]
    assert seq % DIL_CHUNK == 0
    chunks = seq // DIL_CHUNK
    pairs = WIDTH_C // LANES
    cur = lambda b, p, c: (b * chunks + c, p)
    prev = lambda b, p, c: (b * chunks + jnp.maximum(c - 1, 0), p)
    blk = (DIL_CHUNK, LANES)
    n_pat = len(DIL_PATTERNS)
    return pl.pallas_call(
        _dil_kernel,
        grid=(batch, pairs, chunks),
        in_specs=[pl.BlockSpec(blk, cur), pl.BlockSpec(blk, prev), pl.BlockSpec(blk, cur),
                  pl.BlockSpec(blk, prev), pl.BlockSpec(blk, cur)],
        out_specs=pl.BlockSpec(blk, cur),
        out_shape=jax.ShapeDtypeStruct((n, WIDTH_C), BF16),
        scratch_shapes=[
            pltpu.VMEM((2 * DIL_CHUNK, LANES), F32),
            pltpu.VMEM((2 * DIL_CHUNK, LANES), F32),
            pltpu.VMEM((n_pat, DIL_CHUNK, LANES), F32),
            pltpu.VMEM((n_pat, DIL_CHUNK, LANES), F32),
            pltpu.VMEM((n_pat, DIL_CHUNK, LANES), F32),
        ],
        compiler_params=_params("parallel", "parallel", "parallel"),
        name="dilated",
    )(qc, kc, kc, vc, vc)


def _merge_kernel(x_ref, g_ref, oa_ref, ob_ref, oc_ref, wg_ref, wa_ref, wb_ref, wc_ref, wo_ref, y_ref):
    x = x_ref[...]
    d = x.shape[1]
    h = _rms(x, g_ref[...]).astype(BF16)
    merged = None
    for br, (o_ref, w_ref) in enumerate(((oa_ref, wa_ref), (ob_ref, wb_ref), (oc_ref, wc_ref))):
        gate = _sigmoid(_dot(h, wg_ref[:, br * d:(br + 1) * d]))
        y = gate * _dot(o_ref[...], w_ref[...])
        merged = y if merged is None else merged + y
    y_ref[...] = x + _dot(merged.astype(BF16), wo_ref[...])


def _merge(x2, gain, oa, ob, oc, w_gates, w_a, w_b, w_c, w_o, tm):
    n, d = x2.shape
    row = lambda i: (i, 0)
    full = lambda i: (0, 0)
    return pl.pallas_call(
        _merge_kernel,
        grid=(n // tm,),
        in_specs=[
            pl.BlockSpec((tm, d), row),
            pl.BlockSpec((1, d), full),
            pl.BlockSpec((tm, WIDTH_A), row),
            pl.BlockSpec((tm, WIDTH_B), row),
            pl.BlockSpec((tm, WIDTH_C), row),
            pl.BlockSpec(w_gates.shape, full),
            pl.BlockSpec(w_a.shape, full),
            pl.BlockSpec(w_b.shape, full),
            pl.BlockSpec(w_c.shape, full),
            pl.BlockSpec(w_o.shape, full),
        ],
        out_specs=pl.BlockSpec((tm, d), row),
        out_shape=jax.ShapeDtypeStruct((n, d), F32),
        compiler_params=_params("parallel"),
        name="merge",
    )(x2, gain, oa, ob, oc, w_gates, w_a, w_b, w_c, w_o)


def _swiglu(h, wg, wu, wd):
    a = _dot(h, wg)
    u = _dot(h, wu)
    return _dot((a * _sigmoid(a) * u).astype(BF16), wd)


def _ffn_kernel(x_ref, g_ref, wg_ref, wu_ref, wd_ref, y_ref):
    x = x_ref[...]
    h = _rms(x, g_ref[...]).astype(BF16)
    y_ref[...] = x + _swiglu(h, wg_ref[...], wu_ref[...], wd_ref[...])


def _ffn(x2, gain, wg, wu, wd, tm):
    n, d = x2.shape
    row = lambda i: (i, 0)
    full = lambda i: (0, 0)
    return pl.pallas_call(
        _ffn_kernel,
        grid=(n // tm,),
        in_specs=[
            pl.BlockSpec((tm, d), row),
            pl.BlockSpec((1, d), full),
            pl.BlockSpec(wg.shape, full),
            pl.BlockSpec(wu.shape, full),
            pl.BlockSpec(wd.shape, full),
        ],
        out_specs=pl.BlockSpec((tm, d), row),
        out_shape=jax.ShapeDtypeStruct((n, d), F32),
        compiler_params=_params("parallel"),
        name="ffn",
    )(x2, gain, wg, wu, wd)


MOE_GROUP = 2048
MOE_PACK_BLOCK = 256
MOE_CHUNK = 128

def _split_bf16(a):
    hi = a.astype(BF16)
    return hi, (a - hi.astype(F32)).astype(BF16)


def _router_kernel(x_ref, g_ref, wr_ref, h_ref, comb_ref, rank_ref, rankt_ref, count_ref):
    tm = x_ref.shape[0]
    tile_in_group = pl.program_id(0) % (MOE_GROUP // tm)
    h = _rms(x_ref[...], g_ref[...])
    h_ref[...] = h.astype(BF16)
    h_hi, h_lo = _split_bf16(h)
    w_hi, w_lo = _split_bf16(wr_ref[...])
    logits = _dot(h_hi, w_hi) + (_dot(h_hi, w_lo) + _dot(h_lo, w_hi))
    lane = _lane(logits.shape)
    logits = jnp.where(lane < N_EXPERTS, logits, -jnp.inf)
    tops = []
    for _ in range(TOP_K):
        mx = jnp.max(logits, axis=-1, keepdims=True)
        first = jnp.min(jnp.where(logits == mx, lane, LANES), axis=-1, keepdims=True)
        tops.append((mx, first))
        logits = jnp.where(lane == first, -jnp.inf, logits)
    (v1, i1), (v2, i2) = tops
    e2 = jnp.exp(v2 - v1)
    w1 = 1.0 / (1.0 + e2)
    w2 = e2 / (1.0 + e2)
    sel = jnp.where(lane == i1, 1.0, 0.0) + jnp.where(lane == i2, 1.0, 0.0)
    comb_ref[...] = jnp.where(lane == i1, w1, 0.0) + jnp.where(lane == i2, w2, 0.0)

    @pl.when(tile_in_group == 0)
    def _():
        count_ref[...] = jnp.zeros_like(count_ref)

    r = lax.broadcasted_iota(jnp.int32, (tm, tm), 0)
    c = lax.broadcasted_iota(jnp.int32, (tm, tm), 1)
    before = _dot(jnp.where(c < r, 1.0, 0.0).astype(BF16), sel.astype(BF16)) + count_ref[...]
    rank = jnp.where(sel > 0.0, before, -1.0)
    count_ref[...] = count_ref[...] + jnp.sum(sel, axis=0, keepdims=True)
    rank_ref[...] = rank
    rankt_ref[0] = rank.T[:N_EXPERTS, :]


def _router(x2, gain, w_router_pad, tm):
    n, d = x2.shape
    assert n % MOE_GROUP == 0 and MOE_GROUP % tm == 0
    per_group = MOE_GROUP // tm
    row = lambda i: (i, 0)
    return pl.pallas_call(
        _router_kernel,
        grid=(n // tm,),
        in_specs=[
            pl.BlockSpec((tm, d), row),
            pl.BlockSpec((1, d), lambda i: (0, 0)),
            pl.BlockSpec((d, LANES), lambda i: (0, 0)),
        ],
        out_specs=[pl.BlockSpec((tm, d), row),
                   pl.BlockSpec((tm, LANES), row),
                   pl.BlockSpec((tm, LANES), row),
                   pl.BlockSpec((1, N_EXPERTS, tm), lambda i: (i // per_group, 0, i % per_group))],
        out_shape=[jax.ShapeDtypeStruct((n, d), BF16),
                   jax.ShapeDtypeStruct((n, LANES), F32),
                   jax.ShapeDtypeStruct((n, LANES), F32),
                   jax.ShapeDtypeStruct((n // MOE_GROUP, N_EXPERTS, MOE_GROUP), F32)],
        scratch_shapes=[pltpu.VMEM((1, LANES), F32)],
        compiler_params=_params("arbitrary"),
        name="router",
    )(x2, gain, w_router_pad)


def _experts_kernel(h_ref, comb_ref, rank_ref, rankt_ref, wg_ref, wu_ref, wd_ref, y_ref,
                    xs_ref, acc_ref, col_ref, start_ref):
    e = pl.program_id(1)
    f = pl.program_id(2)
    group, d = h_ref.shape
    n_blocks = group // MOE_CHUNK
    per_pack = MOE_PACK_BLOCK // MOE_CHUNK
    blocks = [slice(b * MOE_CHUNK, (b + 1) * MOE_CHUNK) for b in range(n_blocks)]
    pack_blocks = [slice(b * MOE_PACK_BLOCK, (b + 1) * MOE_PACK_BLOCK) for b in range(group // MOE_PACK_BLOCK)]

    def chunk_rows(m, n=1):
        return pl.ds(pl.multiple_of(m * MOE_CHUNK, MOE_CHUNK), n * MOE_CHUNK)

    def n_chunks():
        return (start_ref[n_blocks] + MOE_CHUNK - 1) // MOE_CHUNK

    def overlaps(m, b):
        return ((start_ref[b * per_pack] < (m + 1) * MOE_CHUNK)
                & (start_ref[(b + 1) * per_pack] > m * MOE_CHUNK))

    @pl.when((e == 0) & (f == 0))
    def _():
        y_ref[...] = jnp.zeros_like(y_ref)

    @pl.when(f == 0)
    def _dispatch():
        pick = _lane((group, LANES)) == e
        rank_col = jnp.sum(jnp.where(pick, rank_ref[...], 0.0), axis=1, keepdims=True)
        col_ref[0] = rank_col
        col_ref[1] = jnp.sum(jnp.where(pick, comb_ref[...], 0.0), axis=1, keepdims=True)
        rank_row = rankt_ref[0, pl.ds(e, 1), :]
        start_ref[0] = 0
        for b in range(n_blocks):
            last = jnp.max(rank_row[:, blocks[b]]).astype(jnp.int32) + 1
            start_ref[b + 1] = jnp.maximum(start_ref[b], last)

        def pack(m, carry):
            dest = (lax.broadcasted_iota(jnp.int32, (MOE_CHUNK, MOE_PACK_BLOCK), 0)
                    + m * MOE_CHUNK).astype(F32)
            xs_ref[chunk_rows(m), :] = jnp.zeros((MOE_CHUNK, d), BF16)
            acc_ref[chunk_rows(m), :] = jnp.zeros((MOE_CHUNK, d), F32)
            for b, tokens in enumerate(pack_blocks):
                @pl.when(overlaps(m, b))
                def _():
                    onehot = jnp.where(rank_row[:, tokens] == dest, 1.0, 0.0).astype(BF16)
                    rows = _dot(onehot, h_ref[tokens, :])
                    xs_ref[chunk_rows(m), :] = (xs_ref[chunk_rows(m), :].astype(F32) + rows).astype(BF16)
            return carry

        lax.fori_loop(0, n_chunks(), pack, 0)
        acc_ref[chunk_rows(n_chunks(), 2), :] = jnp.zeros((2 * MOE_CHUNK, d), F32)

    def expert(m, carry):
        xm = xs_ref[chunk_rows(m), :]
        acc_ref[chunk_rows(m), :] += _swiglu(xm, wg_ref[0], wu_ref[0], wd_ref[0])
        return carry

    lax.fori_loop(0, n_chunks(), expert, 0)

    @pl.when(f == pl.num_programs(2) - 1)
    def _combine():
        for b, tokens in enumerate(blocks):
            first = start_ref[b] // MOE_CHUNK
            y_hi, y_lo = _split_bf16(acc_ref[chunk_rows(first, 2), :])
            dest = (_lane((MOE_CHUNK, 2 * MOE_CHUNK)) + first * MOE_CHUNK).astype(F32)
            onehot = jnp.where(col_ref[0, tokens, :] == dest, 1.0, 0.0).astype(BF16)
            y_ref[tokens, :] += col_ref[1, tokens, :] * (_dot(onehot, y_hi) + _dot(onehot, y_lo))


def _experts(hb, comb, rank, rank_t, wg, wu, wd, ff_split):
    n, d = hb.shape
    n_e, _, d_ff = wg.shape
    ff = d_ff // ff_split
    group = lambda s, e, f: (s, 0)
    once = pl.Buffered(1)
    return pl.pallas_call(
        _experts_kernel,
        grid=(n // MOE_GROUP, n_e, ff_split),
        in_specs=[
            pl.BlockSpec((MOE_GROUP, d), group, pipeline_mode=once),
            pl.BlockSpec((MOE_GROUP, LANES), group, pipeline_mode=once),
            pl.BlockSpec((MOE_GROUP, LANES), group, pipeline_mode=once),
            pl.BlockSpec((1, N_EXPERTS, MOE_GROUP), lambda s, e, f: (s, 0, 0)),
            pl.BlockSpec((1, d, ff), lambda s, e, f: (e, 0, f)),
            pl.BlockSpec((1, d, ff), lambda s, e, f: (e, 0, f)),
            pl.BlockSpec((1, ff, d), lambda s, e, f: (e, f, 0)),
        ],
        out_specs=pl.BlockSpec((MOE_GROUP, d), group, pipeline_mode=once),
        out_shape=jax.ShapeDtypeStruct((n, d), F32),
        scratch_shapes=[
            pltpu.VMEM((MOE_GROUP, d), BF16),
            pltpu.VMEM((MOE_GROUP + 2 * MOE_CHUNK, d), F32),
            pltpu.VMEM((2, MOE_GROUP, 1), F32),
            pltpu.SMEM((MOE_GROUP // MOE_CHUNK + 1,), jnp.int32),
        ],
        compiler_params=_params("parallel", "arbitrary", "arbitrary"),
        name="experts",
    )(hb, comb, rank, rank_t, wg, wu, wd)


def _add_kernel(x_ref, y_ref, o_ref):
    o_ref[...] = x_ref[...] + y_ref[...]


def _add_norm_kernel(x_ref, y_ref, g_ref, o_ref):
    o_ref[...] = _rms(x_ref[...] + y_ref[...], g_ref[...])


def _norm_kernel(x_ref, g_ref, o_ref):
    o_ref[...] = _rms(x_ref[...], g_ref[...])


def _rowwise(body, name, tm, x2, *rest):
    n, d = x2.shape
    row = pl.BlockSpec((tm, d), lambda i: (i, 0))
    specs = [row if a.shape == (n, d) else pl.BlockSpec((1, d), lambda i: (0, 0)) for a in (x2,) + rest]
    return pl.pallas_call(
        body,
        grid=(n // tm,),
        in_specs=specs,
        out_specs=row,
        out_shape=jax.ShapeDtypeStruct((n, d), F32),
        compiler_params=_params("parallel"),
        name=name,
    )(x2, *rest)


def _rope_tables(seq):
    pos = jnp.arange(seq, dtype=F32)
    inv_freq = ROPE_THETA ** (-jnp.arange(0, HEAD_DIM, 2, dtype=F32) / HEAD_DIM)
    ang = pos[:, None] * inv_freq[None, :]
    cos, sin = jnp.cos(ang), jnp.sin(ang)
    reps = LANES // HEAD_DIM
    cos_t = jnp.tile(jnp.concatenate([cos, cos], axis=-1), (1, reps))
    sin_t = jnp.tile(jnp.concatenate([-sin, sin], axis=-1), (1, reps))
    return cos_t, sin_t


def _pair_heads_by_group(w, axis):
    shape = w.shape
    split = shape[:axis] + (SWA_KV_HEADS, SWA_GROUP, HEAD_DIM) + shape[axis + 1:]
    return jnp.swapaxes(w.reshape(split), axis, axis + 1).reshape(shape)


def _row_tile(n, want):
    t = min(n, want)
    assert n % t == 0
    return t


def kernel(x, norm_mix, w_in, attn_sinks, w_br_a, w_br_b, w_br_c, w_out, norm_ffn, w_ff_gate, w_ff_up, w_ff_down, w_router, w_moe_gate, w_moe_up, w_moe_down, norm_final):
    batch, seq, d = x.shape
    n = batch * seq
    depth = norm_mix.shape[0]
    tm = _row_tile(seq, 512)
    cos_t, sin_t = _rope_tables(seq)
    x2 = x.reshape(n, d).astype(F32)

    for layer in range(depth):
        w_l = w_in[layer]
        w_qa = _pair_heads_by_group(w_l[:, :WIDTH_A], 1)
        w_qkv = jnp.concatenate([w_qa, w_l[:, WIDTH_A:QKV_COLS]], axis=1).astype(BF16)
        w_gates = w_l[:, QKV_COLS:].astype(BF16)
        gain = norm_mix[layer].reshape(1, d)

        qa, ka, va, qb, kb, vb, qc, kc, vc = _inproj(x2, gain, w_qkv, cos_t, sin_t, seq, tm)
        oa = _swa(attn_sinks[layer], qa, ka, va, batch, seq, tm)
        ob = _moba(qb, kb, vb, batch, seq)
        oc = _dilated(qc, kc, vc, batch, seq)
        x2 = _merge(x2, gain, oa, ob, oc, w_gates,
                    _pair_heads_by_group(w_br_a[layer], 0).astype(BF16),
                    w_br_b[layer].astype(BF16), w_br_c[layer].astype(BF16),
                    w_out[layer].astype(BF16), tm)

        gain_f = norm_ffn[layer].reshape(1, d)
        idx = layer // 2
        if layer % 2 == 0:
            x2 = _ffn(x2, gain_f, w_ff_gate[idx].astype(BF16), w_ff_up[idx].astype(BF16),
                      w_ff_down[idx].astype(BF16), tm)
        else:
            w_r = jnp.pad(w_router[idx], ((0, 0), (0, LANES - N_EXPERTS)))
            hb, comb, rank, rank_t = _router(x2, gain_f, w_r, tm)
            y = _experts(hb, comb, rank, rank_t, w_moe_gate[idx].astype(BF16),
                         w_moe_up[idx].astype(BF16), w_moe_down[idx].astype(BF16), 2)
            if layer == depth - 1:
                out = _rowwise(_add_norm_kernel, "add_norm", tm, x2, y, norm_final.reshape(1, d))
                return out.reshape(batch, seq, d)
            x2 = _rowwise(_add_kernel, "add", tm, x2, y)

    return _rowwise(_norm_kernel, "final_norm", tm, x2, norm_final.reshape(1, d)).reshape(batch, seq, d)
```

```python
import functools

import jax
import jax.numpy as jnp
import numpy as np
from jax import lax
from jax.experimental import pallas as pl
from jax.experimental.pallas import tpu as pltpu

F32 = jnp.float32
BF16 = jnp.bfloat16

HEAD_DIM = 64
ATTN_SCALE = HEAD_DIM ** -0.5
LOG2_E = 1.4426950408889634
ROPE_THETA = 10000.0
RMS_EPS = 1e-5
NEG_INF = -1e30
BAND = 128
SWA_Q_HEADS, SWA_KV_HEADS = 6, 2
SWA_GROUP = SWA_Q_HEADS // SWA_KV_HEADS
SWA_MAX_DIST = 127
MOBA_HEADS, MOBA_BLOCK, MOBA_TOPK = 4, 256, 3
DIL_HEADS = 6
DIL_PATTERNS = ((128, 1), (512, 4), (2048, 16))
DIL_MAX_DIST = 128
N_EXPERTS, TOP_K = 8, 2

LANES = 128
VMEM_LIMIT = 56 * 1024 * 1024

WIDTH_A = SWA_Q_HEADS * HEAD_DIM
KV_A = SWA_KV_HEADS * HEAD_DIM
WIDTH_B = MOBA_HEADS * HEAD_DIM
WIDTH_C = DIL_HEADS * HEAD_DIM
QKV_COLS = WIDTH_A + 2 * KV_A + 3 * WIDTH_B + 3 * WIDTH_C


def _params(*sem):
    return pltpu.CompilerParams(dimension_semantics=sem, vmem_limit_bytes=VMEM_LIMIT)


def _rms(x, gain):
    ms = jnp.mean(x * x, axis=-1, keepdims=True)
    return x * lax.rsqrt(ms + RMS_EPS) * gain


def _dot(a, b):
    return jnp.dot(a, b, preferred_element_type=F32)


def _dot_t(a, b):
    return lax.dot_general(a, b, (((1,), (1,)), ((), ())), preferred_element_type=F32)


def _sigmoid(z):
    return 1.0 / (1.0 + jnp.exp(-z))


def _lane(shape):
    return lax.broadcasted_iota(jnp.int32, shape, len(shape) - 1)


def _inproj_kernel(x_ref, g_ref, w_ref, cos_ref, sin_ref,
                   qa_ref, ka_ref, va_ref, qb_ref, kb_ref, vb_ref, qc_ref, kc_ref, vc_ref):
    tm = x_ref.shape[0]
    h = _rms(x_ref[...], g_ref[...]).astype(BF16)
    cos = cos_ref[...]
    sin = sin_ref[...]
    first_half = (_lane((tm, LANES)) & (HEAD_DIM // 2)) == 0

    def rope(z):
        rot = jnp.where(first_half,
                        pltpu.roll(z, LANES - HEAD_DIM // 2, 1),
                        pltpu.roll(z, HEAD_DIM // 2, 1))
        return z * cos + rot * sin

    z_all = _dot(h, w_ref[...])

    def proj(c0, width):
        return z_all[:, c0:c0 + width]

    def store(ref, z, roped, scale, dtype):
        for t in range(z.shape[1] // LANES):
            zt = z[:, t * LANES:(t + 1) * LANES]
            if roped:
                zt = rope(zt)
            if scale is not None:
                zt = zt * scale
            ref[:, t * LANES:(t + 1) * LANES] = zt.astype(dtype)

    c = 0
    store(qa_ref, proj(c, WIDTH_A), True, ATTN_SCALE, BF16); c += WIDTH_A
    kv = proj(c, 2 * KV_A); c += 2 * KV_A
    store(ka_ref, kv[:, :KV_A], True, None, BF16)
    store(va_ref, kv[:, KV_A:], False, None, BF16)
    store(qb_ref, proj(c, WIDTH_B), True, ATTN_SCALE * LOG2_E, BF16); c += WIDTH_B
    store(kb_ref, proj(c, WIDTH_B), True, None, BF16); c += WIDTH_B
    store(vb_ref, proj(c, WIDTH_B), False, None, BF16); c += WIDTH_B
    store(qc_ref, proj(c, WIDTH_C), True, ATTN_SCALE, F32); c += WIDTH_C
    store(kc_ref, proj(c, WIDTH_C), True, None, F32); c += WIDTH_C
    store(vc_ref, proj(c, WIDTH_C), False, None, F32)


def _inproj(x2, gain, w_qkv, cos_t, sin_t, seq, tm):
    n, d = x2.shape
    tiles_per_seq = seq // tm
    row = lambda i: (i, 0)
    widths = (WIDTH_A, KV_A, KV_A, WIDTH_B, WIDTH_B, WIDTH_B, WIDTH_C, WIDTH_C, WIDTH_C)
    dtypes = (BF16,) * 6 + (F32,) * 3
    return pl.pallas_call(
        _inproj_kernel,
        grid=(n // tm,),
        in_specs=[
            pl.BlockSpec((tm, d), row),
            pl.BlockSpec((1, d), lambda i: (0, 0)),
            pl.BlockSpec((d, QKV_COLS), lambda i: (0, 0)),
            pl.BlockSpec((tm, LANES), lambda i: (i % tiles_per_seq, 0)),
            pl.BlockSpec((tm, LANES), lambda i: (i % tiles_per_seq, 0)),
        ],
        out_specs=[pl.BlockSpec((tm, w), row) for w in widths],
        out_shape=[jax.ShapeDtypeStruct((n, w), dt) for w, dt in zip(widths, dtypes)],
        compiler_params=_params("parallel"),
        name="inproj",
    )(x2, gain, w_qkv, cos_t, sin_t)


def _band_mask(max_dist, block_start, stack=1):
    qi = lax.broadcasted_iota(jnp.int32, (stack * BAND, 2 * BAND), 0) % BAND
    kj = lax.broadcasted_iota(jnp.int32, (stack * BAND, 2 * BAND), 1) - BAND
    diff = qi - kj
    return (diff >= 0) & (diff <= max_dist) & (kj + block_start >= 0)


def _split_heads(a):
    a32 = a.astype(F32)
    lo = _lane(a.shape) < HEAD_DIM
    return jnp.where(lo, a32, 0.0).astype(BF16), jnp.where(lo, 0.0, a32).astype(BF16)


def _pair_dots(q, k):
    return [_dot_t(q, kh) for kh in _split_heads(k)]


def _pair_softmax(scores, mask):
    out = []
    for s in scores:
        s = jnp.where(mask, s, NEG_INF)
        m = jnp.max(s, axis=-1, keepdims=True)
        p = jnp.exp(s - m)
        out.append((m, p, jnp.sum(p, axis=-1, keepdims=True)))
    return out


def _pair_pv(p0, p1, v):
    pc = jnp.concatenate([p0.astype(BF16), p1.astype(BF16)], axis=1)
    return _dot(pc, jnp.concatenate(_split_heads(v), axis=0))


def _by_head(lo_val, hi_val, shape):
    return jnp.where(_lane(shape) < HEAD_DIM, lo_val, hi_val)


def _swa_kernel(sink_ref, q_ref, k_ref, kp_ref, v_ref, vp_ref, o_ref):
    nblk = q_ref.shape[0] // BAND
    chunk_start = pl.program_id(1) * q_ref.shape[0]
    n_pairs = WIDTH_A // LANES
    def load(j):
        rows = slice(j * BAND, (j + 1) * BAND)
        prev = slice((j - 1) * BAND, j * BAND)
        q = q_ref[rows, :]
        qs = jnp.concatenate([q[:, t * LANES:(t + 1) * LANES] for t in range(n_pairs)], axis=0)
        if j == 0:
            k = jnp.concatenate([kp_ref[...], k_ref[rows, :]], axis=0)
            v = jnp.concatenate([vp_ref[...], v_ref[rows, :]], axis=0)
        else:
            k = jnp.concatenate([k_ref[prev, :], k_ref[rows, :]], axis=0)
            v = jnp.concatenate([v_ref[prev, :], v_ref[rows, :]], axis=0)
        return rows, v, _pair_dots(qs, k)

    blocks = [load(j) for j in range(nblk)]
    for j, (rows, v, scores) in enumerate(blocks):
        mask = _band_mask(SWA_MAX_DIST, chunk_start + j * BAND, n_pairs)
        stats = _pair_softmax(scores, mask)
        acc = _pair_pv(stats[0][1], stats[1][1], v)
        scales = []
        for g, (m, _, l) in enumerate(stats):
            sink = jnp.concatenate(
                [jnp.full((BAND, 1), sink_ref[SWA_GROUP * g + t], F32) for t in range(n_pairs)],
                axis=0)
            m_all = jnp.maximum(m, sink)
            keep = jnp.exp(m - m_all)
            den = l * keep + jnp.exp(sink - m_all)
            scales.append(keep / den)
        out = acc * _by_head(scales[0], scales[1], acc.shape)
        for t in range(n_pairs):
            o_ref[rows, t * LANES:(t + 1) * LANES] = out[t * BAND:(t + 1) * BAND, :].astype(o_ref.dtype)


def _swa(sinks, qa, ka, va, batch, seq, tq):
    n = qa.shape[0]
    chunks = seq // tq
    blk = tq // BAND
    cur = lambda b, c: (b * chunks + c, 0)
    prev = lambda b, c: (jnp.maximum((b * chunks + c) * blk - 1, 0), 0)
    return pl.pallas_call(
        _swa_kernel,
        grid=(batch, chunks),
        in_specs=[
            pl.BlockSpec(memory_space=pltpu.SMEM),
            pl.BlockSpec((tq, WIDTH_A), cur),
            pl.BlockSpec((tq, KV_A), cur),
            pl.BlockSpec((BAND, KV_A), prev),
            pl.BlockSpec((tq, KV_A), cur),
            pl.BlockSpec((BAND, KV_A), prev),
        ],
        out_specs=pl.BlockSpec((tq, WIDTH_A), cur),
        out_shape=jax.ShapeDtypeStruct((n, WIDTH_A), BF16),
        compiler_params=_params("parallel", "parallel"),
        name="swa",
    )(sinks, qa, ka, ka, va, va)


MOBA_UNROLL = 4
MOBA_VT_ROWS = HEAD_DIM + 16


def _moba_kernel(q_ref, qn_ref, k_ref, v_ref, o_ref,
                 kaug_ref, vt_ref, kmean_ref, acc_ref, m_ref, l_ref, s_ref, qa_ref):
    i = pl.program_id(2)
    nb = k_ref.shape[0] // MOBA_BLOCK
    blk_shape = (MOBA_BLOCK, LANES)
    lane = _lane(blk_shape)
    lo = lane < HEAD_DIM

    @pl.when(i == 0)
    def _prepare():
        kmean_ref[...] = jnp.zeros_like(kmean_ref)

        def body(n, carry):
            rows = pl.ds(pl.multiple_of(n * MOBA_BLOCK, MOBA_BLOCK), MOBA_BLOCK)
            kb = k_ref[rows, :].astype(F32)
            kmean_ref[pl.ds(n, 1), :] = jnp.sum(kb, axis=0, keepdims=True) * (1.0 / MOBA_BLOCK)
            block_hot = jnp.where((lane == n) | (lane == HEAD_DIM + n), 1.0, 0.0)
            kaug_ref[0, rows, :] = jnp.where(lo, kb, block_hot).astype(BF16)
            kaug_ref[1, rows, :] = jnp.where(lo, block_hot, kb).astype(BF16)
            vt = v_ref[rows, :].astype(F32).T
            ones = jnp.ones((MOBA_VT_ROWS - HEAD_DIM, MOBA_BLOCK), F32)
            vt_ref[n] = jnp.concatenate(
                [vt[:HEAD_DIM], ones, vt[HEAD_DIM:], ones], axis=0).astype(BF16)
            return carry

        lax.fori_loop(0, nb, body, 0)

    def select(q, own):
        kmean = kmean_ref[...]
        lo_m = _lane(kmean.shape) < HEAD_DIM
        sel_shape = (HEAD_DIM, MOBA_BLOCK)
        blk = lax.broadcasted_iota(jnp.int32, sel_shape, 0)
        blk_f = blk.astype(F32)
        pens = []
        for km in (jnp.where(lo_m, kmean, 0.0), jnp.where(lo_m, 0.0, kmean)):
            gate = _dot_t(km.astype(BF16), q)
            gate = jnp.where(blk < own, gate, -jnp.inf)
            sel = blk == own
            for _ in range(MOBA_TOPK):
                mx = jnp.max(gate, axis=0, keepdims=True)
                first = jnp.min(jnp.where(gate == mx, blk_f, float(HEAD_DIM)), axis=0, keepdims=True)
                pick = (blk_f == first) & (mx > -jnp.inf)
                sel = sel | pick
                gate = jnp.where(pick, -jnp.inf, gate)
            pens.append(jnp.where(sel, 0.0, NEG_INF))
        zeros = jnp.zeros(sel_shape, F32)
        q32 = q.astype(F32)
        return (jnp.where(lo, q32, jnp.concatenate([zeros, pens[0]], axis=0).T).astype(BF16),
                jnp.where(lo, jnp.concatenate([pens[1], zeros], axis=0).T, q32).astype(BF16))

    @pl.when(i == 0)
    def _first_block():
        qa_ref[0, 0], qa_ref[0, 1] = select(q_ref[...], 0)

    ahead0, ahead1 = select(qn_ref[...], i + 1)
    qa0 = qa_ref[i % 2, 0]
    qa1 = qa_ref[i % 2, 1]
    qa_ref[(i + 1) % 2, 0] = ahead0
    qa_ref[(i + 1) % 2, 1] = ahead1

    acc_ref[...] = jnp.zeros_like(acc_ref)
    m_ref[...] = jnp.full_like(m_ref, NEG_INF)
    l_ref[...] = jnp.zeros_like(l_ref)

    n_trips = i // MOBA_UNROLL + 1

    def scores(t, n):
        j = jnp.minimum(t * MOBA_UNROLL + n, i)
        rows = pl.ds(pl.multiple_of(j * MOBA_BLOCK, MOBA_BLOCK), MOBA_BLOCK)
        return [_dot_t(kaug_ref[hd, rows, :], qa) for hd, qa in enumerate((qa0, qa1))]

    def update(j, blk_scores, masked):
        vt = vt_ref[jnp.minimum(j, i)]
        ps, alphas = [], []
        for hd, s in enumerate(blk_scores):
            if masked:
                kk = lax.broadcasted_iota(jnp.int32, s.shape, 0)
                qq = lax.broadcasted_iota(jnp.int32, s.shape, 1)
                s = jnp.where(kk <= qq, s, NEG_INF)
            m_old = m_ref[hd]
            m_new = jnp.maximum(m_old, jnp.max(s, axis=0, keepdims=True))
            alphas.append(jnp.exp2(m_old - m_new))
            ps.append(jnp.exp2(s - m_new).astype(BF16))
            m_ref[hd] = m_new
        for hd in range(2):
            pv = _dot(vt[hd * MOBA_VT_ROWS:(hd + 1) * MOBA_VT_ROWS, :], ps[hd])
            acc_ref[hd] = acc_ref[hd] * alphas[hd] + pv[:HEAD_DIM, :]
            l_ref[hd] = l_ref[hd] * alphas[hd] + pv[HEAD_DIM:HEAD_DIM + 1, :]

    for n in range(MOBA_UNROLL):
        for hd, s in enumerate(scores(0, n)):
            s_ref[n, hd] = s

    def body(t, carry):
        for n in range(MOBA_UNROLL):
            cur = [s_ref[n, hd] for hd in range(2)]
            ahead = scores(t + 1, n)
            update(t * MOBA_UNROLL + n, cur, False)
            for hd in range(2):
                s_ref[n, hd] = ahead[hd]
        return carry

    lax.fori_loop(0, n_trips - 1, body, 0)
    for last in range(MOBA_UNROLL):
        @pl.when(i % MOBA_UNROLL == last)
        def _(last=last):
            for n in range(last + 1):
                update(i - last + n, [s_ref[n, hd] for hd in range(2)], n == last)
    out_t = jnp.concatenate([acc_ref[hd] * (1.0 / l_ref[hd]) for hd in range(2)], axis=0)
    o_ref[...] = out_t.T.astype(o_ref.dtype)


def _moba(qb, kb, vb, batch, seq):
    n = qb.shape[0]
    nb = seq // MOBA_BLOCK
    pairs = WIDTH_B // LANES
    assert nb <= HEAD_DIM, "block one-hot must fit in the other head's lanes"
    qmap = lambda b, p, i: (b * nb + i, p)
    qnext = lambda b, p, i: (b * nb + jnp.minimum(i + 1, nb - 1), p)
    kvmap = lambda b, p, i: (b, p)
    return pl.pallas_call(
        _moba_kernel,
        grid=(batch, pairs, nb),
        in_specs=[
            pl.BlockSpec((MOBA_BLOCK, LANES), qmap),
            pl.BlockSpec((MOBA_BLOCK, LANES), qnext),
            pl.BlockSpec((seq, LANES), kvmap),
            pl.BlockSpec((seq, LANES), kvmap),
        ],
        out_specs=pl.BlockSpec((MOBA_BLOCK, LANES), qmap),
        out_shape=jax.ShapeDtypeStruct((n, WIDTH_B), BF16),
        scratch_shapes=[
            pltpu.VMEM((2, seq, LANES), BF16),
            pltpu.VMEM((nb, 2 * MOBA_VT_ROWS, MOBA_BLOCK), BF16),
            pltpu.VMEM((HEAD_DIM, LANES), F32),
            pltpu.VMEM((2, HEAD_DIM, MOBA_BLOCK), F32),
            pltpu.VMEM((2, 1, MOBA_BLOCK), F32),
            pltpu.VMEM((2, 1, MOBA_BLOCK), F32),
            pltpu.VMEM((MOBA_UNROLL, 2, MOBA_BLOCK, MOBA_BLOCK), F32),
            pltpu.VMEM((2, 2, MOBA_BLOCK, LANES), BF16),
        ],
        compiler_params=_params("parallel", "parallel", "arbitrary"),
        name="moba",
    )(qb, qb, kb, vb)


DIL_CHUNK = DIL_PATTERNS[-1][1] * BAND
DIL_UNROLL = 4


def _dil_kernel(q_ref, kp_ref, k_ref, vp_ref, v_ref, o_ref, kk_ref, vv_ref, acc_ref, m_ref, l_ref):
    c = pl.program_id(2)
    kk_ref[0:DIL_CHUNK, :] = kp_ref[...]
    kk_ref[DIL_CHUNK:, :] = k_ref[...]
    vv_ref[0:DIL_CHUNK, :] = vp_ref[...]
    vv_ref[DIL_CHUNK:, :] = v_ref[...]
    blocks = DIL_CHUNK // BAND
    shape = (BAND, LANES)

    for pi, (_, d) in enumerate(DIL_PATTERNS):
        per_class = blocks // d

        def load(t, d=d, per_class=per_class):
            r = t // per_class
            j = t % per_class
            q0 = r + BAND * d * j
            k0 = DIL_CHUNK + q0 - BAND * d
            if d == 1:
                qrows = pl.ds(pl.multiple_of(q0, BAND), BAND)
                krows = pl.ds(pl.multiple_of(k0, BAND), 2 * BAND)
            else:
                qrows = pl.ds(q0, BAND, stride=d)
                krows = pl.ds(k0, 2 * BAND, stride=d)
            scores = _pair_dots(q_ref[qrows, :].astype(BF16), kk_ref[krows, :])
            return qrows, krows, (c * per_class + j) * BAND, scores

        def body(t, carry, pi=pi, load=load):
            group = [load(t * DIL_UNROLL + u) for u in range(DIL_UNROLL)]
            for qrows, krows, block_start, scores in group:
                mask = _band_mask(DIL_MAX_DIST, block_start)
                (m0, p0, l0), (m1, p1, l1) = _pair_softmax(scores, mask)
                acc_ref[pi, qrows, :] = _pair_pv(p0, p1, vv_ref[krows, :])
                m_ref[pi, qrows, :] = _by_head(m0, m1, shape)
                l_ref[pi, qrows, :] = _by_head(l0, l1, shape)
            return carry

        lax.fori_loop(0, blocks // DIL_UNROLL, body, 0)

    def combine(t, carry):
        rows = pl.ds(pl.multiple_of(t * BAND, BAND), BAND)
        ms = [m_ref[pi, rows, :] for pi in range(len(DIL_PATTERNS))]
        m_max = functools.reduce(jnp.maximum, ms)
        num = jnp.zeros(shape, F32)
        den = jnp.zeros(shape, F32)
        for pi, m in enumerate(ms):
            e = jnp.exp(m - m_max)
            num = num + acc_ref[pi, rows, :] * e
            den = den + l_ref[pi, rows, :] * e
        o_ref[rows, :] = (num / den).astype(o_ref.dtype)
        return carry

    lax.fori_loop(0, blocks, combine, 0)


def _dilated(qc, kc, vc, batch, seq):
    n = qc.shape[0]
    assert seq % DIL_CHUNK == 0
    chunks = seq // DIL_CHUNK
    pairs = WIDTH_C // LANES
    cur = lambda b, p, c: (b * chunks + c, p)
    prev = lambda b, p, c: (b * chunks + jnp.maximum(c - 1, 0), p)
    blk = (DIL_CHUNK, LANES)
    n_pat = len(DIL_PATTERNS)
    return pl.pallas_call(
        _dil_kernel,
        grid=(batch, pairs, chunks),
        in_specs=[pl.BlockSpec(blk, cur), pl.BlockSpec(blk, prev), pl.BlockSpec(blk, cur),
                  pl.BlockSpec(blk, prev), pl.BlockSpec(blk, cur)],
        out_specs=pl.BlockSpec(blk, cur),
        out_shape=jax.ShapeDtypeStruct((n, WIDTH_C), BF16),
        scratch_shapes=[
            pltpu.VMEM((2 * DIL_CHUNK, LANES), F32),
            pltpu.VMEM((2 * DIL_CHUNK, LANES), F32),
            pltpu.VMEM((n_pat, DIL_CHUNK, LANES), F32),
            pltpu.VMEM((n_pat, DIL_CHUNK, LANES), F32),
            pltpu.VMEM((n_pat, DIL_CHUNK, LANES), F32),
        ],
        compiler_params=_params("parallel", "parallel", "parallel"),
        name="dilated",
    )(qc, kc, kc, vc, vc)


def _merged_residual(x_ref, g_ref, oa_ref, ob_ref, oc_ref, wg_ref, wa_ref, wb_ref, wc_ref, wo_ref):
    x = x_ref[...]
    d = x.shape[1]
    h = _rms(x, g_ref[...]).astype(BF16)
    merged = None
    for br, (o_ref, w_ref) in enumerate(((oa_ref, wa_ref), (ob_ref, wb_ref), (oc_ref, wc_ref))):
        gate = _sigmoid(_dot(h, wg_ref[:, br * d:(br + 1) * d]))
        y = gate * _dot(o_ref[...], w_ref[...])
        merged = y if merged is None else merged + y
    return x + _dot(merged.astype(BF16), wo_ref[...])


def _merge_kernel(*refs):
    y_ref = refs[-1]
    y_ref[...] = _merged_residual(*refs[:-1])


def _merge_route_kernel(*refs):
    gf_ref, wr_ref, y_ref = refs[10:13]
    y = _merged_residual(*refs[:10])
    y_ref[...] = y
    _route_tile(y, gf_ref, wr_ref, *refs[13:])


def _merge(x2, gain, oa, ob, oc, w_gates, w_a, w_b, w_c, w_o, tm, route=None):
    n, d = x2.shape
    row = lambda i: (i, 0)
    full = lambda i: (0, 0)
    in_specs = [
        pl.BlockSpec((tm, d), row),
        pl.BlockSpec((1, d), full),
        pl.BlockSpec((tm, WIDTH_A), row),
        pl.BlockSpec((tm, WIDTH_B), row),
        pl.BlockSpec((tm, WIDTH_C), row),
        pl.BlockSpec(w_gates.shape, full),
        pl.BlockSpec(w_a.shape, full),
        pl.BlockSpec(w_b.shape, full),
        pl.BlockSpec(w_c.shape, full),
        pl.BlockSpec(w_o.shape, full),
    ]
    out_spec = pl.BlockSpec((tm, d), row)
    out_shape = jax.ShapeDtypeStruct((n, d), F32)
    args = (x2, gain, oa, ob, oc, w_gates, w_a, w_b, w_c, w_o)
    if route is None:
        return pl.pallas_call(
            _merge_kernel, grid=(n // tm,), in_specs=in_specs, out_specs=out_spec,
            out_shape=out_shape, compiler_params=_params("parallel"), name="merge",
        )(*args)
    gain_f, w_router_pad = route
    r_specs, r_shapes, r_scratch = _route_outputs(n, d, tm)
    return pl.pallas_call(
        _merge_route_kernel,
        grid=(n // tm,),
        in_specs=in_specs + [pl.BlockSpec((1, d), full), pl.BlockSpec(w_router_pad.shape, full)],
        out_specs=[out_spec] + r_specs,
        out_shape=[out_shape] + r_shapes,
        scratch_shapes=r_scratch,
        compiler_params=_params("arbitrary"),
        name="merge_route",
    )(*args, gain_f, w_router_pad)


def _swiglu(h, wg, wu, wd):
    a = _dot(h, wg)
    u = _dot(h, wu)
    return _dot((a * _sigmoid(a) * u).astype(BF16), wd)


def _ffn_kernel(x_ref, g_ref, wg_ref, wu_ref, wd_ref, y_ref):
    x = x_ref[...]
    h = _rms(x, g_ref[...]).astype(BF16)
    y_ref[...] = x + _swiglu(h, wg_ref[...], wu_ref[...], wd_ref[...])


def _ffn(x2, gain, wg, wu, wd, tm):
    n, d = x2.shape
    row = lambda i: (i, 0)
    full = lambda i: (0, 0)
    return pl.pallas_call(
        _ffn_kernel,
        grid=(n // tm,),
        in_specs=[
            pl.BlockSpec((tm, d), row),
            pl.BlockSpec((1, d), full),
            pl.BlockSpec(wg.shape, full),
            pl.BlockSpec(wu.shape, full),
            pl.BlockSpec(wd.shape, full),
        ],
        out_specs=pl.BlockSpec((tm, d), row),
        out_shape=jax.ShapeDtypeStruct((n, d), F32),
        compiler_params=_params("parallel"),
        name="ffn",
    )(x2, gain, wg, wu, wd)


MOE_GROUP = 2048
MOE_PACK_BLOCK = 256
MOE_CHUNK = 128

def _split_bf16(a):
    hi = a.astype(BF16)
    return hi, (a - hi.astype(F32)).astype(BF16)


def _route_tile(x, g_ref, wr_ref, h_ref, comb_ref, rank_ref, rankt_ref, count_ref):
    tm = x.shape[0]
    tile_in_group = pl.program_id(0) % (MOE_GROUP // tm)
    h = _rms(x, g_ref[...])
    h_ref[...] = h.astype(BF16)
    h_hi, h_lo = _split_bf16(h)
    w_hi, w_lo = _split_bf16(wr_ref[...])
    logits = _dot(h_hi, w_hi) + (_dot(h_hi, w_lo) + _dot(h_lo, w_hi))
    lane = _lane(logits.shape)
    logits = jnp.where(lane < N_EXPERTS, logits, -jnp.inf)
    tops = []
    for _ in range(TOP_K):
        mx = jnp.max(logits, axis=-1, keepdims=True)
        first = jnp.min(jnp.where(logits == mx, lane, LANES), axis=-1, keepdims=True)
        tops.append((mx, first))
        logits = jnp.where(lane == first, -jnp.inf, logits)
    (v1, i1), (v2, i2) = tops
    e2 = jnp.exp(v2 - v1)
    w1 = 1.0 / (1.0 + e2)
    w2 = e2 / (1.0 + e2)
    sel = jnp.where(lane == i1, 1.0, 0.0) + jnp.where(lane == i2, 1.0, 0.0)
    comb_ref[...] = jnp.where(lane == i1, w1, 0.0) + jnp.where(lane == i2, w2, 0.0)

    @pl.when(tile_in_group == 0)
    def _():
        count_ref[...] = jnp.zeros_like(count_ref)

    r = lax.broadcasted_iota(jnp.int32, (tm, tm), 0)
    c = lax.broadcasted_iota(jnp.int32, (tm, tm), 1)
    before = _dot(jnp.where(c < r, 1.0, 0.0).astype(BF16), sel.astype(BF16)) + count_ref[...]
    rank = jnp.where(sel > 0.0, before, -1.0)
    count_ref[...] = count_ref[...] + jnp.sum(sel, axis=0, keepdims=True)
    rank_ref[...] = rank
    rankt_ref[0] = rank.T[:N_EXPERTS, :]


def _route_outputs(n, d, tm):
    assert n % MOE_GROUP == 0 and MOE_GROUP % tm == 0
    per_group = MOE_GROUP // tm
    row = lambda i: (i, 0)
    specs = [pl.BlockSpec((tm, d), row),
             pl.BlockSpec((tm, LANES), row),
             pl.BlockSpec((tm, LANES), row),
             pl.BlockSpec((1, N_EXPERTS, tm), lambda i: (i // per_group, 0, i % per_group))]
    shapes = [jax.ShapeDtypeStruct((n, d), BF16),
              jax.ShapeDtypeStruct((n, LANES), F32),
              jax.ShapeDtypeStruct((n, LANES), F32),
              jax.ShapeDtypeStruct((n // MOE_GROUP, N_EXPERTS, MOE_GROUP), F32)]
    scratch = [pltpu.VMEM((1, LANES), F32)]
    return specs, shapes, scratch


def _experts_kernel(h_ref, comb_ref, rank_ref, rankt_ref, wg_ref, wu_ref, wd_ref, y_ref,
                    xs_ref, acc_ref, col_ref, start_ref):
    e = pl.program_id(1)
    f = pl.program_id(2)
    group, d = h_ref.shape
    n_blocks = group // MOE_CHUNK
    per_pack = MOE_PACK_BLOCK // MOE_CHUNK
    blocks = [slice(b * MOE_CHUNK, (b + 1) * MOE_CHUNK) for b in range(n_blocks)]
    pack_blocks = [slice(b * MOE_PACK_BLOCK, (b + 1) * MOE_PACK_BLOCK) for b in range(group // MOE_PACK_BLOCK)]

    def chunk_rows(m, n=1):
        return pl.ds(pl.multiple_of(m * MOE_CHUNK, MOE_CHUNK), n * MOE_CHUNK)

    def n_chunks():
        return (start_ref[n_blocks] + MOE_CHUNK - 1) // MOE_CHUNK

    def overlaps(m, b):
        return ((start_ref[b * per_pack] < (m + 1) * MOE_CHUNK)
                & (start_ref[(b + 1) * per_pack] > m * MOE_CHUNK))

    @pl.when((e == 0) & (f == 0))
    def _():
        y_ref[...] = jnp.zeros_like(y_ref)

    @pl.when(f == 0)
    def _dispatch():
        pick = _lane((group, LANES)) == e
        rank_col = jnp.sum(jnp.where(pick, rank_ref[...], 0.0), axis=1, keepdims=True)
        col_ref[0] = rank_col
        col_ref[1] = jnp.sum(jnp.where(pick, comb_ref[...], 0.0), axis=1, keepdims=True)
        rank_row = rankt_ref[0, pl.ds(e, 1), :]
        start_ref[0] = 0
        for b in range(n_blocks):
            last = jnp.max(rank_row[:, blocks[b]]).astype(jnp.int32) + 1
            start_ref[b + 1] = jnp.maximum(start_ref[b], last)

        def pack(m, carry):
            dest = (lax.broadcasted_iota(jnp.int32, (MOE_CHUNK, MOE_PACK_BLOCK), 0)
                    + m * MOE_CHUNK).astype(F32)
            xs_ref[chunk_rows(m), :] = jnp.zeros((MOE_CHUNK, d), BF16)
            acc_ref[chunk_rows(m), :] = jnp.zeros((MOE_CHUNK, d), F32)
            for b, tokens in enumerate(pack_blocks):
                @pl.when(overlaps(m, b))
                def _():
                    onehot = jnp.where(rank_row[:, tokens] == dest, 1.0, 0.0).astype(BF16)
                    rows = _dot(onehot, h_ref[tokens, :])
                    xs_ref[chunk_rows(m), :] = (xs_ref[chunk_rows(m), :].astype(F32) + rows).astype(BF16)
            return carry

        lax.fori_loop(0, n_chunks(), pack, 0)
        xs_ref[chunk_rows(n_chunks(), 2), :] = jnp.zeros((2 * MOE_CHUNK, d), BF16)
        acc_ref[chunk_rows(n_chunks(), 2), :] = jnp.zeros((2 * MOE_CHUNK, d), F32)

    last_slice = f == pl.num_programs(2) - 1

    @pl.when(jnp.logical_not(last_slice))
    def _():
        def expert(m, carry):
            xm = xs_ref[chunk_rows(m), :]
            acc_ref[chunk_rows(m), :] += _swiglu(xm, wg_ref[0], wu_ref[0], wd_ref[0])
            return carry

        lax.fori_loop(0, n_chunks(), expert, 0)

    @pl.when(last_slice)
    def _combine():
        def expert(m, carry):
            xm = xs_ref[chunk_rows(m), :]
            y = acc_ref[chunk_rows(m), :] + _swiglu(xm, wg_ref[0], wu_ref[0], wd_ref[0])
            y_hi, y_lo = _split_bf16(y)
            xs_ref[chunk_rows(m), :] = y_hi
            acc_ref[chunk_rows(m), :] = y_lo.astype(F32)
            return carry

        lax.fori_loop(0, n_chunks(), expert, 0)
        for b, tokens in enumerate(blocks):
            first = start_ref[b] // MOE_CHUNK
            y_hi = xs_ref[chunk_rows(first, 2), :]
            y_lo = acc_ref[chunk_rows(first, 2), :].astype(BF16)
            dest = (_lane((MOE_CHUNK, 2 * MOE_CHUNK)) + first * MOE_CHUNK).astype(F32)
            onehot = jnp.where(col_ref[0, tokens, :] == dest, 1.0, 0.0).astype(BF16)
            y_ref[tokens, :] += col_ref[1, tokens, :] * (_dot(onehot, y_hi) + _dot(onehot, y_lo))


def _experts(hb, comb, rank, rank_t, wg, wu, wd, ff_split):
    n, d = hb.shape
    n_e, _, d_ff = wg.shape
    ff = d_ff // ff_split
    group = lambda s, e, f: (s, 0)
    once = pl.Buffered(1)
    return pl.pallas_call(
        _experts_kernel,
        grid=(n // MOE_GROUP, n_e, ff_split),
        in_specs=[
            pl.BlockSpec((MOE_GROUP, d), group, pipeline_mode=once),
            pl.BlockSpec((MOE_GROUP, LANES), group, pipeline_mode=once),
            pl.BlockSpec((MOE_GROUP, LANES), group, pipeline_mode=once),
            pl.BlockSpec((1, N_EXPERTS, MOE_GROUP), lambda s, e, f: (s, 0, 0)),
            pl.BlockSpec((1, d, ff), lambda s, e, f: (e, 0, f)),
            pl.BlockSpec((1, d, ff), lambda s, e, f: (e, 0, f)),
            pl.BlockSpec((1, ff, d), lambda s, e, f: (e, f, 0)),
        ],
        out_specs=pl.BlockSpec((MOE_GROUP, d), group, pipeline_mode=once),
        out_shape=jax.ShapeDtypeStruct((n, d), F32),
        scratch_shapes=[
            pltpu.VMEM((MOE_GROUP + 2 * MOE_CHUNK, d), BF16),
            pltpu.VMEM((MOE_GROUP + 2 * MOE_CHUNK, d), F32),
            pltpu.VMEM((2, MOE_GROUP, 1), F32),
            pltpu.SMEM((MOE_GROUP // MOE_CHUNK + 1,), jnp.int32),
        ],
        compiler_params=_params("parallel", "arbitrary", "arbitrary"),
        name="experts",
    )(hb, comb, rank, rank_t, wg, wu, wd)


def _add_kernel(x_ref, y_ref, o_ref):
    o_ref[...] = x_ref[...] + y_ref[...]


def _add_norm_kernel(x_ref, y_ref, g_ref, o_ref):
    o_ref[...] = _rms(x_ref[...] + y_ref[...], g_ref[...])


def _norm_kernel(x_ref, g_ref, o_ref):
    o_ref[...] = _rms(x_ref[...], g_ref[...])


def _rowwise(body, name, tm, x2, *rest):
    n, d = x2.shape
    row = pl.BlockSpec((tm, d), lambda i: (i, 0))
    specs = [row if a.shape == (n, d) else pl.BlockSpec((1, d), lambda i: (0, 0)) for a in (x2,) + rest]
    return pl.pallas_call(
        body,
        grid=(n // tm,),
        in_specs=specs,
        out_specs=row,
        out_shape=jax.ShapeDtypeStruct((n, d), F32),
        compiler_params=_params("parallel"),
        name=name,
    )(x2, *rest)


def _rope_tables(seq):
    pos = jnp.arange(seq, dtype=F32)
    inv_freq = ROPE_THETA ** (-jnp.arange(0, HEAD_DIM, 2, dtype=F32) / HEAD_DIM)
    ang = pos[:, None] * inv_freq[None, :]
    cos, sin = jnp.cos(ang), jnp.sin(ang)
    reps = LANES // HEAD_DIM
    cos_t = jnp.tile(jnp.concatenate([cos, cos], axis=-1), (1, reps))
    sin_t = jnp.tile(jnp.concatenate([-sin, sin], axis=-1), (1, reps))
    return cos_t, sin_t


def _pair_heads_by_group(w, axis):
    shape = w.shape
    split = shape[:axis] + (SWA_KV_HEADS, SWA_GROUP, HEAD_DIM) + shape[axis + 1:]
    return jnp.swapaxes(w.reshape(split), axis, axis + 1).reshape(shape)


def _row_tile(n, want):
    t = min(n, want)
    assert n % t == 0
    return t


def kernel(x, norm_mix, w_in, attn_sinks, w_br_a, w_br_b, w_br_c, w_out, norm_ffn, w_ff_gate, w_ff_up, w_ff_down, w_router, w_moe_gate, w_moe_up, w_moe_down, norm_final):
    batch, seq, d = x.shape
    n = batch * seq
    depth = norm_mix.shape[0]
    tm = _row_tile(seq, 512)
    cos_t, sin_t = _rope_tables(seq)
    x2 = x.reshape(n, d).astype(F32)

    for layer in range(depth):
        w_l = w_in[layer]
        w_qa = _pair_heads_by_group(w_l[:, :WIDTH_A], 1)
        w_qkv = jnp.concatenate([w_qa, w_l[:, WIDTH_A:QKV_COLS]], axis=1).astype(BF16)
        w_gates = w_l[:, QKV_COLS:].astype(BF16)
        gain = norm_mix[layer].reshape(1, d)

        qa, ka, va, qb, kb, vb, qc, kc, vc = _inproj(x2, gain, w_qkv, cos_t, sin_t, seq, tm)
        oa = _swa(attn_sinks[layer], qa, ka, va, batch, seq, tm)
        ob = _moba(qb, kb, vb, batch, seq)
        oc = _dilated(qc, kc, vc, batch, seq)
        gain_f = norm_ffn[layer].reshape(1, d)
        idx = layer // 2
        dense = layer % 2 == 0
        route = None if dense else (gain_f, jnp.pad(w_router[idx], ((0, 0), (0, LANES - N_EXPERTS))))
        merged = _merge(x2, gain, oa, ob, oc, w_gates,
                        _pair_heads_by_group(w_br_a[layer], 0).astype(BF16),
                        w_br_b[layer].astype(BF16), w_br_c[layer].astype(BF16),
                        w_out[layer].astype(BF16), tm, route)

        if dense:
            x2 = _ffn(merged, gain_f, w_ff_gate[idx].astype(BF16), w_ff_up[idx].astype(BF16),
                      w_ff_down[idx].astype(BF16), tm)
        else:
            x2, hb, comb, rank, rank_t = merged
            y = _experts(hb, comb, rank, rank_t, w_moe_gate[idx].astype(BF16),
                         w_moe_up[idx].astype(BF16), w_moe_down[idx].astype(BF16), 2)
            if layer == depth - 1:
                out = _rowwise(_add_norm_kernel, "add_norm", tm, x2, y, norm_final.reshape(1, d))
                return out.reshape(batch, seq, d)
            x2 = _rowwise(_add_kernel, "add", tm, x2, y)

    return _rowwise(_norm_kernel, "final_norm", tm, x2, norm_final.reshape(1, d)).reshape(batch, seq, d)
```

```python
import functools

import jax
import jax.numpy as jnp
import numpy as np
from jax import lax
from jax.experimental import pallas as pl
from jax.experimental.pallas import tpu as pltpu

F32 = jnp.float32
BF16 = jnp.bfloat16

HEAD_DIM = 64
ATTN_SCALE = HEAD_DIM ** -0.5
LOG2_E = 1.4426950408889634
ROPE_THETA = 10000.0
RMS_EPS = 1e-5
NEG_INF = -1e30
BAND = 128
SWA_Q_HEADS, SWA_KV_HEADS = 6, 2
SWA_GROUP = SWA_Q_HEADS // SWA_KV_HEADS
SWA_MAX_DIST = 127
MOBA_HEADS, MOBA_BLOCK, MOBA_TOPK = 4, 256, 3
DIL_HEADS = 6
DIL_PATTERNS = ((128, 1), (512, 4), (2048, 16))
DIL_MAX_DIST = 128
N_EXPERTS, TOP_K = 8, 2

LANES = 128
VMEM_LIMIT = 56 * 1024 * 1024

WIDTH_A = SWA_Q_HEADS * HEAD_DIM
KV_A = SWA_KV_HEADS * HEAD_DIM
WIDTH_B = MOBA_HEADS * HEAD_DIM
WIDTH_C = DIL_HEADS * HEAD_DIM
QKV_COLS = WIDTH_A + 2 * KV_A + 3 * WIDTH_B + 3 * WIDTH_C


def _params(*sem):
    return pltpu.CompilerParams(dimension_semantics=sem, vmem_limit_bytes=VMEM_LIMIT)


def _row_call(body, name, steps, in_specs, args, out_specs, out_shapes, rider=None):
    in_specs, out_specs, out_shapes, args = list(in_specs), list(out_specs), list(out_shapes), list(args)
    if rider is not None:
        rows, cols = rider.shape
        assert rows % steps == 0
        n_in, n_out, inner = len(in_specs), len(out_specs), body
        slab = pl.BlockSpec((rows // steps, cols), lambda i: (i, 0))

        def body(*refs):
            inner(*refs[:n_in], *refs[n_in + 1:n_in + 1 + n_out])
            refs[-1][...] = refs[n_in][...].astype(BF16)

        in_specs.append(slab)
        args.append(rider)
        out_specs.append(slab)
        out_shapes.append(jax.ShapeDtypeStruct((rows, cols), BF16))
    return pl.pallas_call(
        body, grid=(steps,), in_specs=in_specs, out_specs=out_specs, out_shape=out_shapes,
        compiler_params=_params("parallel"), name=name,
    )(*args)


def _rms(x, gain):
    ms = jnp.mean(x * x, axis=-1, keepdims=True)
    return x * lax.rsqrt(ms + RMS_EPS) * gain


def _dot(a, b):
    return jnp.dot(a, b, preferred_element_type=F32)


def _dot_t(a, b):
    return lax.dot_general(a, b, (((1,), (1,)), ((), ())), preferred_element_type=F32)


def _sigmoid(z):
    return 1.0 / (1.0 + jnp.exp(-z))


def _lane(shape):
    return lax.broadcasted_iota(jnp.int32, shape, len(shape) - 1)


def _inproj_kernel(x_ref, g_ref, w_ref, cos_ref, sin_ref,
                   qa_ref, ka_ref, va_ref, qb_ref, kb_ref, vb_ref, qc_ref, kc_ref, vc_ref):
    tm = x_ref.shape[0]
    h = _rms(x_ref[...], g_ref[...]).astype(BF16)
    cos = cos_ref[...]
    sin = sin_ref[...]
    first_half = (_lane((tm, LANES)) & (HEAD_DIM // 2)) == 0

    def rope(z):
        rot = jnp.where(first_half,
                        pltpu.roll(z, LANES - HEAD_DIM // 2, 1),
                        pltpu.roll(z, HEAD_DIM // 2, 1))
        return z * cos + rot * sin

    z_all = _dot(h, w_ref[...])

    def proj(c0, width):
        return z_all[:, c0:c0 + width]

    def store(ref, z, roped, scale, dtype):
        for t in range(z.shape[1] // LANES):
            zt = z[:, t * LANES:(t + 1) * LANES]
            if roped:
                zt = rope(zt)
            if scale is not None:
                zt = zt * scale
            ref[:, t * LANES:(t + 1) * LANES] = zt.astype(dtype)

    c = 0
    store(qa_ref, proj(c, WIDTH_A), True, ATTN_SCALE, BF16); c += WIDTH_A
    kv = proj(c, 2 * KV_A); c += 2 * KV_A
    store(ka_ref, kv[:, :KV_A], True, None, BF16)
    store(va_ref, kv[:, KV_A:], False, None, BF16)
    store(qb_ref, proj(c, WIDTH_B), True, ATTN_SCALE * LOG2_E, BF16); c += WIDTH_B
    store(kb_ref, proj(c, WIDTH_B), True, None, BF16); c += WIDTH_B
    store(vb_ref, proj(c, WIDTH_B), False, None, BF16); c += WIDTH_B
    store(qc_ref, proj(c, WIDTH_C), True, ATTN_SCALE, F32); c += WIDTH_C
    store(kc_ref, proj(c, WIDTH_C), True, None, F32); c += WIDTH_C
    store(vc_ref, proj(c, WIDTH_C), False, None, F32)


def _inproj(x2, gain, w_qkv, cos_t, sin_t, seq, tm, rider=None):
    n, d = x2.shape
    tiles_per_seq = seq // tm
    row = lambda i: (i, 0)
    widths = (WIDTH_A, KV_A, KV_A, WIDTH_B, WIDTH_B, WIDTH_B, WIDTH_C, WIDTH_C, WIDTH_C)
    dtypes = (BF16,) * 6 + (F32,) * 3
    return _row_call(
        _inproj_kernel, "inproj", n // tm,
        [
            pl.BlockSpec((tm, d), row),
            pl.BlockSpec((1, d), lambda i: (0, 0)),
            pl.BlockSpec((d, QKV_COLS), lambda i: (0, 0)),
            pl.BlockSpec((tm, LANES), lambda i: (i % tiles_per_seq, 0)),
            pl.BlockSpec((tm, LANES), lambda i: (i % tiles_per_seq, 0)),
        ],
        (x2, gain, w_qkv, cos_t, sin_t),
        [pl.BlockSpec((tm, w), row) for w in widths],
        [jax.ShapeDtypeStruct((n, w), dt) for w, dt in zip(widths, dtypes)],
        rider)


def _band_mask(max_dist, block_start, stack=1):
    qi = lax.broadcasted_iota(jnp.int32, (stack * BAND, 2 * BAND), 0) % BAND
    kj = lax.broadcasted_iota(jnp.int32, (stack * BAND, 2 * BAND), 1) - BAND
    diff = qi - kj
    return (diff >= 0) & (diff <= max_dist) & (kj + block_start >= 0)


def _split_heads(a):
    a32 = a.astype(F32)
    lo = _lane(a.shape) < HEAD_DIM
    return jnp.where(lo, a32, 0.0).astype(BF16), jnp.where(lo, 0.0, a32).astype(BF16)


def _pair_dots(q, k):
    return [_dot_t(q, kh) for kh in _split_heads(k)]


def _pair_softmax(scores, mask):
    out = []
    for s in scores:
        s = jnp.where(mask, s, NEG_INF)
        m = jnp.max(s, axis=-1, keepdims=True)
        p = jnp.exp(s - m)
        out.append((m, p, jnp.sum(p, axis=-1, keepdims=True)))
    return out


def _pair_pv(p0, p1, v):
    pc = jnp.concatenate([p0.astype(BF16), p1.astype(BF16)], axis=1)
    return _dot(pc, jnp.concatenate(_split_heads(v), axis=0))


def _by_head(lo_val, hi_val, shape):
    return jnp.where(_lane(shape) < HEAD_DIM, lo_val, hi_val)


def _swa_kernel(sink_ref, q_ref, k_ref, kp_ref, v_ref, vp_ref, o_ref):
    nblk = q_ref.shape[0] // BAND
    chunk_start = pl.program_id(1) * q_ref.shape[0]
    n_pairs = WIDTH_A // LANES
    def load(j):
        rows = slice(j * BAND, (j + 1) * BAND)
        prev = slice((j - 1) * BAND, j * BAND)
        q = q_ref[rows, :]
        qs = jnp.concatenate([q[:, t * LANES:(t + 1) * LANES] for t in range(n_pairs)], axis=0)
        if j == 0:
            k = jnp.concatenate([kp_ref[...], k_ref[rows, :]], axis=0)
            v = jnp.concatenate([vp_ref[...], v_ref[rows, :]], axis=0)
        else:
            k = jnp.concatenate([k_ref[prev, :], k_ref[rows, :]], axis=0)
            v = jnp.concatenate([v_ref[prev, :], v_ref[rows, :]], axis=0)
        return rows, v, _pair_dots(qs, k)

    blocks = [load(j) for j in range(nblk)]
    for j, (rows, v, scores) in enumerate(blocks):
        mask = _band_mask(SWA_MAX_DIST, chunk_start + j * BAND, n_pairs)
        stats = _pair_softmax(scores, mask)
        acc = _pair_pv(stats[0][1], stats[1][1], v)
        scales = []
        for g, (m, _, l) in enumerate(stats):
            sink = jnp.concatenate(
                [jnp.full((BAND, 1), sink_ref[SWA_GROUP * g + t], F32) for t in range(n_pairs)],
                axis=0)
            m_all = jnp.maximum(m, sink)
            keep = jnp.exp(m - m_all)
            den = l * keep + jnp.exp(sink - m_all)
            scales.append(keep / den)
        out = acc * _by_head(scales[0], scales[1], acc.shape)
        for t in range(n_pairs):
            o_ref[rows, t * LANES:(t + 1) * LANES] = out[t * BAND:(t + 1) * BAND, :].astype(o_ref.dtype)


def _swa(sinks, qa, ka, va, batch, seq, tq):
    n = qa.shape[0]
    chunks = seq // tq
    blk = tq // BAND
    cur = lambda b, c: (b * chunks + c, 0)
    prev = lambda b, c: (jnp.maximum((b * chunks + c) * blk - 1, 0), 0)
    return pl.pallas_call(
        _swa_kernel,
        grid=(batch, chunks),
        in_specs=[
            pl.BlockSpec(memory_space=pltpu.SMEM),
            pl.BlockSpec((tq, WIDTH_A), cur),
            pl.BlockSpec((tq, KV_A), cur),
            pl.BlockSpec((BAND, KV_A), prev),
            pl.BlockSpec((tq, KV_A), cur),
            pl.BlockSpec((BAND, KV_A), prev),
        ],
        out_specs=pl.BlockSpec((tq, WIDTH_A), cur),
        out_shape=jax.ShapeDtypeStruct((n, WIDTH_A), BF16),
        compiler_params=_params("parallel", "parallel"),
        name="swa",
    )(sinks, qa, ka, ka, va, va)


MOBA_UNROLL = 4
MOBA_VT_ROWS = HEAD_DIM + 16


def _moba_kernel(q_ref, qn_ref, k_ref, v_ref, o_ref,
                 kaug_ref, vt_ref, kmean_ref, acc_ref, m_ref, l_ref, s_ref, qa_ref):
    i = pl.program_id(2)
    nb = k_ref.shape[0] // MOBA_BLOCK
    blk_shape = (MOBA_BLOCK, LANES)
    lane = _lane(blk_shape)
    lo = lane < HEAD_DIM

    @pl.when(i == 0)
    def _prepare():
        kmean_ref[...] = jnp.zeros_like(kmean_ref)

        def body(n, carry):
            rows = pl.ds(pl.multiple_of(n * MOBA_BLOCK, MOBA_BLOCK), MOBA_BLOCK)
            kb = k_ref[rows, :].astype(F32)
            kmean_ref[pl.ds(n, 1), :] = jnp.sum(kb, axis=0, keepdims=True) * (1.0 / MOBA_BLOCK)
            block_hot = jnp.where((lane == n) | (lane == HEAD_DIM + n), 1.0, 0.0)
            kaug_ref[0, rows, :] = jnp.where(lo, kb, block_hot).astype(BF16)
            kaug_ref[1, rows, :] = jnp.where(lo, block_hot, kb).astype(BF16)
            vt = v_ref[rows, :].astype(F32).T
            ones = jnp.ones((MOBA_VT_ROWS - HEAD_DIM, MOBA_BLOCK), F32)
            vt_ref[n] = jnp.concatenate(
                [vt[:HEAD_DIM], ones, vt[HEAD_DIM:], ones], axis=0).astype(BF16)
            return carry

        lax.fori_loop(0, nb, body, 0)

    def select(q, own):
        kmean = kmean_ref[...]
        lo_m = _lane(kmean.shape) < HEAD_DIM
        sel_shape = (HEAD_DIM, MOBA_BLOCK)
        blk = lax.broadcasted_iota(jnp.int32, sel_shape, 0)
        blk_f = blk.astype(F32)
        pens = []
        for km in (jnp.where(lo_m, kmean, 0.0), jnp.where(lo_m, 0.0, kmean)):
            gate = _dot_t(km.astype(BF16), q)
            gate = jnp.where(blk < own, gate, -jnp.inf)
            sel = blk == own
            for _ in range(MOBA_TOPK):
                mx = jnp.max(gate, axis=0, keepdims=True)
                first = jnp.min(jnp.where(gate == mx, blk_f, float(HEAD_DIM)), axis=0, keepdims=True)
                pick = (blk_f == first) & (mx > -jnp.inf)
                sel = sel | pick
                gate = jnp.where(pick, -jnp.inf, gate)
            pens.append(jnp.where(sel, 0.0, NEG_INF))
        zeros = jnp.zeros(sel_shape, F32)
        q32 = q.astype(F32)
        return (jnp.where(lo, q32, jnp.concatenate([zeros, pens[0]], axis=0).T).astype(BF16),
                jnp.where(lo, jnp.concatenate([pens[1], zeros], axis=0).T, q32).astype(BF16))

    @pl.when(i == 0)
    def _first_block():
        qa_ref[0, 0], qa_ref[0, 1] = select(q_ref[...], 0)

    ahead0, ahead1 = select(qn_ref[...], i + 1)
    qa0 = qa_ref[i % 2, 0]
    qa1 = qa_ref[i % 2, 1]
    qa_ref[(i + 1) % 2, 0] = ahead0
    qa_ref[(i + 1) % 2, 1] = ahead1

    acc_ref[...] = jnp.zeros_like(acc_ref)
    m_ref[...] = jnp.full_like(m_ref, NEG_INF)
    l_ref[...] = jnp.zeros_like(l_ref)

    n_trips = i // MOBA_UNROLL + 1

    def scores(t, n):
        j = jnp.minimum(t * MOBA_UNROLL + n, i)
        rows = pl.ds(pl.multiple_of(j * MOBA_BLOCK, MOBA_BLOCK), MOBA_BLOCK)
        return [_dot_t(kaug_ref[hd, rows, :], qa) for hd, qa in enumerate((qa0, qa1))]

    def update(j, blk_scores, masked):
        vt = vt_ref[jnp.minimum(j, i)]
        ps, alphas = [], []
        for hd, s in enumerate(blk_scores):
            if masked:
                kk = lax.broadcasted_iota(jnp.int32, s.shape, 0)
                qq = lax.broadcasted_iota(jnp.int32, s.shape, 1)
                s = jnp.where(kk <= qq, s, NEG_INF)
            m_old = m_ref[hd]
            m_new = jnp.maximum(m_old, jnp.max(s, axis=0, keepdims=True))
            alphas.append(jnp.exp2(m_old - m_new))
            ps.append(jnp.exp2(s - m_new).astype(BF16))
            m_ref[hd] = m_new
        for hd in range(2):
            pv = _dot(vt[hd * MOBA_VT_ROWS:(hd + 1) * MOBA_VT_ROWS, :], ps[hd])
            acc_ref[hd] = acc_ref[hd] * alphas[hd] + pv[:HEAD_DIM, :]
            l_ref[hd] = l_ref[hd] * alphas[hd] + pv[HEAD_DIM:HEAD_DIM + 1, :]

    for n in range(MOBA_UNROLL):
        for hd, s in enumerate(scores(0, n)):
            s_ref[n, hd] = s

    def body(t, carry):
        for n in range(MOBA_UNROLL):
            cur = [s_ref[n, hd] for hd in range(2)]
            ahead = scores(t + 1, n)
            update(t * MOBA_UNROLL + n, cur, False)
            for hd in range(2):
                s_ref[n, hd] = ahead[hd]
        return carry

    lax.fori_loop(0, n_trips - 1, body, 0)
    for last in range(MOBA_UNROLL):
        @pl.when(i % MOBA_UNROLL == last)
        def _(last=last):
            for n in range(last + 1):
                update(i - last + n, [s_ref[n, hd] for hd in range(2)], n == last)
    out_t = jnp.concatenate([acc_ref[hd] * (1.0 / l_ref[hd]) for hd in range(2)], axis=0)
    o_ref[...] = out_t.T.astype(o_ref.dtype)


def _moba(qb, kb, vb, batch, seq):
    n = qb.shape[0]
    nb = seq // MOBA_BLOCK
    pairs = WIDTH_B // LANES
    assert nb <= HEAD_DIM, "block one-hot must fit in the other head's lanes"
    qmap = lambda b, p, i: (b * nb + i, p)
    qnext = lambda b, p, i: (b * nb + jnp.minimum(i + 1, nb - 1), p)
    kvmap = lambda b, p, i: (b, p)
    return pl.pallas_call(
        _moba_kernel,
        grid=(batch, pairs, nb),
        in_specs=[
            pl.BlockSpec((MOBA_BLOCK, LANES), qmap),
            pl.BlockSpec((MOBA_BLOCK, LANES), qnext),
            pl.BlockSpec((seq, LANES), kvmap),
            pl.BlockSpec((seq, LANES), kvmap),
        ],
        out_specs=pl.BlockSpec((MOBA_BLOCK, LANES), qmap),
        out_shape=jax.ShapeDtypeStruct((n, WIDTH_B), BF16),
        scratch_shapes=[
            pltpu.VMEM((2, seq, LANES), BF16),
            pltpu.VMEM((nb, 2 * MOBA_VT_ROWS, MOBA_BLOCK), BF16),
            pltpu.VMEM((HEAD_DIM, LANES), F32),
            pltpu.VMEM((2, HEAD_DIM, MOBA_BLOCK), F32),
            pltpu.VMEM((2, 1, MOBA_BLOCK), F32),
            pltpu.VMEM((2, 1, MOBA_BLOCK), F32),
            pltpu.VMEM((MOBA_UNROLL, 2, MOBA_BLOCK, MOBA_BLOCK), F32),
            pltpu.VMEM((2, 2, MOBA_BLOCK, LANES), BF16),
        ],
        compiler_params=_params("parallel", "parallel", "arbitrary"),
        name="moba",
    )(qb, qb, kb, vb)


DIL_CHUNK = DIL_PATTERNS[-1][1] * BAND
DIL_UNROLL = 4


def _dil_kernel(q_ref, kp_ref, k_ref, vp_ref, v_ref, o_ref, kk_ref, vv_ref, acc_ref, m_ref, l_ref):
    c = pl.program_id(2)
    kk_ref[0:DIL_CHUNK, :] = kp_ref[...]
    kk_ref[DIL_CHUNK:, :] = k_ref[...]
    vv_ref[0:DIL_CHUNK, :] = vp_ref[...]
    vv_ref[DIL_CHUNK:, :] = v_ref[...]
    blocks = DIL_CHUNK // BAND
    shape = (BAND, LANES)

    for pi, (_, d) in enumerate(DIL_PATTERNS):
        per_class = blocks // d

        def load(t, d=d, per_class=per_class):
            r = t // per_class
            j = t % per_class
            q0 = r + BAND * d * j
            k0 = DIL_CHUNK + q0 - BAND * d
            if d == 1:
                qrows = pl.ds(pl.multiple_of(q0, BAND), BAND)
                krows = pl.ds(pl.multiple_of(k0, BAND), 2 * BAND)
            else:
                qrows = pl.ds(q0, BAND, stride=d)
                krows = pl.ds(k0, 2 * BAND, stride=d)
            scores = _pair_dots(q_ref[qrows, :].astype(BF16), kk_ref[krows, :])
            return qrows, krows, (c * per_class + j) * BAND, scores

        def body(t, carry, pi=pi, load=load):
            group = [load(t * DIL_UNROLL + u) for u in range(DIL_UNROLL)]
            for qrows, krows, block_start, scores in group:
                mask = _band_mask(DIL_MAX_DIST, block_start)
                (m0, p0, l0), (m1, p1, l1) = _pair_softmax(scores, mask)
                acc_ref[pi, qrows, :] = _pair_pv(p0, p1, vv_ref[krows, :])
                m_ref[pi, qrows, :] = _by_head(m0, m1, shape)
                l_ref[pi, qrows, :] = _by_head(l0, l1, shape)
            return carry

        lax.fori_loop(0, blocks // DIL_UNROLL, body, 0)

    def combine(t, carry):
        rows = pl.ds(pl.multiple_of(t * BAND, BAND), BAND)
        ms = [m_ref[pi, rows, :] for pi in range(len(DIL_PATTERNS))]
        m_max = functools.reduce(jnp.maximum, ms)
        num = jnp.zeros(shape, F32)
        den = jnp.zeros(shape, F32)
        for pi, m in enumerate(ms):
            e = jnp.exp(m - m_max)
            num = num + acc_ref[pi, rows, :] * e
            den = den + l_ref[pi, rows, :] * e
        o_ref[rows, :] = (num / den).astype(o_ref.dtype)
        return carry

    lax.fori_loop(0, blocks, combine, 0)


def _dilated(qc, kc, vc, batch, seq):
    n = qc.shape[0]
    assert seq % DIL_CHUNK == 0
    chunks = seq // DIL_CHUNK
    pairs = WIDTH_C // LANES
    cur = lambda b, p, c: (b * chunks + c, p)
    prev = lambda b, p, c: (b * chunks + jnp.maximum(c - 1, 0), p)
    blk = (DIL_CHUNK, LANES)
    n_pat = len(DIL_PATTERNS)
    return pl.pallas_call(
        _dil_kernel,
        grid=(batch, pairs, chunks),
        in_specs=[pl.BlockSpec(blk, cur), pl.BlockSpec(blk, prev), pl.BlockSpec(blk, cur),
                  pl.BlockSpec(blk, prev), pl.BlockSpec(blk, cur)],
        out_specs=pl.BlockSpec(blk, cur),
        out_shape=jax.ShapeDtypeStruct((n, WIDTH_C), BF16),
        scratch_shapes=[
            pltpu.VMEM((2 * DIL_CHUNK, LANES), F32),
            pltpu.VMEM((2 * DIL_CHUNK, LANES), F32),
            pltpu.VMEM((n_pat, DIL_CHUNK, LANES), F32),
            pltpu.VMEM((n_pat, DIL_CHUNK, LANES), F32),
            pltpu.VMEM((n_pat, DIL_CHUNK, LANES), F32),
        ],
        compiler_params=_params("parallel", "parallel", "parallel"),
        name="dilated",
    )(qc, kc, kc, vc, vc)


def _merged_residual(x_ref, g_ref, oa_ref, ob_ref, oc_ref, wg_ref, wa_ref, wb_ref, wc_ref, wo_ref):
    x = x_ref[...]
    d = x.shape[1]
    h = _rms(x, g_ref[...]).astype(BF16)
    merged = None
    for br, (o_ref, w_ref) in enumerate(((oa_ref, wa_ref), (ob_ref, wb_ref), (oc_ref, wc_ref))):
        gate = _sigmoid(_dot(h, wg_ref[:, br * d:(br + 1) * d]))
        y = gate * _dot(o_ref[...], w_ref[...])
        merged = y if merged is None else merged + y
    return x + _dot(merged.astype(BF16), wo_ref[...])


def _merge_kernel(*refs):
    y_ref = refs[-1]
    y_ref[...] = _merged_residual(*refs[:-1])


def _merge(x2, gain, oa, ob, oc, w_gates, w_a, w_b, w_c, w_o, tm, rider=None):
    n, d = x2.shape
    row = lambda i: (i, 0)
    full = lambda i: (0, 0)
    return _row_call(
        _merge_kernel, "merge", n // tm,
        [
            pl.BlockSpec((tm, d), row),
            pl.BlockSpec((1, d), full),
            pl.BlockSpec((tm, WIDTH_A), row),
            pl.BlockSpec((tm, WIDTH_B), row),
            pl.BlockSpec((tm, WIDTH_C), row),
            pl.BlockSpec(w_gates.shape, full),
            pl.BlockSpec(w_a.shape, full),
            pl.BlockSpec(w_b.shape, full),
            pl.BlockSpec(w_c.shape, full),
            pl.BlockSpec(w_o.shape, full),
        ],
        (x2, gain, oa, ob, oc, w_gates, w_a, w_b, w_c, w_o),
        [pl.BlockSpec((tm, d), row)],
        [jax.ShapeDtypeStruct((n, d), F32)],
        rider)


def _swiglu(h, wg, wu, wd):
    a = _dot(h, wg)
    u = _dot(h, wu)
    return _dot((a * _sigmoid(a) * u).astype(BF16), wd)


def _ffn_kernel(x_ref, g_ref, wg_ref, wu_ref, wd_ref, y_ref):
    x = x_ref[...]
    h = _rms(x, g_ref[...]).astype(BF16)
    y_ref[...] = x + _swiglu(h, wg_ref[...], wu_ref[...], wd_ref[...])


def _ffn(x2, gain, wg, wu, wd, tm, rider=None):
    n, d = x2.shape
    row = lambda i: (i, 0)
    full = lambda i: (0, 0)
    once = pl.Buffered(1)
    return _row_call(
        _ffn_kernel, "ffn", n // tm,
        [
            pl.BlockSpec((tm, d), row),
            pl.BlockSpec((1, d), full),
            pl.BlockSpec(wg.shape, full, pipeline_mode=once),
            pl.BlockSpec(wu.shape, full, pipeline_mode=once),
            pl.BlockSpec(wd.shape, full, pipeline_mode=once),
        ],
        (x2, gain, wg, wu, wd),
        [pl.BlockSpec((tm, d), row)],
        [jax.ShapeDtypeStruct((n, d), F32)],
        rider)


MOE_GROUP = 2048
MOE_PACK_BLOCK = 256
MOE_CHUNK = 128

def _split_bf16(a):
    hi = a.astype(BF16)
    return hi, (a - hi.astype(F32)).astype(BF16)


def _route_tile(x, g_ref, wr_ref, h_ref, comb_ref, rank_ref, rankt_ref, count_ref):
    tm = x.shape[0]
    tile_in_group = pl.program_id(0) % (MOE_GROUP // tm)
    h = _rms(x, g_ref[...])
    h_ref[...] = h.astype(BF16)
    h_hi, h_lo = _split_bf16(h)
    w_hi, w_lo = _split_bf16(wr_ref[...])
    logits = _dot(h_hi, w_hi) + (_dot(h_hi, w_lo) + _dot(h_lo, w_hi))
    lane = _lane(logits.shape)
    logits = jnp.where(lane < N_EXPERTS, logits, -jnp.inf)
    tops = []
    for _ in range(TOP_K):
        mx = jnp.max(logits, axis=-1, keepdims=True)
        first = jnp.min(jnp.where(logits == mx, lane, LANES), axis=-1, keepdims=True)
        tops.append((mx, first))
        logits = jnp.where(lane == first, -jnp.inf, logits)
    (v1, i1), (v2, i2) = tops
    e2 = jnp.exp(v2 - v1)
    w1 = 1.0 / (1.0 + e2)
    w2 = e2 / (1.0 + e2)
    sel = jnp.where(lane == i1, 1.0, 0.0) + jnp.where(lane == i2, 1.0, 0.0)
    comb_ref[...] = jnp.where(lane == i1, w1, 0.0) + jnp.where(lane == i2, w2, 0.0)

    @pl.when(tile_in_group == 0)
    def _():
        count_ref[...] = jnp.zeros_like(count_ref)

    r = lax.broadcasted_iota(jnp.int32, (tm, tm), 0)
    c = lax.broadcasted_iota(jnp.int32, (tm, tm), 1)
    before = _dot(jnp.where(c < r, 1.0, 0.0).astype(BF16), sel.astype(BF16)) + count_ref[...]
    rank = jnp.where(sel > 0.0, before, -1.0)
    count_ref[...] = count_ref[...] + jnp.sum(sel, axis=0, keepdims=True)
    rank_ref[...] = rank
    rankt_ref[0] = rank.T[:N_EXPERTS, :]


def _router_kernel(x_ref, g_ref, wr_ref, *out_and_scratch):
    _route_tile(x_ref[...], g_ref, wr_ref, *out_and_scratch)


def _router(x2, gain, w_router_pad, tm):
    n, d = x2.shape
    assert n % MOE_GROUP == 0 and MOE_GROUP % tm == 0
    per_group = MOE_GROUP // tm
    row = lambda i: (i, 0)
    return pl.pallas_call(
        _router_kernel,
        grid=(n // tm,),
        in_specs=[
            pl.BlockSpec((tm, d), row),
            pl.BlockSpec((1, d), lambda i: (0, 0)),
            pl.BlockSpec((d, LANES), lambda i: (0, 0)),
        ],
        out_specs=[pl.BlockSpec((tm, d), row),
                   pl.BlockSpec((tm, LANES), row),
                   pl.BlockSpec((tm, LANES), row),
                   pl.BlockSpec((1, N_EXPERTS, tm), lambda i: (i // per_group, 0, i % per_group))],
        out_shape=[jax.ShapeDtypeStruct((n, d), BF16),
                   jax.ShapeDtypeStruct((n, LANES), F32),
                   jax.ShapeDtypeStruct((n, LANES), F32),
                   jax.ShapeDtypeStruct((n // MOE_GROUP, N_EXPERTS, MOE_GROUP), F32)],
        scratch_shapes=[pltpu.VMEM((1, LANES), F32)],
        compiler_params=_params("arbitrary"),
        name="router",
    )(x2, gain, w_router_pad)


def _experts_kernel(h_ref, comb_ref, rank_ref, rankt_ref, wg_ref, wu_ref, wd_ref, y_ref,
                    xs_ref, acc_ref, col_ref, start_ref):
    e = pl.program_id(1)
    f = pl.program_id(2)
    group, d = h_ref.shape
    n_blocks = group // MOE_CHUNK
    per_pack = MOE_PACK_BLOCK // MOE_CHUNK
    blocks = [slice(b * MOE_CHUNK, (b + 1) * MOE_CHUNK) for b in range(n_blocks)]
    pack_blocks = [slice(b * MOE_PACK_BLOCK, (b + 1) * MOE_PACK_BLOCK) for b in range(group // MOE_PACK_BLOCK)]

    def chunk_rows(m, n=1):
        return pl.ds(pl.multiple_of(m * MOE_CHUNK, MOE_CHUNK), n * MOE_CHUNK)

    def n_chunks():
        return (start_ref[n_blocks] + MOE_CHUNK - 1) // MOE_CHUNK

    def overlaps(m, b):
        return ((start_ref[b * per_pack] < (m + 1) * MOE_CHUNK)
                & (start_ref[(b + 1) * per_pack] > m * MOE_CHUNK))

    @pl.when((e == 0) & (f == 0))
    def _():
        y_ref[...] = jnp.zeros_like(y_ref)

    @pl.when(f == 0)
    def _dispatch():
        pick = _lane((group, LANES)) == e
        rank_col = jnp.sum(jnp.where(pick, rank_ref[...], 0.0), axis=1, keepdims=True)
        col_ref[0] = rank_col
        col_ref[1] = jnp.sum(jnp.where(pick, comb_ref[...], 0.0), axis=1, keepdims=True)
        rank_row = rankt_ref[0, pl.ds(e, 1), :]
        start_ref[0] = 0
        for b in range(n_blocks):
            last = jnp.max(rank_row[:, blocks[b]]).astype(jnp.int32) + 1
            start_ref[b + 1] = jnp.maximum(start_ref[b], last)

        def pack(m, carry):
            dest = (lax.broadcasted_iota(jnp.int32, (MOE_CHUNK, MOE_PACK_BLOCK), 0)
                    + m * MOE_CHUNK).astype(F32)
            xs_ref[chunk_rows(m), :] = jnp.zeros((MOE_CHUNK, d), BF16)
            acc_ref[chunk_rows(m), :] = jnp.zeros((MOE_CHUNK, d), F32)
            for b, tokens in enumerate(pack_blocks):
                @pl.when(overlaps(m, b))
                def _():
                    onehot = jnp.where(rank_row[:, tokens] == dest, 1.0, 0.0).astype(BF16)
                    rows = _dot(onehot, h_ref[tokens, :])
                    xs_ref[chunk_rows(m), :] = (xs_ref[chunk_rows(m), :].astype(F32) + rows).astype(BF16)
            return carry

        lax.fori_loop(0, n_chunks(), pack, 0)
        xs_ref[chunk_rows(n_chunks(), 2), :] = jnp.zeros((2 * MOE_CHUNK, d), BF16)
        acc_ref[chunk_rows(n_chunks(), 2), :] = jnp.zeros((2 * MOE_CHUNK, d), F32)

    last_slice = f == pl.num_programs(2) - 1

    @pl.when(jnp.logical_not(last_slice))
    def _():
        def expert(m, carry):
            xm = xs_ref[chunk_rows(m), :]
            acc_ref[chunk_rows(m), :] += _swiglu(xm, wg_ref[0], wu_ref[0], wd_ref[0])
            return carry

        lax.fori_loop(0, n_chunks(), expert, 0)

    @pl.when(last_slice)
    def _combine():
        def expert(m, carry):
            xm = xs_ref[chunk_rows(m), :]
            y = acc_ref[chunk_rows(m), :] + _swiglu(xm, wg_ref[0], wu_ref[0], wd_ref[0])
            y_hi, y_lo = _split_bf16(y)
            xs_ref[chunk_rows(m), :] = y_hi
            acc_ref[chunk_rows(m), :] = y_lo.astype(F32)
            return carry

        lax.fori_loop(0, n_chunks(), expert, 0)
        for b, tokens in enumerate(blocks):
            first = start_ref[b] // MOE_CHUNK
            y_hi = xs_ref[chunk_rows(first, 2), :]
            y_lo = acc_ref[chunk_rows(first, 2), :].astype(BF16)
            dest = (_lane((MOE_CHUNK, 2 * MOE_CHUNK)) + first * MOE_CHUNK).astype(F32)
            onehot = jnp.where(col_ref[0, tokens, :] == dest, 1.0, 0.0).astype(BF16)
            y_ref[tokens, :] += col_ref[1, tokens, :] * (_dot(onehot, y_hi) + _dot(onehot, y_lo))


def _experts(hb, comb, rank, rank_t, wg, wu, wd, ff_split):
    n, d = hb.shape
    n_e, _, d_ff = wg.shape
    ff = d_ff // ff_split
    group = lambda s, e, f: (s, 0)
    once = pl.Buffered(1)
    return pl.pallas_call(
        _experts_kernel,
        grid=(n // MOE_GROUP, n_e, ff_split),
        in_specs=[
            pl.BlockSpec((MOE_GROUP, d), group, pipeline_mode=once),
            pl.BlockSpec((MOE_GROUP, LANES), group, pipeline_mode=once),
            pl.BlockSpec((MOE_GROUP, LANES), group, pipeline_mode=once),
            pl.BlockSpec((1, N_EXPERTS, MOE_GROUP), lambda s, e, f: (s, 0, 0)),
            pl.BlockSpec((1, d, ff), lambda s, e, f: (e, 0, f)),
            pl.BlockSpec((1, d, ff), lambda s, e, f: (e, 0, f)),
            pl.BlockSpec((1, ff, d), lambda s, e, f: (e, f, 0)),
        ],
        out_specs=pl.BlockSpec((MOE_GROUP, d), group, pipeline_mode=once),
        out_shape=jax.ShapeDtypeStruct((n, d), F32),
        scratch_shapes=[
            pltpu.VMEM((MOE_GROUP + 2 * MOE_CHUNK, d), BF16),
            pltpu.VMEM((MOE_GROUP + 2 * MOE_CHUNK, d), F32),
            pltpu.VMEM((2, MOE_GROUP, 1), F32),
            pltpu.SMEM((MOE_GROUP // MOE_CHUNK + 1,), jnp.int32),
        ],
        compiler_params=_params("parallel", "arbitrary", "arbitrary"),
        name="experts",
    )(hb, comb, rank, rank_t, wg, wu, wd)


def _add_kernel(x_ref, y_ref, o_ref):
    o_ref[...] = x_ref[...] + y_ref[...]


def _add_norm_kernel(x_ref, y_ref, g_ref, o_ref):
    o_ref[...] = _rms(x_ref[...] + y_ref[...], g_ref[...])


def _norm_kernel(x_ref, g_ref, o_ref):
    o_ref[...] = _rms(x_ref[...], g_ref[...])


def _rowwise(body, name, tm, x2, *rest):
    n, d = x2.shape
    row = pl.BlockSpec((tm, d), lambda i: (i, 0))
    specs = [row if a.shape == (n, d) else pl.BlockSpec((1, d), lambda i: (0, 0)) for a in (x2,) + rest]
    return pl.pallas_call(
        body,
        grid=(n // tm,),
        in_specs=specs,
        out_specs=row,
        out_shape=jax.ShapeDtypeStruct((n, d), F32),
        compiler_params=_params("parallel"),
        name=name,
    )(x2, *rest)


def _rope_tables(seq):
    pos = jnp.arange(seq, dtype=F32)
    inv_freq = ROPE_THETA ** (-jnp.arange(0, HEAD_DIM, 2, dtype=F32) / HEAD_DIM)
    ang = pos[:, None] * inv_freq[None, :]
    cos, sin = jnp.cos(ang), jnp.sin(ang)
    reps = LANES // HEAD_DIM
    cos_t = jnp.tile(jnp.concatenate([cos, cos], axis=-1), (1, reps))
    sin_t = jnp.tile(jnp.concatenate([-sin, sin], axis=-1), (1, reps))
    return cos_t, sin_t


def _pair_heads_by_group(w, axis):
    shape = w.shape
    split = shape[:axis] + (SWA_KV_HEADS, SWA_GROUP, HEAD_DIM) + shape[axis + 1:]
    return jnp.swapaxes(w.reshape(split), axis, axis + 1).reshape(shape)


def _row_tile(n, want):
    t = min(n, want)
    assert n % t == 0
    return t


def kernel(x, norm_mix, w_in, attn_sinks, w_br_a, w_br_b, w_br_c, w_out, norm_ffn, w_ff_gate, w_ff_up, w_ff_down, w_router, w_moe_gate, w_moe_up, w_moe_down, norm_final):
    batch, seq, d = x.shape
    n = batch * seq
    depth = norm_mix.shape[0]
    tm = _row_tile(seq, 512)
    cos_t, sin_t = _rope_tables(seq)
    x2 = x.reshape(n, d).astype(F32)
    expert_bf16 = {}

    for layer in range(depth):
        w_l = w_in[layer]
        w_qa = _pair_heads_by_group(w_l[:, :WIDTH_A], 1)
        w_qkv = jnp.concatenate([w_qa, w_l[:, WIDTH_A:QKV_COLS]], axis=1).astype(BF16)
        w_gates = w_l[:, QKV_COLS:].astype(BF16)
        gain = norm_mix[layer].reshape(1, d)

        idx = layer // 2
        is_moe = layer % 2 == 1
        flat = lambda w: w.reshape(-1, w.shape[-1])

        qa, ka, va, qb, kb, vb, qc, kc, vc, *cast = _inproj(
            x2, gain, w_qkv, cos_t, sin_t, seq, tm, flat(w_moe_down[idx]) if is_moe else None)
        if cast:
            expert_bf16["down"] = cast[0].reshape(w_moe_down[idx].shape)
        oa = _swa(attn_sinks[layer], qa, ka, va, batch, seq, tm)
        ob = _moba(qb, kb, vb, batch, seq)
        oc = _dilated(qc, kc, vc, batch, seq)
        gain_f = norm_ffn[layer].reshape(1, d)
        x2, *cast = _merge(x2, gain, oa, ob, oc, w_gates,
                           _pair_heads_by_group(w_br_a[layer], 0).astype(BF16),
                           w_br_b[layer].astype(BF16), w_br_c[layer].astype(BF16),
                           w_out[layer].astype(BF16), tm, flat(w_moe_up[idx]) if is_moe else None)
        if cast:
            expert_bf16["up"] = cast[0].reshape(w_moe_up[idx].shape)

        if not is_moe:
            next_moe = layer + 1 < depth
            x2, *cast = _ffn(x2, gain_f, w_ff_gate[idx].astype(BF16), w_ff_up[idx].astype(BF16),
                             w_ff_down[idx].astype(BF16), tm,
                             flat(w_moe_gate[(layer + 1) // 2]) if next_moe else None)
            if cast:
                expert_bf16["gate"] = cast[0].reshape(w_moe_gate[(layer + 1) // 2].shape)
        else:
            w_r = jnp.pad(w_router[idx], ((0, 0), (0, LANES - N_EXPERTS)))
            hb, comb, rank, rank_t = _router(x2, gain_f, w_r, tm)
            y = _experts(hb, comb, rank, rank_t,
                         expert_bf16.pop("gate") if "gate" in expert_bf16 else w_moe_gate[idx].astype(BF16),
                         expert_bf16.pop("up"), expert_bf16.pop("down"), 2)
            if layer == depth - 1:
                out = _rowwise(_add_norm_kernel, "add_norm", tm, x2, y, norm_final.reshape(1, d))
                return out.reshape(batch, seq, d)
            x2 = _rowwise(_add_kernel, "add", tm, x2, y)

    return _rowwise(_norm_kernel, "final_norm", tm, x2, norm_final.reshape(1, d)).reshape(batch, seq, d)
```

```python
import functools

import jax
import jax.numpy as jnp
import numpy as np
from jax import lax
from jax.experimental import pallas as pl
from jax.experimental.pallas import tpu as pltpu

F32 = jnp.float32
BF16 = jnp.bfloat16

HEAD_DIM = 64
ATTN_SCALE = HEAD_DIM ** -0.5
LOG2_E = 1.4426950408889634
ROPE_THETA = 10000.0
RMS_EPS = 1e-5
NEG_INF = -1e30
BAND = 128
SWA_Q_HEADS, SWA_KV_HEADS = 6, 2
SWA_GROUP = SWA_Q_HEADS // SWA_KV_HEADS
SWA_MAX_DIST = 127
MOBA_HEADS, MOBA_BLOCK, MOBA_TOPK = 4, 256, 3
DIL_HEADS = 6
DIL_PATTERNS = ((128, 1), (512, 4), (2048, 16))
DIL_MAX_DIST = 128
N_EXPERTS, TOP_K = 8, 2

LANES = 128
VMEM_LIMIT = 56 * 1024 * 1024

WIDTH_A = SWA_Q_HEADS * HEAD_DIM
KV_A = SWA_KV_HEADS * HEAD_DIM
WIDTH_B = MOBA_HEADS * HEAD_DIM
WIDTH_C = DIL_HEADS * HEAD_DIM
QKV_COLS = WIDTH_A + 2 * KV_A + 3 * WIDTH_B + 3 * WIDTH_C


def _params(*sem):
    return pltpu.CompilerParams(dimension_semantics=sem, vmem_limit_bytes=VMEM_LIMIT)


def _row_call(body, name, steps, in_specs, args, out_specs, out_shapes, rider=None):
    in_specs, out_specs, out_shapes, args = list(in_specs), list(out_specs), list(out_shapes), list(args)
    if rider is not None:
        rows, cols = rider.shape
        assert rows % steps == 0
        n_in, n_out, inner = len(in_specs), len(out_specs), body
        slab = pl.BlockSpec((rows // steps, cols), lambda i: (i, 0))

        def body(*refs):
            inner(*refs[:n_in], *refs[n_in + 1:n_in + 1 + n_out])
            refs[-1][...] = refs[n_in][...].astype(BF16)

        in_specs.append(slab)
        args.append(rider)
        out_specs.append(slab)
        out_shapes.append(jax.ShapeDtypeStruct((rows, cols), BF16))
    return pl.pallas_call(
        body, grid=(steps,), in_specs=in_specs, out_specs=out_specs, out_shape=out_shapes,
        compiler_params=_params("parallel"), name=name,
    )(*args)


def _rms(x, gain):
    ms = jnp.mean(x * x, axis=-1, keepdims=True)
    return x * lax.rsqrt(ms + RMS_EPS) * gain


def _dot(a, b):
    return jnp.dot(a, b, preferred_element_type=F32)


def _dot_t(a, b):
    return lax.dot_general(a, b, (((1,), (1,)), ((), ())), preferred_element_type=F32)


def _sigmoid(z):
    return 1.0 / (1.0 + jnp.exp(-z))


def _lane(shape):
    return lax.broadcasted_iota(jnp.int32, shape, len(shape) - 1)


def _inproj_kernel(x_ref, g_ref, w_ref, cos_ref, sin_ref,
                   qa_ref, ka_ref, va_ref, qb_ref, kb_ref, vb_ref, qc_ref, kc_ref, vc_ref):
    tm = x_ref.shape[0]
    h = _rms(x_ref[...], g_ref[...]).astype(BF16)
    cos = cos_ref[...]
    sin = sin_ref[...]
    first_half = (_lane((tm, LANES)) & (HEAD_DIM // 2)) == 0

    def rope(z):
        rot = jnp.where(first_half,
                        pltpu.roll(z, LANES - HEAD_DIM // 2, 1),
                        pltpu.roll(z, HEAD_DIM // 2, 1))
        return z * cos + rot * sin

    z_all = _dot(h, w_ref[...])

    def proj(c0, width):
        return z_all[:, c0:c0 + width]

    def store(ref, z, roped, scale, dtype):
        for t in range(z.shape[1] // LANES):
            zt = z[:, t * LANES:(t + 1) * LANES]
            if roped:
                zt = rope(zt)
            if scale is not None:
                zt = zt * scale
            ref[:, t * LANES:(t + 1) * LANES] = zt.astype(dtype)

    c = 0
    store(qa_ref, proj(c, WIDTH_A), True, ATTN_SCALE, BF16); c += WIDTH_A
    kv = proj(c, 2 * KV_A); c += 2 * KV_A
    store(ka_ref, kv[:, :KV_A], True, None, BF16)
    store(va_ref, kv[:, KV_A:], False, None, BF16)
    store(qb_ref, proj(c, WIDTH_B), True, ATTN_SCALE * LOG2_E, BF16); c += WIDTH_B
    store(kb_ref, proj(c, WIDTH_B), True, None, BF16); c += WIDTH_B
    store(vb_ref, proj(c, WIDTH_B), False, None, BF16); c += WIDTH_B
    store(qc_ref, proj(c, WIDTH_C), True, ATTN_SCALE, F32); c += WIDTH_C
    store(kc_ref, proj(c, WIDTH_C), True, None, F32); c += WIDTH_C
    store(vc_ref, proj(c, WIDTH_C), False, None, F32)


def _inproj(x2, gain, w_qkv, cos_t, sin_t, seq, tm, rider=None):
    n, d = x2.shape
    tiles_per_seq = seq // tm
    row = lambda i: (i, 0)
    widths = (WIDTH_A, KV_A, KV_A, WIDTH_B, WIDTH_B, WIDTH_B, WIDTH_C, WIDTH_C, WIDTH_C)
    dtypes = (BF16,) * 6 + (F32,) * 3
    return _row_call(
        _inproj_kernel, "inproj", n // tm,
        [
            pl.BlockSpec((tm, d), row),
            pl.BlockSpec((1, d), lambda i: (0, 0)),
            pl.BlockSpec((d, QKV_COLS), lambda i: (0, 0)),
            pl.BlockSpec((tm, LANES), lambda i: (i % tiles_per_seq, 0)),
            pl.BlockSpec((tm, LANES), lambda i: (i % tiles_per_seq, 0)),
        ],
        (x2, gain, w_qkv, cos_t, sin_t),
        [pl.BlockSpec((tm, w), row) for w in widths],
        [jax.ShapeDtypeStruct((n, w), dt) for w, dt in zip(widths, dtypes)],
        rider)


def _band_mask(max_dist, block_start, stack=1):
    qi = lax.broadcasted_iota(jnp.int32, (stack * BAND, 2 * BAND), 0) % BAND
    kj = lax.broadcasted_iota(jnp.int32, (stack * BAND, 2 * BAND), 1) - BAND
    diff = qi - kj
    return (diff >= 0) & (diff <= max_dist) & (kj + block_start >= 0)


def _split_heads(a):
    a32 = a.astype(F32)
    lo = _lane(a.shape) < HEAD_DIM
    return jnp.where(lo, a32, 0.0).astype(BF16), jnp.where(lo, 0.0, a32).astype(BF16)


def _pair_dots(q, k):
    return [_dot_t(q, kh) for kh in _split_heads(k)]


def _pair_softmax(scores, mask):
    out = []
    for s in scores:
        s = jnp.where(mask, s, NEG_INF)
        m = jnp.max(s, axis=-1, keepdims=True)
        p = jnp.exp(s - m)
        out.append((m, p, jnp.sum(p, axis=-1, keepdims=True)))
    return out


def _pair_pv(p0, p1, v):
    pc = jnp.concatenate([p0.astype(BF16), p1.astype(BF16)], axis=1)
    return _dot(pc, jnp.concatenate(_split_heads(v), axis=0))


def _by_head(lo_val, hi_val, shape):
    return jnp.where(_lane(shape) < HEAD_DIM, lo_val, hi_val)


def _swa_kernel(sink_ref, q_ref, k_ref, kp_ref, v_ref, vp_ref, o_ref):
    nblk = q_ref.shape[0] // BAND
    chunk_start = pl.program_id(1) * q_ref.shape[0]
    n_pairs = WIDTH_A // LANES
    def load(j):
        rows = slice(j * BAND, (j + 1) * BAND)
        prev = slice((j - 1) * BAND, j * BAND)
        q = q_ref[rows, :]
        qs = jnp.concatenate([q[:, t * LANES:(t + 1) * LANES] for t in range(n_pairs)], axis=0)
        if j == 0:
            k = jnp.concatenate([kp_ref[...], k_ref[rows, :]], axis=0)
            v = jnp.concatenate([vp_ref[...], v_ref[rows, :]], axis=0)
        else:
            k = jnp.concatenate([k_ref[prev, :], k_ref[rows, :]], axis=0)
            v = jnp.concatenate([v_ref[prev, :], v_ref[rows, :]], axis=0)
        return rows, v, _pair_dots(qs, k)

    blocks = [load(j) for j in range(nblk)]
    for j, (rows, v, scores) in enumerate(blocks):
        mask = _band_mask(SWA_MAX_DIST, chunk_start + j * BAND, n_pairs)
        stats = _pair_softmax(scores, mask)
        acc = _pair_pv(stats[0][1], stats[1][1], v)
        scales = []
        for g, (m, _, l) in enumerate(stats):
            sink = jnp.concatenate(
                [jnp.full((BAND, 1), sink_ref[SWA_GROUP * g + t], F32) for t in range(n_pairs)],
                axis=0)
            m_all = jnp.maximum(m, sink)
            keep = jnp.exp(m - m_all)
            den = l * keep + jnp.exp(sink - m_all)
            scales.append(keep / den)
        out = acc * _by_head(scales[0], scales[1], acc.shape)
        for t in range(n_pairs):
            o_ref[rows, t * LANES:(t + 1) * LANES] = out[t * BAND:(t + 1) * BAND, :].astype(o_ref.dtype)


def _swa(sinks, qa, ka, va, batch, seq, tq):
    n = qa.shape[0]
    chunks = seq // tq
    blk = tq // BAND
    cur = lambda b, c: (b * chunks + c, 0)
    prev = lambda b, c: (jnp.maximum((b * chunks + c) * blk - 1, 0), 0)
    return pl.pallas_call(
        _swa_kernel,
        grid=(batch, chunks),
        in_specs=[
            pl.BlockSpec(memory_space=pltpu.SMEM),
            pl.BlockSpec((tq, WIDTH_A), cur),
            pl.BlockSpec((tq, KV_A), cur),
            pl.BlockSpec((BAND, KV_A), prev),
            pl.BlockSpec((tq, KV_A), cur),
            pl.BlockSpec((BAND, KV_A), prev),
        ],
        out_specs=pl.BlockSpec((tq, WIDTH_A), cur),
        out_shape=jax.ShapeDtypeStruct((n, WIDTH_A), BF16),
        compiler_params=_params("parallel", "parallel"),
        name="swa",
    )(sinks, qa, ka, ka, va, va)


MOBA_UNROLL = 4
MOBA_VT_ROWS = HEAD_DIM + 16


def _moba_kernel(q_ref, qn_ref, k_ref, v_ref, o_ref,
                 kaug_ref, vt_ref, kmean_ref, acc_ref, m_ref, l_ref, s_ref, qa_ref):
    i = pl.program_id(2)
    nb = k_ref.shape[0] // MOBA_BLOCK
    blk_shape = (MOBA_BLOCK, LANES)
    lane = _lane(blk_shape)
    lo = lane < HEAD_DIM

    @pl.when(i == 0)
    def _prepare():
        kmean_ref[...] = jnp.zeros_like(kmean_ref)

        def body(n, carry):
            rows = pl.ds(pl.multiple_of(n * MOBA_BLOCK, MOBA_BLOCK), MOBA_BLOCK)
            kb = k_ref[rows, :].astype(F32)
            kmean_ref[pl.ds(n, 1), :] = jnp.sum(kb, axis=0, keepdims=True) * (1.0 / MOBA_BLOCK)
            block_hot = jnp.where((lane == n) | (lane == HEAD_DIM + n), 1.0, 0.0)
            kaug_ref[0, rows, :] = jnp.where(lo, kb, block_hot).astype(BF16)
            kaug_ref[1, rows, :] = jnp.where(lo, block_hot, kb).astype(BF16)
            vt = v_ref[rows, :].astype(F32).T
            ones = jnp.ones((MOBA_VT_ROWS - HEAD_DIM, MOBA_BLOCK), F32)
            vt_ref[n] = jnp.concatenate(
                [vt[:HEAD_DIM], ones, vt[HEAD_DIM:], ones], axis=0).astype(BF16)
            return carry

        lax.fori_loop(0, nb, body, 0)

    def select(q, own):
        kmean = kmean_ref[...]
        lo_m = _lane(kmean.shape) < HEAD_DIM
        sel_shape = (HEAD_DIM, MOBA_BLOCK)
        blk = lax.broadcasted_iota(jnp.int32, sel_shape, 0)
        blk_f = blk.astype(F32)
        pens = []
        for km in (jnp.where(lo_m, kmean, 0.0), jnp.where(lo_m, 0.0, kmean)):
            gate = _dot_t(km.astype(BF16), q)
            gate = jnp.where(blk < own, gate, -jnp.inf)
            sel = blk == own
            for _ in range(MOBA_TOPK):
                mx = jnp.max(gate, axis=0, keepdims=True)
                first = jnp.min(jnp.where(gate == mx, blk_f, float(HEAD_DIM)), axis=0, keepdims=True)
                pick = (blk_f == first) & (mx > -jnp.inf)
                sel = sel | pick
                gate = jnp.where(pick, -jnp.inf, gate)
            pens.append(jnp.where(sel, 0.0, NEG_INF))
        zeros = jnp.zeros(sel_shape, F32)
        q32 = q.astype(F32)
        return (jnp.where(lo, q32, jnp.concatenate([zeros, pens[0]], axis=0).T).astype(BF16),
                jnp.where(lo, jnp.concatenate([pens[1], zeros], axis=0).T, q32).astype(BF16))

    @pl.when(i == 0)
    def _first_block():
        qa_ref[0, 0], qa_ref[0, 1] = select(q_ref[...], 0)

    ahead0, ahead1 = select(qn_ref[...], i + 1)
    qa0 = qa_ref[i % 2, 0]
    qa1 = qa_ref[i % 2, 1]
    qa_ref[(i + 1) % 2, 0] = ahead0
    qa_ref[(i + 1) % 2, 1] = ahead1

    acc_ref[...] = jnp.zeros_like(acc_ref)
    m_ref[...] = jnp.full_like(m_ref, NEG_INF)
    l_ref[...] = jnp.zeros_like(l_ref)

    n_trips = i // MOBA_UNROLL + 1

    def scores(t, n):
        j = jnp.minimum(t * MOBA_UNROLL + n, i)
        rows = pl.ds(pl.multiple_of(j * MOBA_BLOCK, MOBA_BLOCK), MOBA_BLOCK)
        return [_dot_t(kaug_ref[hd, rows, :], qa) for hd, qa in enumerate((qa0, qa1))]

    def update(j, blk_scores, masked):
        vt = vt_ref[jnp.minimum(j, i)]
        ps, alphas = [], []
        for hd, s in enumerate(blk_scores):
            if masked:
                kk = lax.broadcasted_iota(jnp.int32, s.shape, 0)
                qq = lax.broadcasted_iota(jnp.int32, s.shape, 1)
                s = jnp.where(kk <= qq, s, NEG_INF)
            m_old = m_ref[hd]
            m_new = jnp.maximum(m_old, jnp.max(s, axis=0, keepdims=True))
            alphas.append(jnp.exp2(m_old - m_new))
            ps.append(jnp.exp2(s - m_new).astype(BF16))
            m_ref[hd] = m_new
        for hd in range(2):
            pv = _dot(vt[hd * MOBA_VT_ROWS:(hd + 1) * MOBA_VT_ROWS, :], ps[hd])
            acc_ref[hd] = acc_ref[hd] * alphas[hd] + pv[:HEAD_DIM, :]
            l_ref[hd] = l_ref[hd] * alphas[hd] + pv[HEAD_DIM:HEAD_DIM + 1, :]

    for n in range(MOBA_UNROLL):
        for hd, s in enumerate(scores(0, n)):
            s_ref[n, hd] = s

    def body(t, carry):
        for n in range(MOBA_UNROLL):
            cur = [s_ref[n, hd] for hd in range(2)]
            ahead = scores(t + 1, n)
            update(t * MOBA_UNROLL + n, cur, False)
            for hd in range(2):
                s_ref[n, hd] = ahead[hd]
        return carry

    lax.fori_loop(0, n_trips - 1, body, 0)
    for last in range(MOBA_UNROLL):
        @pl.when(i % MOBA_UNROLL == last)
        def _(last=last):
            for n in range(last + 1):
                update(i - last + n, [s_ref[n, hd] for hd in range(2)], n == last)
    out_t = jnp.concatenate([acc_ref[hd] * (1.0 / l_ref[hd]) for hd in range(2)], axis=0)
    o_ref[...] = out_t.T.astype(o_ref.dtype)


def _moba(qb, kb, vb, batch, seq):
    n = qb.shape[0]
    nb = seq // MOBA_BLOCK
    pairs = WIDTH_B // LANES
    assert nb <= HEAD_DIM, "block one-hot must fit in the other head's lanes"
    qmap = lambda b, p, i: (b * nb + i, p)
    qnext = lambda b, p, i: (b * nb + jnp.minimum(i + 1, nb - 1), p)
    kvmap = lambda b, p, i: (b, p)
    return pl.pallas_call(
        _moba_kernel,
        grid=(batch, pairs, nb),
        in_specs=[
            pl.BlockSpec((MOBA_BLOCK, LANES), qmap),
            pl.BlockSpec((MOBA_BLOCK, LANES), qnext),
            pl.BlockSpec((seq, LANES), kvmap),
            pl.BlockSpec((seq, LANES), kvmap),
        ],
        out_specs=pl.BlockSpec((MOBA_BLOCK, LANES), qmap),
        out_shape=jax.ShapeDtypeStruct((n, WIDTH_B), BF16),
        scratch_shapes=[
            pltpu.VMEM((2, seq, LANES), BF16),
            pltpu.VMEM((nb, 2 * MOBA_VT_ROWS, MOBA_BLOCK), BF16),
            pltpu.VMEM((HEAD_DIM, LANES), F32),
            pltpu.VMEM((2, HEAD_DIM, MOBA_BLOCK), F32),
            pltpu.VMEM((2, 1, MOBA_BLOCK), F32),
            pltpu.VMEM((2, 1, MOBA_BLOCK), F32),
            pltpu.VMEM((MOBA_UNROLL, 2, MOBA_BLOCK, MOBA_BLOCK), F32),
            pltpu.VMEM((2, 2, MOBA_BLOCK, LANES), BF16),
        ],
        compiler_params=_params("parallel", "parallel", "arbitrary"),
        name="moba",
    )(qb, qb, kb, vb)


DIL_CHUNK = DIL_PATTERNS[-1][1] * BAND
DIL_UNROLL = {1: 8, 4: 8, 16: 4}


def _dil_kernel(q_ref, kp_ref, k_ref, vp_ref, v_ref, o_ref, kk_ref, vv_ref, acc_ref, m_ref, l_ref):
    c = pl.program_id(2)
    kk_ref[0:DIL_CHUNK, :] = kp_ref[...]
    kk_ref[DIL_CHUNK:, :] = k_ref[...]
    vv_ref[0:DIL_CHUNK, :] = vp_ref[...]
    vv_ref[DIL_CHUNK:, :] = v_ref[...]
    blocks = DIL_CHUNK // BAND
    shape = (BAND, LANES)

    for pi, (_, d) in enumerate(DIL_PATTERNS):
        per_class = blocks // d

        def load(t, d=d, per_class=per_class):
            r = t // per_class
            j = t % per_class
            q0 = r + BAND * d * j
            k0 = DIL_CHUNK + q0 - BAND * d
            if d == 1:
                qrows = pl.ds(pl.multiple_of(q0, BAND), BAND)
                krows = pl.ds(pl.multiple_of(k0, BAND), 2 * BAND)
            else:
                qrows = pl.ds(q0, BAND, stride=d)
                krows = pl.ds(k0, 2 * BAND, stride=d)
            scores = _pair_dots(q_ref[qrows, :].astype(BF16), kk_ref[krows, :])
            return qrows, krows, (c * per_class + j) * BAND, scores

        unroll = DIL_UNROLL[d]

        def body(t, carry, pi=pi, load=load, unroll=unroll):
            group = [load(t * unroll + u) for u in range(unroll)]
            for qrows, krows, block_start, scores in group:
                mask = _band_mask(DIL_MAX_DIST, block_start)
                (m0, p0, l0), (m1, p1, l1) = _pair_softmax(scores, mask)
                acc_ref[pi, qrows, :] = _pair_pv(p0, p1, vv_ref[krows, :])
                m_ref[pi, qrows, :] = _by_head(m0, m1, shape)
                l_ref[pi, qrows, :] = _by_head(l0, l1, shape)
            return carry

        lax.fori_loop(0, blocks // unroll, body, 0)

    def combine(t, carry):
        rows = pl.ds(pl.multiple_of(t * BAND, BAND), BAND)
        ms = [m_ref[pi, rows, :] for pi in range(len(DIL_PATTERNS))]
        m_max = functools.reduce(jnp.maximum, ms)
        num = jnp.zeros(shape, F32)
        den = jnp.zeros(shape, F32)
        for pi, m in enumerate(ms):
            e = jnp.exp(m - m_max)
            num = num + acc_ref[pi, rows, :] * e
            den = den + l_ref[pi, rows, :] * e
        o_ref[rows, :] = (num / den).astype(o_ref.dtype)
        return carry

    lax.fori_loop(0, blocks, combine, 0)


def _dilated(qc, kc, vc, batch, seq):
    n = qc.shape[0]
    assert seq % DIL_CHUNK == 0
    chunks = seq // DIL_CHUNK
    pairs = WIDTH_C // LANES
    cur = lambda b, p, c: (b * chunks + c, p)
    prev = lambda b, p, c: (b * chunks + jnp.maximum(c - 1, 0), p)
    blk = (DIL_CHUNK, LANES)
    n_pat = len(DIL_PATTERNS)
    return pl.pallas_call(
        _dil_kernel,
        grid=(batch, pairs, chunks),
        in_specs=[pl.BlockSpec(blk, cur), pl.BlockSpec(blk, prev), pl.BlockSpec(blk, cur),
                  pl.BlockSpec(blk, prev), pl.BlockSpec(blk, cur)],
        out_specs=pl.BlockSpec(blk, cur),
        out_shape=jax.ShapeDtypeStruct((n, WIDTH_C), BF16),
        scratch_shapes=[
            pltpu.VMEM((2 * DIL_CHUNK, LANES), F32),
            pltpu.VMEM((2 * DIL_CHUNK, LANES), F32),
            pltpu.VMEM((n_pat, DIL_CHUNK, LANES), F32),
            pltpu.VMEM((n_pat, DIL_CHUNK, LANES), F32),
            pltpu.VMEM((n_pat, DIL_CHUNK, LANES), F32),
        ],
        compiler_params=_params("parallel", "parallel", "parallel"),
        name="dilated",
    )(qc, kc, kc, vc, vc)


def _merged_residual(x_ref, g_ref, oa_ref, ob_ref, oc_ref, wg_ref, wa_ref, wb_ref, wc_ref, wo_ref):
    x = x_ref[...]
    d = x.shape[1]
    h = _rms(x, g_ref[...]).astype(BF16)
    merged = None
    for br, (o_ref, w_ref) in enumerate(((oa_ref, wa_ref), (ob_ref, wb_ref), (oc_ref, wc_ref))):
        gate = _sigmoid(_dot(h, wg_ref[:, br * d:(br + 1) * d]))
        y = gate * _dot(o_ref[...], w_ref[...])
        merged = y if merged is None else merged + y
    return x + _dot(merged.astype(BF16), wo_ref[...])


def _merge_kernel(*refs):
    y_ref = refs[-1]
    y_ref[...] = _merged_residual(*refs[:-1])


def _merge(x2, gain, oa, ob, oc, w_gates, w_a, w_b, w_c, w_o, tm, rider=None):
    n, d = x2.shape
    row = lambda i: (i, 0)
    full = lambda i: (0, 0)
    return _row_call(
        _merge_kernel, "merge", n // tm,
        [
            pl.BlockSpec((tm, d), row),
            pl.BlockSpec((1, d), full),
            pl.BlockSpec((tm, WIDTH_A), row),
            pl.BlockSpec((tm, WIDTH_B), row),
            pl.BlockSpec((tm, WIDTH_C), row),
            pl.BlockSpec(w_gates.shape, full),
            pl.BlockSpec(w_a.shape, full),
            pl.BlockSpec(w_b.shape, full),
            pl.BlockSpec(w_c.shape, full),
            pl.BlockSpec(w_o.shape, full),
        ],
        (x2, gain, oa, ob, oc, w_gates, w_a, w_b, w_c, w_o),
        [pl.BlockSpec((tm, d), row)],
        [jax.ShapeDtypeStruct((n, d), F32)],
        rider)


def _swiglu(h, wg, wu, wd):
    a = _dot(h, wg)
    u = _dot(h, wu)
    return _dot((a * _sigmoid(a) * u).astype(BF16), wd)


def _ffn_kernel(x_ref, g_ref, wg_ref, wu_ref, wd_ref, y_ref):
    x = x_ref[...]
    h = _rms(x, g_ref[...]).astype(BF16)
    y_ref[...] = x + _swiglu(h, wg_ref[...], wu_ref[...], wd_ref[...])


def _ffn(x2, gain, wg, wu, wd, tm, rider=None):
    n, d = x2.shape
    row = lambda i: (i, 0)
    full = lambda i: (0, 0)
    once = pl.Buffered(1)
    return _row_call(
        _ffn_kernel, "ffn", n // tm,
        [
            pl.BlockSpec((tm, d), row),
            pl.BlockSpec((1, d), full),
            pl.BlockSpec(wg.shape, full, pipeline_mode=once),
            pl.BlockSpec(wu.shape, full, pipeline_mode=once),
            pl.BlockSpec(wd.shape, full, pipeline_mode=once),
        ],
        (x2, gain, wg, wu, wd),
        [pl.BlockSpec((tm, d), row)],
        [jax.ShapeDtypeStruct((n, d), F32)],
        rider)


MOE_GROUP = 2048
MOE_PACK_BLOCK = 256
MOE_CHUNK = 128

def _split_bf16(a):
    hi = a.astype(BF16)
    return hi, (a - hi.astype(F32)).astype(BF16)


def _route_tile(x, g_ref, wr_ref, h_ref, rankt_ref, combt_ref, count_ref):
    tm = x.shape[0]
    tile_in_group = pl.program_id(0) % (MOE_GROUP // tm)
    h = _rms(x, g_ref[...])
    h_ref[...] = h.astype(BF16)
    h_hi, h_lo = _split_bf16(h)
    w_hi, w_lo = _split_bf16(wr_ref[...])
    logits = _dot(h_hi, w_hi) + (_dot(h_hi, w_lo) + _dot(h_lo, w_hi))
    lane = _lane(logits.shape)
    logits = jnp.where(lane < N_EXPERTS, logits, -jnp.inf)
    tops = []
    for _ in range(TOP_K):
        mx = jnp.max(logits, axis=-1, keepdims=True)
        first = jnp.min(jnp.where(logits == mx, lane, LANES), axis=-1, keepdims=True)
        tops.append((mx, first))
        logits = jnp.where(lane == first, -jnp.inf, logits)
    (v1, i1), (v2, i2) = tops
    e2 = jnp.exp(v2 - v1)
    w1 = 1.0 / (1.0 + e2)
    w2 = e2 / (1.0 + e2)
    sel = jnp.where(lane == i1, 1.0, 0.0) + jnp.where(lane == i2, 1.0, 0.0)
    comb = jnp.where(lane == i1, w1, 0.0) + jnp.where(lane == i2, w2, 0.0)
    combt_ref[0] = comb.T[:N_EXPERTS, :]

    @pl.when(tile_in_group == 0)
    def _():
        count_ref[...] = jnp.zeros_like(count_ref)

    r = lax.broadcasted_iota(jnp.int32, (tm, tm), 0)
    c = lax.broadcasted_iota(jnp.int32, (tm, tm), 1)
    before = _dot(jnp.where(c < r, 1.0, 0.0).astype(BF16), sel.astype(BF16)) + count_ref[...]
    rank = jnp.where(sel > 0.0, before, -1.0)
    count_ref[...] = count_ref[...] + jnp.sum(sel, axis=0, keepdims=True)
    rankt_ref[0] = rank.T[:N_EXPERTS, :]


def _router_kernel(x_ref, g_ref, wr_ref, *out_and_scratch):
    _route_tile(x_ref[...], g_ref, wr_ref, *out_and_scratch)


def _router(x2, gain, w_router_pad, tm):
    n, d = x2.shape
    assert n % MOE_GROUP == 0 and MOE_GROUP % tm == 0
    per_group = MOE_GROUP // tm
    row = lambda i: (i, 0)
    return pl.pallas_call(
        _router_kernel,
        grid=(n // tm,),
        in_specs=[
            pl.BlockSpec((tm, d), row),
            pl.BlockSpec((1, d), lambda i: (0, 0)),
            pl.BlockSpec((d, LANES), lambda i: (0, 0)),
        ],
        out_specs=[pl.BlockSpec((tm, d), row)]
                  + [pl.BlockSpec((1, N_EXPERTS, tm), lambda i: (i // per_group, 0, i % per_group))] * 2,
        out_shape=[jax.ShapeDtypeStruct((n, d), BF16)]
                  + [jax.ShapeDtypeStruct((n // MOE_GROUP, N_EXPERTS, MOE_GROUP), F32)] * 2,
        scratch_shapes=[pltpu.VMEM((1, LANES), F32)],
        compiler_params=_params("arbitrary"),
        name="router",
    )(x2, gain, w_router_pad)


def _experts_kernel(h_ref, rankt_ref, combt_ref, wg_ref, wu_ref, wd_ref, y_ref,
                    xs_ref, acc_ref, col_ref, start_ref):
    e = pl.program_id(1)
    f = pl.program_id(2)
    group, d = h_ref.shape
    n_blocks = group // MOE_CHUNK
    per_pack = MOE_PACK_BLOCK // MOE_CHUNK
    blocks = [slice(b * MOE_CHUNK, (b + 1) * MOE_CHUNK) for b in range(n_blocks)]
    pack_blocks = [slice(b * MOE_PACK_BLOCK, (b + 1) * MOE_PACK_BLOCK) for b in range(group // MOE_PACK_BLOCK)]

    def chunk_rows(m, n=1):
        return pl.ds(pl.multiple_of(m * MOE_CHUNK, MOE_CHUNK), n * MOE_CHUNK)

    def n_chunks():
        return (start_ref[n_blocks] + MOE_CHUNK - 1) // MOE_CHUNK

    def overlaps(m, b):
        return ((start_ref[b * per_pack] < (m + 1) * MOE_CHUNK)
                & (start_ref[(b + 1) * per_pack] > m * MOE_CHUNK))

    @pl.when((e == 0) & (f == 0))
    def _():
        y_ref[...] = jnp.zeros_like(y_ref)

    @pl.when(f == 0)
    def _dispatch():
        rank_row = rankt_ref[0, pl.ds(e, 1), :]
        sub = lax.broadcasted_iota(jnp.int32, (LANES, LANES), 0)
        for which, row in enumerate((rank_row, combt_ref[0, pl.ds(e, 1), :])):
            stacked = jnp.zeros((LANES, LANES), F32)
            for b, tokens in enumerate(blocks):
                stacked = jnp.where(sub == b, row[:, tokens], stacked)
            col_ref[which] = stacked.T
        start_ref[0] = 0
        for b in range(n_blocks):
            last = jnp.max(rank_row[:, blocks[b]]).astype(jnp.int32) + 1
            start_ref[b + 1] = jnp.maximum(start_ref[b], last)

        def pack(m, carry):
            dest = (lax.broadcasted_iota(jnp.int32, (MOE_CHUNK, MOE_PACK_BLOCK), 0)
                    + m * MOE_CHUNK).astype(F32)
            xs_ref[chunk_rows(m), :] = jnp.zeros((MOE_CHUNK, d), BF16)
            acc_ref[chunk_rows(m), :] = jnp.zeros((MOE_CHUNK, d), F32)
            for b, tokens in enumerate(pack_blocks):
                @pl.when(overlaps(m, b))
                def _():
                    onehot = jnp.where(rank_row[:, tokens] == dest, 1.0, 0.0).astype(BF16)
                    rows = _dot(onehot, h_ref[tokens, :])
                    xs_ref[chunk_rows(m), :] = (xs_ref[chunk_rows(m), :].astype(F32) + rows).astype(BF16)
            return carry

        lax.fori_loop(0, n_chunks(), pack, 0)
        xs_ref[chunk_rows(n_chunks(), 2), :] = jnp.zeros((2 * MOE_CHUNK, d), BF16)
        acc_ref[chunk_rows(n_chunks(), 2), :] = jnp.zeros((2 * MOE_CHUNK, d), F32)

    last_slice = f == pl.num_programs(2) - 1

    @pl.when(jnp.logical_not(last_slice))
    def _():
        def expert(m, carry):
            xm = xs_ref[chunk_rows(m), :]
            acc_ref[chunk_rows(m), :] += _swiglu(xm, wg_ref[0], wu_ref[0], wd_ref[0])
            return carry

        lax.fori_loop(0, n_chunks(), expert, 0)

    @pl.when(last_slice)
    def _combine():
        def expert(m, carry):
            xm = xs_ref[chunk_rows(m), :]
            y = acc_ref[chunk_rows(m), :] + _swiglu(xm, wg_ref[0], wu_ref[0], wd_ref[0])
            y_hi, y_lo = _split_bf16(y)
            xs_ref[chunk_rows(m), :] = y_hi
            acc_ref[chunk_rows(m), :] = y_lo.astype(F32)
            return carry

        lax.fori_loop(0, n_chunks(), expert, 0)
        for b, tokens in enumerate(blocks):
            first = start_ref[b] // MOE_CHUNK
            y_hi = xs_ref[chunk_rows(first, 2), :]
            y_lo = acc_ref[chunk_rows(first, 2), :].astype(BF16)
            dest = (_lane((MOE_CHUNK, 2 * MOE_CHUNK)) + first * MOE_CHUNK).astype(F32)
            onehot = jnp.where(col_ref[0, :, b:b + 1] == dest, 1.0, 0.0).astype(BF16)
            y_ref[tokens, :] += col_ref[1, :, b:b + 1] * (_dot(onehot, y_hi) + _dot(onehot, y_lo))


def _experts(hb, rank_t, comb_t, wg, wu, wd, ff_split):
    n, d = hb.shape
    n_e, _, d_ff = wg.shape
    ff = d_ff // ff_split
    assert MOE_GROUP // MOE_CHUNK <= LANES
    group = lambda s, e, f: (s, 0)
    once = pl.Buffered(1)
    return pl.pallas_call(
        _experts_kernel,
        grid=(n // MOE_GROUP, n_e, ff_split),
        in_specs=[
            pl.BlockSpec((MOE_GROUP, d), group, pipeline_mode=once),
            pl.BlockSpec((1, N_EXPERTS, MOE_GROUP), lambda s, e, f: (s, 0, 0)),
            pl.BlockSpec((1, N_EXPERTS, MOE_GROUP), lambda s, e, f: (s, 0, 0)),
            pl.BlockSpec((1, d, ff), lambda s, e, f: (e, 0, f)),
            pl.BlockSpec((1, d, ff), lambda s, e, f: (e, 0, f)),
            pl.BlockSpec((1, ff, d), lambda s, e, f: (e, f, 0)),
        ],
        out_specs=pl.BlockSpec((MOE_GROUP, d), group, pipeline_mode=once),
        out_shape=jax.ShapeDtypeStruct((n, d), F32),
        scratch_shapes=[
            pltpu.VMEM((MOE_GROUP + 2 * MOE_CHUNK, d), BF16),
            pltpu.VMEM((MOE_GROUP + 2 * MOE_CHUNK, d), F32),
            pltpu.VMEM((2, LANES, LANES), F32),
            pltpu.SMEM((MOE_GROUP // MOE_CHUNK + 1,), jnp.int32),
        ],
        compiler_params=_params("parallel", "arbitrary", "arbitrary"),
        name="experts",
    )(hb, rank_t, comb_t, wg, wu, wd)


def _add_kernel(x_ref, y_ref, o_ref):
    o_ref[...] = x_ref[...] + y_ref[...]


def _add_norm_kernel(x_ref, y_ref, g_ref, o_ref):
    o_ref[...] = _rms(x_ref[...] + y_ref[...], g_ref[...])


def _norm_kernel(x_ref, g_ref, o_ref):
    o_ref[...] = _rms(x_ref[...], g_ref[...])


def _rowwise(body, name, tm, x2, *rest):
    n, d = x2.shape
    row = pl.BlockSpec((tm, d), lambda i: (i, 0))
    specs = [row if a.shape == (n, d) else pl.BlockSpec((1, d), lambda i: (0, 0)) for a in (x2,) + rest]
    return pl.pallas_call(
        body,
        grid=(n // tm,),
        in_specs=specs,
        out_specs=row,
        out_shape=jax.ShapeDtypeStruct((n, d), F32),
        compiler_params=_params("parallel"),
        name=name,
    )(x2, *rest)


def _rope_tables(seq):
    pos = jnp.arange(seq, dtype=F32)
    inv_freq = ROPE_THETA ** (-jnp.arange(0, HEAD_DIM, 2, dtype=F32) / HEAD_DIM)
    ang = pos[:, None] * inv_freq[None, :]
    cos, sin = jnp.cos(ang), jnp.sin(ang)
    reps = LANES // HEAD_DIM
    cos_t = jnp.tile(jnp.concatenate([cos, cos], axis=-1), (1, reps))
    sin_t = jnp.tile(jnp.concatenate([-sin, sin], axis=-1), (1, reps))
    return cos_t, sin_t


def _pair_heads_by_group(w, axis):
    shape = w.shape
    split = shape[:axis] + (SWA_KV_HEADS, SWA_GROUP, HEAD_DIM) + shape[axis + 1:]
    return jnp.swapaxes(w.reshape(split), axis, axis + 1).reshape(shape)


def _row_tile(n, want):
    t = min(n, want)
    assert n % t == 0
    return t


def kernel(x, norm_mix, w_in, attn_sinks, w_br_a, w_br_b, w_br_c, w_out, norm_ffn, w_ff_gate, w_ff_up, w_ff_down, w_router, w_moe_gate, w_moe_up, w_moe_down, norm_final):
    batch, seq, d = x.shape
    n = batch * seq
    depth = norm_mix.shape[0]
    tm = _row_tile(seq, 512)
    cos_t, sin_t = _rope_tables(seq)
    x2 = x.reshape(n, d).astype(F32)
    expert_bf16 = {}

    for layer in range(depth):
        w_l = w_in[layer]
        w_qa = _pair_heads_by_group(w_l[:, :WIDTH_A], 1)
        w_qkv = jnp.concatenate([w_qa, w_l[:, WIDTH_A:QKV_COLS]], axis=1).astype(BF16)
        w_gates = w_l[:, QKV_COLS:].astype(BF16)
        gain = norm_mix[layer].reshape(1, d)

        idx = layer // 2
        is_moe = layer % 2 == 1
        feeds_moe = not is_moe and layer + 1 < depth
        nxt = (layer + 1) // 2
        flat = lambda w: w.reshape(-1, w.shape[-1])

        qa, ka, va, qb, kb, vb, qc, kc, vc = _inproj(x2, gain, w_qkv, cos_t, sin_t, seq, tm)
        oa = _swa(attn_sinks[layer], qa, ka, va, batch, seq, tm)
        ob = _moba(qb, kb, vb, batch, seq)
        oc = _dilated(qc, kc, vc, batch, seq)
        gain_f = norm_ffn[layer].reshape(1, d)
        rider = flat(w_moe_down[idx]) if is_moe else flat(w_moe_up[nxt]) if feeds_moe else None
        x2, *cast = _merge(x2, gain, oa, ob, oc, w_gates,
                           _pair_heads_by_group(w_br_a[layer], 0).astype(BF16),
                           w_br_b[layer].astype(BF16), w_br_c[layer].astype(BF16),
                           w_out[layer].astype(BF16), tm, rider)
        if cast:
            expert_bf16["down" if is_moe else "up"] = cast[0]

        if not is_moe:
            x2, *cast = _ffn(x2, gain_f, w_ff_gate[idx].astype(BF16), w_ff_up[idx].astype(BF16),
                             w_ff_down[idx].astype(BF16), tm,
                             flat(w_moe_gate[nxt]) if feeds_moe else None)
            if cast:
                expert_bf16["gate"] = cast[0]
        else:
            w_r = jnp.pad(w_router[idx], ((0, 0), (0, LANES - N_EXPERTS)))
            hb, rank_t, comb_t = _router(x2, gain_f, w_r, tm)
            weights = [expert_bf16.pop(k).reshape(w.shape) if k in expert_bf16 else w.astype(BF16)
                       for k, w in (("gate", w_moe_gate[idx]), ("up", w_moe_up[idx]),
                                    ("down", w_moe_down[idx]))]
            y = _experts(hb, rank_t, comb_t, *weights, 2)
            if layer == depth - 1:
                out = _rowwise(_add_norm_kernel, "add_norm", tm, x2, y, norm_final.reshape(1, d))
                return out.reshape(batch, seq, d)
            x2 = _rowwise(_add_kernel, "add", tm, x2, y)

    return _rowwise(_norm_kernel, "final_norm", tm, x2, norm_final.reshape(1, d)).reshape(batch, seq, d)
```

```python
import functools

import jax
import jax.numpy as jnp
import numpy as np
from jax import lax
from jax.experimental import pallas as pl
from jax.experimental.pallas import tpu as pltpu

F32 = jnp.float32
BF16 = jnp.bfloat16

HEAD_DIM = 64
ATTN_SCALE = HEAD_DIM ** -0.5
LOG2_E = 1.4426950408889634
ROPE_THETA = 10000.0
RMS_EPS = 1e-5
NEG_INF = -1e30
BAND = 128
SWA_Q_HEADS, SWA_KV_HEADS = 6, 2
SWA_GROUP = SWA_Q_HEADS // SWA_KV_HEADS
SWA_MAX_DIST = 127
MOBA_HEADS, MOBA_BLOCK, MOBA_TOPK = 4, 256, 3
DIL_HEADS = 6
DIL_PATTERNS = ((128, 1), (512, 4), (2048, 16))
DIL_MAX_DIST = 128
N_EXPERTS, TOP_K = 8, 2

LANES = 128
VMEM_LIMIT = 56 * 1024 * 1024

WIDTH_A = SWA_Q_HEADS * HEAD_DIM
KV_A = SWA_KV_HEADS * HEAD_DIM
WIDTH_B = MOBA_HEADS * HEAD_DIM
WIDTH_C = DIL_HEADS * HEAD_DIM
QKV_COLS = WIDTH_A + 2 * KV_A + 3 * WIDTH_B + 3 * WIDTH_C


def _params(*sem):
    return pltpu.CompilerParams(dimension_semantics=sem, vmem_limit_bytes=VMEM_LIMIT)


def _row_call(body, name, steps, in_specs, args, out_specs, out_shapes, rider=None):
    in_specs, out_specs, out_shapes, args = list(in_specs), list(out_specs), list(out_shapes), list(args)
    if rider is not None:
        rows, cols = rider.shape
        assert rows % steps == 0
        n_in, n_out, inner = len(in_specs), len(out_specs), body
        slab = pl.BlockSpec((rows // steps, cols), lambda i: (i, 0))

        def body(*refs):
            inner(*refs[:n_in], *refs[n_in + 1:n_in + 1 + n_out])
            refs[-1][...] = refs[n_in][...].astype(BF16)

        in_specs.append(slab)
        args.append(rider)
        out_specs.append(slab)
        out_shapes.append(jax.ShapeDtypeStruct((rows, cols), BF16))
    return pl.pallas_call(
        body, grid=(steps,), in_specs=in_specs, out_specs=out_specs, out_shape=out_shapes,
        compiler_params=_params("parallel"), name=name,
    )(*args)


def _rms(x, gain):
    ms = jnp.mean(x * x, axis=-1, keepdims=True)
    return x * lax.rsqrt(ms + RMS_EPS) * gain


def _dot(a, b):
    return jnp.dot(a, b, preferred_element_type=F32)


def _dot_t(a, b):
    return lax.dot_general(a, b, (((1,), (1,)), ((), ())), preferred_element_type=F32)


def _sigmoid(z):
    return 1.0 / (1.0 + jnp.exp(-z))


def _lane(shape):
    return lax.broadcasted_iota(jnp.int32, shape, len(shape) - 1)


def _inproj_kernel(x_ref, g_ref, w_ref, cos_ref, sin_ref,
                   qa_ref, ka_ref, va_ref, qb_ref, kb_ref, vb_ref, qc_ref, kc_ref, vc_ref):
    tm = x_ref.shape[0]
    h = _rms(x_ref[...], g_ref[...]).astype(BF16)
    cos = cos_ref[...]
    sin = sin_ref[...]
    first_half = (_lane((tm, LANES)) & (HEAD_DIM // 2)) == 0

    def rope(z):
        rot = jnp.where(first_half,
                        pltpu.roll(z, LANES - HEAD_DIM // 2, 1),
                        pltpu.roll(z, HEAD_DIM // 2, 1))
        return z * cos + rot * sin

    z_all = _dot(h, w_ref[...])

    def proj(c0, width):
        return z_all[:, c0:c0 + width]

    def store(ref, z, roped, scale, dtype):
        for t in range(z.shape[1] // LANES):
            zt = z[:, t * LANES:(t + 1) * LANES]
            if roped:
                zt = rope(zt)
            if scale is not None:
                zt = zt * scale
            ref[:, t * LANES:(t + 1) * LANES] = zt.astype(dtype)

    c = 0
    store(qa_ref, proj(c, WIDTH_A), True, ATTN_SCALE, BF16); c += WIDTH_A
    kv = proj(c, 2 * KV_A); c += 2 * KV_A
    store(ka_ref, kv[:, :KV_A], True, None, BF16)
    store(va_ref, kv[:, KV_A:], False, None, BF16)
    store(qb_ref, proj(c, WIDTH_B), True, ATTN_SCALE * LOG2_E, BF16); c += WIDTH_B
    store(kb_ref, proj(c, WIDTH_B), True, None, BF16); c += WIDTH_B
    store(vb_ref, proj(c, WIDTH_B), False, None, BF16); c += WIDTH_B
    store(qc_ref, proj(c, WIDTH_C), True, ATTN_SCALE, F32); c += WIDTH_C
    store(kc_ref, proj(c, WIDTH_C), True, None, F32); c += WIDTH_C
    store(vc_ref, proj(c, WIDTH_C), False, None, F32)


def _inproj(x2, gain, w_qkv, cos_t, sin_t, seq, tm, rider=None):
    n, d = x2.shape
    tiles_per_seq = seq // tm
    row = lambda i: (i, 0)
    widths = (WIDTH_A, KV_A, KV_A, WIDTH_B, WIDTH_B, WIDTH_B, WIDTH_C, WIDTH_C, WIDTH_C)
    dtypes = (BF16,) * 6 + (F32,) * 3
    return _row_call(
        _inproj_kernel, "inproj", n // tm,
        [
            pl.BlockSpec((tm, d), row),
            pl.BlockSpec((1, d), lambda i: (0, 0)),
            pl.BlockSpec((d, QKV_COLS), lambda i: (0, 0)),
            pl.BlockSpec((tm, LANES), lambda i: (i % tiles_per_seq, 0)),
            pl.BlockSpec((tm, LANES), lambda i: (i % tiles_per_seq, 0)),
        ],
        (x2, gain, w_qkv, cos_t, sin_t),
        [pl.BlockSpec((tm, w), row) for w in widths],
        [jax.ShapeDtypeStruct((n, w), dt) for w, dt in zip(widths, dtypes)],
        rider)


def _band_mask(max_dist, block_start, stack=1):
    qi = lax.broadcasted_iota(jnp.int32, (stack * BAND, 2 * BAND), 0) % BAND
    kj = lax.broadcasted_iota(jnp.int32, (stack * BAND, 2 * BAND), 1) - BAND
    diff = qi - kj
    return (diff >= 0) & (diff <= max_dist) & (kj + block_start >= 0)


def _split_heads(a):
    a32 = a.astype(F32)
    lo = _lane(a.shape) < HEAD_DIM
    return jnp.where(lo, a32, 0.0).astype(BF16), jnp.where(lo, 0.0, a32).astype(BF16)


def _pair_dots(q, k):
    return [_dot_t(q, kh) for kh in _split_heads(k)]


def _pair_softmax(scores, mask):
    out = []
    for s in scores:
        s = jnp.where(mask, s, NEG_INF)
        m = jnp.max(s, axis=-1, keepdims=True)
        p = jnp.exp(s - m)
        out.append((m, p, jnp.sum(p, axis=-1, keepdims=True)))
    return out


def _pair_pv(p0, p1, v):
    pc = jnp.concatenate([p0.astype(BF16), p1.astype(BF16)], axis=1)
    return _dot(pc, jnp.concatenate(_split_heads(v), axis=0))


def _by_head(lo_val, hi_val, shape):
    return jnp.where(_lane(shape) < HEAD_DIM, lo_val, hi_val)


def _swa_kernel(sink_ref, q_ref, k_ref, kp_ref, v_ref, vp_ref, o_ref):
    nblk = q_ref.shape[0] // BAND
    chunk_start = pl.program_id(1) * q_ref.shape[0]
    n_pairs = WIDTH_A // LANES
    def load(j):
        rows = slice(j * BAND, (j + 1) * BAND)
        prev = slice((j - 1) * BAND, j * BAND)
        q = q_ref[rows, :]
        qs = jnp.concatenate([q[:, t * LANES:(t + 1) * LANES] for t in range(n_pairs)], axis=0)
        if j == 0:
            k = jnp.concatenate([kp_ref[...], k_ref[rows, :]], axis=0)
            v = jnp.concatenate([vp_ref[...], v_ref[rows, :]], axis=0)
        else:
            k = jnp.concatenate([k_ref[prev, :], k_ref[rows, :]], axis=0)
            v = jnp.concatenate([v_ref[prev, :], v_ref[rows, :]], axis=0)
        return rows, v, _pair_dots(qs, k)

    blocks = [load(j) for j in range(nblk)]
    for j, (rows, v, scores) in enumerate(blocks):
        mask = _band_mask(SWA_MAX_DIST, chunk_start + j * BAND, n_pairs)
        stats = _pair_softmax(scores, mask)
        acc = _pair_pv(stats[0][1], stats[1][1], v)
        scales = []
        for g, (m, _, l) in enumerate(stats):
            sink = jnp.concatenate(
                [jnp.full((BAND, 1), sink_ref[SWA_GROUP * g + t], F32) for t in range(n_pairs)],
                axis=0)
            m_all = jnp.maximum(m, sink)
            keep = jnp.exp(m - m_all)
            den = l * keep + jnp.exp(sink - m_all)
            scales.append(keep / den)
        out = acc * _by_head(scales[0], scales[1], acc.shape)
        for t in range(n_pairs):
            o_ref[rows, t * LANES:(t + 1) * LANES] = out[t * BAND:(t + 1) * BAND, :].astype(o_ref.dtype)


def _swa(sinks, qa, ka, va, batch, seq, tq):
    n = qa.shape[0]
    chunks = seq // tq
    blk = tq // BAND
    cur = lambda b, c: (b * chunks + c, 0)
    prev = lambda b, c: (jnp.maximum((b * chunks + c) * blk - 1, 0), 0)
    return pl.pallas_call(
        _swa_kernel,
        grid=(batch, chunks),
        in_specs=[
            pl.BlockSpec(memory_space=pltpu.SMEM),
            pl.BlockSpec((tq, WIDTH_A), cur),
            pl.BlockSpec((tq, KV_A), cur),
            pl.BlockSpec((BAND, KV_A), prev),
            pl.BlockSpec((tq, KV_A), cur),
            pl.BlockSpec((BAND, KV_A), prev),
        ],
        out_specs=pl.BlockSpec((tq, WIDTH_A), cur),
        out_shape=jax.ShapeDtypeStruct((n, WIDTH_A), BF16),
        compiler_params=_params("parallel", "parallel"),
        name="swa",
    )(sinks, qa, ka, ka, va, va)


MOBA_UNROLL = 4
MOBA_VT_ROWS = HEAD_DIM + 16


def _moba_kernel(q_ref, qn_ref, k_ref, v_ref, o_ref,
                 kaug_ref, vt_ref, kmean_ref, acc_ref, m_ref, l_ref, s_ref, qa_ref):
    i = pl.program_id(1)
    nb = k_ref.shape[0] // MOBA_BLOCK
    n_pairs = q_ref.shape[1] // LANES
    heads = range(2 * n_pairs)
    pair_cols = [slice(p * LANES, (p + 1) * LANES) for p in range(n_pairs)]
    blk_shape = (MOBA_BLOCK, LANES)
    lane = _lane(blk_shape)
    lo = lane < HEAD_DIM

    @pl.when(i == 0)
    def _prepare():
        kmean_ref[...] = jnp.zeros_like(kmean_ref)

        def body(n, carry):
            rows = pl.ds(pl.multiple_of(n * MOBA_BLOCK, MOBA_BLOCK), MOBA_BLOCK)
            block_hot = jnp.where((lane == n) | (lane == HEAD_DIM + n), 1.0, 0.0)
            ones = jnp.ones((MOBA_VT_ROWS - HEAD_DIM, MOBA_BLOCK), F32)
            for p, cols in enumerate(pair_cols):
                kb = k_ref[rows, cols].astype(F32)
                kmean_ref[p, pl.ds(n, 1), :] = jnp.sum(kb, axis=0, keepdims=True) * (1.0 / MOBA_BLOCK)
                kaug_ref[2 * p, rows, :] = jnp.where(lo, kb, block_hot).astype(BF16)
                kaug_ref[2 * p + 1, rows, :] = jnp.where(lo, block_hot, kb).astype(BF16)
                vt = v_ref[rows, cols].astype(F32).T
                vt_ref[n, 2 * p * MOBA_VT_ROWS:(2 * p + 2) * MOBA_VT_ROWS, :] = jnp.concatenate(
                    [vt[:HEAD_DIM], ones, vt[HEAD_DIM:], ones], axis=0).astype(BF16)
            return carry

        lax.fori_loop(0, nb, body, 0)

    def select(q, own, p):
        kmean = kmean_ref[p]
        lo_m = _lane(kmean.shape) < HEAD_DIM
        sel_shape = (HEAD_DIM, MOBA_BLOCK)
        blk = lax.broadcasted_iota(jnp.int32, sel_shape, 0)
        blk_f = blk.astype(F32)
        pens = []
        for km in (jnp.where(lo_m, kmean, 0.0), jnp.where(lo_m, 0.0, kmean)):
            gate = _dot_t(km.astype(BF16), q)
            gate = jnp.where(blk < own, gate, -jnp.inf)
            sel = blk == own
            for _ in range(MOBA_TOPK):
                mx = jnp.max(gate, axis=0, keepdims=True)
                first = jnp.min(jnp.where(gate == mx, blk_f, float(HEAD_DIM)), axis=0, keepdims=True)
                pick = (blk_f == first) & (mx > -jnp.inf)
                sel = sel | pick
                gate = jnp.where(pick, -jnp.inf, gate)
            pens.append(jnp.where(sel, 0.0, NEG_INF))
        zeros = jnp.zeros(sel_shape, F32)
        q32 = q.astype(F32)
        return (jnp.where(lo, q32, jnp.concatenate([zeros, pens[0]], axis=0).T).astype(BF16),
                jnp.where(lo, jnp.concatenate([pens[1], zeros], axis=0).T, q32).astype(BF16))

    @pl.when(i == 0)
    def _first_block():
        for p, cols in enumerate(pair_cols):
            qa_ref[0, 2 * p], qa_ref[0, 2 * p + 1] = select(q_ref[:, cols], 0, p)

    ahead = [qa for p, cols in enumerate(pair_cols) for qa in select(qn_ref[:, cols], i + 1, p)]
    qas = [qa_ref[i % 2, hd] for hd in heads]
    for hd in heads:
        qa_ref[(i + 1) % 2, hd] = ahead[hd]

    acc_ref[...] = jnp.zeros_like(acc_ref)
    m_ref[...] = jnp.full_like(m_ref, NEG_INF)
    l_ref[...] = jnp.zeros_like(l_ref)

    n_trips = i // MOBA_UNROLL + 1

    def scores(t, n):
        j = jnp.minimum(t * MOBA_UNROLL + n, i)
        rows = pl.ds(pl.multiple_of(j * MOBA_BLOCK, MOBA_BLOCK), MOBA_BLOCK)
        return [_dot_t(kaug_ref[hd, rows, :], qas[hd]) for hd in heads]

    def update(j, blk_scores, masked):
        vt = vt_ref[jnp.minimum(j, i)]
        ps, alphas = [], []
        for hd, s in enumerate(blk_scores):
            if masked:
                kk = lax.broadcasted_iota(jnp.int32, s.shape, 0)
                qq = lax.broadcasted_iota(jnp.int32, s.shape, 1)
                s = jnp.where(kk <= qq, s, NEG_INF)
            m_old = m_ref[hd]
            m_new = jnp.maximum(m_old, jnp.max(s, axis=0, keepdims=True))
            alphas.append(jnp.exp2(m_old - m_new))
            ps.append(jnp.exp2(s - m_new).astype(BF16))
            m_ref[hd] = m_new
        for hd in heads:
            pv = _dot(vt[hd * MOBA_VT_ROWS:(hd + 1) * MOBA_VT_ROWS, :], ps[hd])
            acc_ref[hd] = acc_ref[hd] * alphas[hd] + pv[:HEAD_DIM, :]
            l_ref[hd] = l_ref[hd] * alphas[hd] + pv[HEAD_DIM:HEAD_DIM + 1, :]

    for n in range(MOBA_UNROLL):
        for hd, s in enumerate(scores(0, n)):
            s_ref[n, hd] = s

    def body(t, carry):
        for n in range(MOBA_UNROLL):
            cur = [s_ref[n, hd] for hd in heads]
            nxt = scores(t + 1, n)
            update(t * MOBA_UNROLL + n, cur, False)
            for hd in heads:
                s_ref[n, hd] = nxt[hd]
        return carry

    lax.fori_loop(0, n_trips - 1, body, 0)
    for last in range(MOBA_UNROLL):
        @pl.when(i % MOBA_UNROLL == last)
        def _(last=last):
            for n in range(last + 1):
                update(i - last + n, [s_ref[n, hd] for hd in heads], n == last)
    for p, cols in enumerate(pair_cols):
        out_t = jnp.concatenate(
            [acc_ref[hd] * (1.0 / l_ref[hd]) for hd in (2 * p, 2 * p + 1)], axis=0)
        o_ref[:, cols] = out_t.T.astype(o_ref.dtype)


def _moba(qb, kb, vb, batch, seq):
    n = qb.shape[0]
    nb = seq // MOBA_BLOCK
    assert nb <= HEAD_DIM, "block one-hot must fit in the other head's lanes"
    qmap = lambda b, i: (b * nb + i, 0)
    qnext = lambda b, i: (b * nb + jnp.minimum(i + 1, nb - 1), 0)
    kvmap = lambda b, i: (b, 0)
    return pl.pallas_call(
        _moba_kernel,
        grid=(batch, nb),
        in_specs=[
            pl.BlockSpec((MOBA_BLOCK, WIDTH_B), qmap),
            pl.BlockSpec((MOBA_BLOCK, WIDTH_B), qnext),
            pl.BlockSpec((seq, WIDTH_B), kvmap),
            pl.BlockSpec((seq, WIDTH_B), kvmap),
        ],
        out_specs=pl.BlockSpec((MOBA_BLOCK, WIDTH_B), qmap),
        out_shape=jax.ShapeDtypeStruct((n, WIDTH_B), BF16),
        scratch_shapes=[
            pltpu.VMEM((MOBA_HEADS, seq, LANES), BF16),
            pltpu.VMEM((nb, MOBA_HEADS * MOBA_VT_ROWS, MOBA_BLOCK), BF16),
            pltpu.VMEM((MOBA_HEADS // 2, HEAD_DIM, LANES), F32),
            pltpu.VMEM((MOBA_HEADS, HEAD_DIM, MOBA_BLOCK), F32),
            pltpu.VMEM((MOBA_HEADS, 1, MOBA_BLOCK), F32),
            pltpu.VMEM((MOBA_HEADS, 1, MOBA_BLOCK), F32),
            pltpu.VMEM((MOBA_UNROLL, MOBA_HEADS, MOBA_BLOCK, MOBA_BLOCK), F32),
            pltpu.VMEM((2, MOBA_HEADS, MOBA_BLOCK, LANES), BF16),
        ],
        compiler_params=_params("parallel", "arbitrary"),
        name="moba",
    )(qb, qb, kb, vb)


DIL_CHUNK = DIL_PATTERNS[-1][1] * BAND
DIL_UNROLL = {1: 8, 4: 8, 16: 4}


def _dil_kernel(q_ref, kp_ref, k_ref, vp_ref, v_ref, o_ref, kk_ref, vv_ref, acc_ref, m_ref, l_ref):
    c = pl.program_id(2)
    kk_ref[0:DIL_CHUNK, :] = kp_ref[...]
    kk_ref[DIL_CHUNK:, :] = k_ref[...]
    vv_ref[0:DIL_CHUNK, :] = vp_ref[...]
    vv_ref[DIL_CHUNK:, :] = v_ref[...]
    blocks = DIL_CHUNK // BAND
    shape = (BAND, LANES)

    for pi, (_, d) in enumerate(DIL_PATTERNS):
        per_class = blocks // d

        def load(t, d=d, per_class=per_class):
            r = t // per_class
            j = t % per_class
            q0 = r + BAND * d * j
            k0 = DIL_CHUNK + q0 - BAND * d
            if d == 1:
                qrows = pl.ds(pl.multiple_of(q0, BAND), BAND)
                krows = pl.ds(pl.multiple_of(k0, BAND), 2 * BAND)
            else:
                qrows = pl.ds(q0, BAND, stride=d)
                krows = pl.ds(k0, 2 * BAND, stride=d)
            scores = _pair_dots(q_ref[qrows, :].astype(BF16), kk_ref[krows, :])
            return qrows, krows, (c * per_class + j) * BAND, scores

        unroll = DIL_UNROLL[d]

        def body(t, carry, pi=pi, load=load, unroll=unroll):
            group = [load(t * unroll + u) for u in range(unroll)]
            for qrows, krows, block_start, scores in group:
                mask = _band_mask(DIL_MAX_DIST, block_start)
                (m0, p0, l0), (m1, p1, l1) = _pair_softmax(scores, mask)
                acc_ref[pi, qrows, :] = _pair_pv(p0, p1, vv_ref[krows, :])
                m_ref[pi, qrows, :] = _by_head(m0, m1, shape)
                l_ref[pi, qrows, :] = _by_head(l0, l1, shape)
            return carry

        lax.fori_loop(0, blocks // unroll, body, 0)

    def combine(t, carry):
        rows = pl.ds(pl.multiple_of(t * BAND, BAND), BAND)
        ms = [m_ref[pi, rows, :] for pi in range(len(DIL_PATTERNS))]
        m_max = functools.reduce(jnp.maximum, ms)
        num = jnp.zeros(shape, F32)
        den = jnp.zeros(shape, F32)
        for pi, m in enumerate(ms):
            e = jnp.exp(m - m_max)
            num = num + acc_ref[pi, rows, :] * e
            den = den + l_ref[pi, rows, :] * e
        o_ref[rows, :] = (num / den).astype(o_ref.dtype)
        return carry

    lax.fori_loop(0, blocks, combine, 0)


def _dilated(qc, kc, vc, batch, seq):
    n = qc.shape[0]
    assert seq % DIL_CHUNK == 0
    chunks = seq // DIL_CHUNK
    pairs = WIDTH_C // LANES
    cur = lambda b, p, c: (b * chunks + c, p)
    prev = lambda b, p, c: (b * chunks + jnp.maximum(c - 1, 0), p)
    blk = (DIL_CHUNK, LANES)
    n_pat = len(DIL_PATTERNS)
    return pl.pallas_call(
        _dil_kernel,
        grid=(batch, pairs, chunks),
        in_specs=[pl.BlockSpec(blk, cur), pl.BlockSpec(blk, prev), pl.BlockSpec(blk, cur),
                  pl.BlockSpec(blk, prev), pl.BlockSpec(blk, cur)],
        out_specs=pl.BlockSpec(blk, cur),
        out_shape=jax.ShapeDtypeStruct((n, WIDTH_C), BF16),
        scratch_shapes=[
            pltpu.VMEM((2 * DIL_CHUNK, LANES), F32),
            pltpu.VMEM((2 * DIL_CHUNK, LANES), F32),
            pltpu.VMEM((n_pat, DIL_CHUNK, LANES), F32),
            pltpu.VMEM((n_pat, DIL_CHUNK, LANES), F32),
            pltpu.VMEM((n_pat, DIL_CHUNK, LANES), F32),
        ],
        compiler_params=_params("parallel", "parallel", "parallel"),
        name="dilated",
    )(qc, kc, kc, vc, vc)


def _merged_residual(x_ref, g_ref, oa_ref, ob_ref, oc_ref, wg_ref, wa_ref, wb_ref, wc_ref, wo_ref):
    x = x_ref[...]
    d = x.shape[1]
    h = _rms(x, g_ref[...]).astype(BF16)
    merged = None
    for br, (o_ref, w_ref) in enumerate(((oa_ref, wa_ref), (ob_ref, wb_ref), (oc_ref, wc_ref))):
        gate = _sigmoid(_dot(h, wg_ref[:, br * d:(br + 1) * d]))
        y = gate * _dot(o_ref[...], w_ref[...])
        merged = y if merged is None else merged + y
    return x + _dot(merged.astype(BF16), wo_ref[...])


def _merge_kernel(*refs):
    y_ref = refs[-1]
    y_ref[...] = _merged_residual(*refs[:-1])


def _merge(x2, gain, oa, ob, oc, w_gates, w_a, w_b, w_c, w_o, tm, rider=None):
    n, d = x2.shape
    row = lambda i: (i, 0)
    full = lambda i: (0, 0)
    return _row_call(
        _merge_kernel, "merge", n // tm,
        [
            pl.BlockSpec((tm, d), row),
            pl.BlockSpec((1, d), full),
            pl.BlockSpec((tm, WIDTH_A), row),
            pl.BlockSpec((tm, WIDTH_B), row),
            pl.BlockSpec((tm, WIDTH_C), row),
            pl.BlockSpec(w_gates.shape, full),
            pl.BlockSpec(w_a.shape, full),
            pl.BlockSpec(w_b.shape, full),
            pl.BlockSpec(w_c.shape, full),
            pl.BlockSpec(w_o.shape, full),
        ],
        (x2, gain, oa, ob, oc, w_gates, w_a, w_b, w_c, w_o),
        [pl.BlockSpec((tm, d), row)],
        [jax.ShapeDtypeStruct((n, d), F32)],
        rider)


def _swiglu(h, wg, wu, wd):
    a = _dot(h, wg)
    u = _dot(h, wu)
    return _dot((a * _sigmoid(a) * u).astype(BF16), wd)


def _ffn_kernel(x_ref, g_ref, wg_ref, wu_ref, wd_ref, y_ref):
    x = x_ref[...]
    h = _rms(x, g_ref[...]).astype(BF16)
    y_ref[...] = x + _swiglu(h, wg_ref[...], wu_ref[...], wd_ref[...])


def _ffn(x2, gain, wg, wu, wd, tm, rider=None):
    n, d = x2.shape
    row = lambda i: (i, 0)
    full = lambda i: (0, 0)
    once = pl.Buffered(1)
    return _row_call(
        _ffn_kernel, "ffn", n // tm,
        [
            pl.BlockSpec((tm, d), row),
            pl.BlockSpec((1, d), full),
            pl.BlockSpec(wg.shape, full, pipeline_mode=once),
            pl.BlockSpec(wu.shape, full, pipeline_mode=once),
            pl.BlockSpec(wd.shape, full, pipeline_mode=once),
        ],
        (x2, gain, wg, wu, wd),
        [pl.BlockSpec((tm, d), row)],
        [jax.ShapeDtypeStruct((n, d), F32)],
        rider)


MOE_GROUP = 2048
MOE_PACK_BLOCK = 256
MOE_CHUNK = 128

def _split_bf16(a):
    hi = a.astype(BF16)
    return hi, (a - hi.astype(F32)).astype(BF16)


def _route_tile(x, g_ref, wr_ref, h_ref, rankt_ref, combt_ref, count_ref):
    tm = x.shape[0]
    tile_in_group = pl.program_id(0) % (MOE_GROUP // tm)
    h = _rms(x, g_ref[...])
    h_ref[...] = h.astype(BF16)
    h_hi, h_lo = _split_bf16(h)
    w_hi, w_lo = _split_bf16(wr_ref[...])
    logits = _dot(h_hi, w_hi) + (_dot(h_hi, w_lo) + _dot(h_lo, w_hi))
    lane = _lane(logits.shape)
    logits = jnp.where(lane < N_EXPERTS, logits, -jnp.inf)
    tops = []
    for _ in range(TOP_K):
        mx = jnp.max(logits, axis=-1, keepdims=True)
        first = jnp.min(jnp.where(logits == mx, lane, LANES), axis=-1, keepdims=True)
        tops.append((mx, first))
        logits = jnp.where(lane == first, -jnp.inf, logits)
    (v1, i1), (v2, i2) = tops
    e2 = jnp.exp(v2 - v1)
    w1 = 1.0 / (1.0 + e2)
    w2 = e2 / (1.0 + e2)
    sel = jnp.where(lane == i1, 1.0, 0.0) + jnp.where(lane == i2, 1.0, 0.0)
    comb = jnp.where(lane == i1, w1, 0.0) + jnp.where(lane == i2, w2, 0.0)
    combt_ref[0] = comb.T[:N_EXPERTS, :]

    @pl.when(tile_in_group == 0)
    def _():
        count_ref[...] = jnp.zeros_like(count_ref)

    r = lax.broadcasted_iota(jnp.int32, (tm, tm), 0)
    c = lax.broadcasted_iota(jnp.int32, (tm, tm), 1)
    before = _dot(jnp.where(c < r, 1.0, 0.0).astype(BF16), sel.astype(BF16)) + count_ref[...]
    rank = jnp.where(sel > 0.0, before, -1.0)
    count_ref[...] = count_ref[...] + jnp.sum(sel, axis=0, keepdims=True)
    rankt_ref[0] = rank.T[:N_EXPERTS, :]


def _router_kernel(x_ref, g_ref, wr_ref, *out_and_scratch):
    _route_tile(x_ref[...], g_ref, wr_ref, *out_and_scratch)


def _router(x2, gain, w_router_pad, tm):
    n, d = x2.shape
    assert n % MOE_GROUP == 0 and MOE_GROUP % tm == 0
    per_group = MOE_GROUP // tm
    row = lambda i: (i, 0)
    return pl.pallas_call(
        _router_kernel,
        grid=(n // tm,),
        in_specs=[
            pl.BlockSpec((tm, d), row),
            pl.BlockSpec((1, d), lambda i: (0, 0)),
            pl.BlockSpec((d, LANES), lambda i: (0, 0)),
        ],
        out_specs=[pl.BlockSpec((tm, d), row)]
                  + [pl.BlockSpec((1, N_EXPERTS, tm), lambda i: (i // per_group, 0, i % per_group))] * 2,
        out_shape=[jax.ShapeDtypeStruct((n, d), BF16)]
                  + [jax.ShapeDtypeStruct((n // MOE_GROUP, N_EXPERTS, MOE_GROUP), F32)] * 2,
        scratch_shapes=[pltpu.VMEM((1, LANES), F32)],
        compiler_params=_params("arbitrary"),
        name="router",
    )(x2, gain, w_router_pad)


def _experts_kernel(h_ref, rankt_ref, combt_ref, wg_ref, wu_ref, wd_ref, y_ref,
                    xs_ref, acc_ref, col_ref, start_ref):
    e = pl.program_id(1)
    f = pl.program_id(2)
    group, d = h_ref.shape
    n_blocks = group // MOE_CHUNK
    per_pack = MOE_PACK_BLOCK // MOE_CHUNK
    blocks = [slice(b * MOE_CHUNK, (b + 1) * MOE_CHUNK) for b in range(n_blocks)]
    pack_blocks = [slice(b * MOE_PACK_BLOCK, (b + 1) * MOE_PACK_BLOCK) for b in range(group // MOE_PACK_BLOCK)]

    def chunk_rows(m, n=1):
        return pl.ds(pl.multiple_of(m * MOE_CHUNK, MOE_CHUNK), n * MOE_CHUNK)

    def n_chunks():
        return (start_ref[n_blocks] + MOE_CHUNK - 1) // MOE_CHUNK

    def overlaps(m, b):
        return ((start_ref[b * per_pack] < (m + 1) * MOE_CHUNK)
                & (start_ref[(b + 1) * per_pack] > m * MOE_CHUNK))

    @pl.when((e == 0) & (f == 0))
    def _():
        y_ref[...] = jnp.zeros_like(y_ref)

    @pl.when(f == 0)
    def _dispatch():
        rank_row = rankt_ref[0, pl.ds(e, 1), :]
        sub = lax.broadcasted_iota(jnp.int32, (LANES, LANES), 0)
        for which, row in enumerate((rank_row, combt_ref[0, pl.ds(e, 1), :])):
            stacked = jnp.zeros((LANES, LANES), F32)
            for b, tokens in enumerate(blocks):
                stacked = jnp.where(sub == b, row[:, tokens], stacked)
            col_ref[which] = stacked.T
        start_ref[0] = 0
        for b in range(n_blocks):
            last = jnp.max(rank_row[:, blocks[b]]).astype(jnp.int32) + 1
            start_ref[b + 1] = jnp.maximum(start_ref[b], last)

        def pack(m, carry):
            dest = (lax.broadcasted_iota(jnp.int32, (MOE_CHUNK, MOE_PACK_BLOCK), 0)
                    + m * MOE_CHUNK).astype(F32)
            xs_ref[chunk_rows(m), :] = jnp.zeros((MOE_CHUNK, d), BF16)
            acc_ref[chunk_rows(m), :] = jnp.zeros((MOE_CHUNK, d), F32)
            for b, tokens in enumerate(pack_blocks):
                @pl.when(overlaps(m, b))
                def _():
                    onehot = jnp.where(rank_row[:, tokens] == dest, 1.0, 0.0).astype(BF16)
                    rows = _dot(onehot, h_ref[tokens, :])
                    xs_ref[chunk_rows(m), :] = (xs_ref[chunk_rows(m), :].astype(F32) + rows).astype(BF16)
            return carry

        lax.fori_loop(0, n_chunks(), pack, 0)
        xs_ref[chunk_rows(n_chunks(), 2), :] = jnp.zeros((2 * MOE_CHUNK, d), BF16)
        acc_ref[chunk_rows(n_chunks(), 2), :] = jnp.zeros((2 * MOE_CHUNK, d), F32)

    last_slice = f == pl.num_programs(2) - 1

    @pl.when(jnp.logical_not(last_slice))
    def _():
        def expert(m, carry):
            xm = xs_ref[chunk_rows(m), :]
            acc_ref[chunk_rows(m), :] += _swiglu(xm, wg_ref[0], wu_ref[0], wd_ref[0])
            return carry

        lax.fori_loop(0, n_chunks(), expert, 0)

    @pl.when(last_slice)
    def _combine():
        def expert(m, carry):
            xm = xs_ref[chunk_rows(m), :]
            y = acc_ref[chunk_rows(m), :] + _swiglu(xm, wg_ref[0], wu_ref[0], wd_ref[0])
            y_hi, y_lo = _split_bf16(y)
            xs_ref[chunk_rows(m), :] = y_hi
            acc_ref[chunk_rows(m), :] = y_lo.astype(F32)
            return carry

        lax.fori_loop(0, n_chunks(), expert, 0)
        for b, tokens in enumerate(blocks):
            first = start_ref[b] // MOE_CHUNK
            y_hi = xs_ref[chunk_rows(first, 2), :]
            y_lo = acc_ref[chunk_rows(first, 2), :].astype(BF16)
            dest = (_lane((MOE_CHUNK, 2 * MOE_CHUNK)) + first * MOE_CHUNK).astype(F32)
            onehot = jnp.where(col_ref[0, :, b:b + 1] == dest, 1.0, 0.0).astype(BF16)
            y_ref[tokens, :] += col_ref[1, :, b:b + 1] * (_dot(onehot, y_hi) + _dot(onehot, y_lo))


def _experts(hb, rank_t, comb_t, wg, wu, wd, ff_split):
    n, d = hb.shape
    n_e, _, d_ff = wg.shape
    ff = d_ff // ff_split
    assert MOE_GROUP // MOE_CHUNK <= LANES
    group = lambda s, e, f: (s, 0)
    once = pl.Buffered(1)
    return pl.pallas_call(
        _experts_kernel,
        grid=(n // MOE_GROUP, n_e, ff_split),
        in_specs=[
            pl.BlockSpec((MOE_GROUP, d), group, pipeline_mode=once),
            pl.BlockSpec((1, N_EXPERTS, MOE_GROUP), lambda s, e, f: (s, 0, 0)),
            pl.BlockSpec((1, N_EXPERTS, MOE_GROUP), lambda s, e, f: (s, 0, 0)),
            pl.BlockSpec((1, d, ff), lambda s, e, f: (e, 0, f)),
            pl.BlockSpec((1, d, ff), lambda s, e, f: (e, 0, f)),
            pl.BlockSpec((1, ff, d), lambda s, e, f: (e, f, 0)),
        ],
        out_specs=pl.BlockSpec((MOE_GROUP, d), group, pipeline_mode=once),
        out_shape=jax.ShapeDtypeStruct((n, d), F32),
        scratch_shapes=[
            pltpu.VMEM((MOE_GROUP + 2 * MOE_CHUNK, d), BF16),
            pltpu.VMEM((MOE_GROUP + 2 * MOE_CHUNK, d), F32),
            pltpu.VMEM((2, LANES, LANES), F32),
            pltpu.SMEM((MOE_GROUP // MOE_CHUNK + 1,), jnp.int32),
        ],
        compiler_params=_params("parallel", "arbitrary", "arbitrary"),
        name="experts",
    )(hb, rank_t, comb_t, wg, wu, wd)


def _add_kernel(x_ref, y_ref, o_ref):
    o_ref[...] = x_ref[...] + y_ref[...]


def _add_norm_kernel(x_ref, y_ref, g_ref, o_ref):
    o_ref[...] = _rms(x_ref[...] + y_ref[...], g_ref[...])


def _norm_kernel(x_ref, g_ref, o_ref):
    o_ref[...] = _rms(x_ref[...], g_ref[...])


def _rowwise(body, name, tm, x2, *rest):
    n, d = x2.shape
    row = pl.BlockSpec((tm, d), lambda i: (i, 0))
    specs = [row if a.shape == (n, d) else pl.BlockSpec((1, d), lambda i: (0, 0)) for a in (x2,) + rest]
    return pl.pallas_call(
        body,
        grid=(n // tm,),
        in_specs=specs,
        out_specs=row,
        out_shape=jax.ShapeDtypeStruct((n, d), F32),
        compiler_params=_params("parallel"),
        name=name,
    )(x2, *rest)


def _rope_tables(seq):
    pos = np.arange(seq, dtype=np.float64)
    inv_freq = ROPE_THETA ** (-np.arange(0, HEAD_DIM, 2, dtype=np.float64) / HEAD_DIM)
    ang = pos[:, None] * inv_freq[None, :]
    cos, sin = np.cos(ang), np.sin(ang)
    reps = LANES // HEAD_DIM
    cos_t = np.tile(np.concatenate([cos, cos], axis=-1), (1, reps))
    sin_t = np.tile(np.concatenate([-sin, sin], axis=-1), (1, reps))
    return jnp.asarray(cos_t, F32), jnp.asarray(sin_t, F32)


def _pair_heads_by_group(w, axis):
    shape = w.shape
    split = shape[:axis] + (SWA_KV_HEADS, SWA_GROUP, HEAD_DIM) + shape[axis + 1:]
    return jnp.swapaxes(w.reshape(split), axis, axis + 1).reshape(shape)


def _row_tile(n, want):
    t = min(n, want)
    assert n % t == 0
    return t


def kernel(x, norm_mix, w_in, attn_sinks, w_br_a, w_br_b, w_br_c, w_out, norm_ffn, w_ff_gate, w_ff_up, w_ff_down, w_router, w_moe_gate, w_moe_up, w_moe_down, norm_final):
    batch, seq, d = x.shape
    n = batch * seq
    depth = norm_mix.shape[0]
    tm = _row_tile(seq, 512)
    cos_t, sin_t = _rope_tables(seq)
    x2 = x.reshape(n, d).astype(F32)
    expert_bf16 = {}

    for layer in range(depth):
        w_l = w_in[layer]
        w_qa = _pair_heads_by_group(w_l[:, :WIDTH_A], 1)
        w_qkv = jnp.concatenate([w_qa, w_l[:, WIDTH_A:QKV_COLS]], axis=1).astype(BF16)
        w_gates = w_l[:, QKV_COLS:].astype(BF16)
        gain = norm_mix[layer].reshape(1, d)

        idx = layer // 2
        is_moe = layer % 2 == 1
        feeds_moe = not is_moe and layer + 1 < depth
        nxt = (layer + 1) // 2
        flat = lambda w: w.reshape(-1, w.shape[-1])

        qa, ka, va, qb, kb, vb, qc, kc, vc = _inproj(x2, gain, w_qkv, cos_t, sin_t, seq, tm)
        oa = _swa(attn_sinks[layer], qa, ka, va, batch, seq, tm)
        ob = _moba(qb, kb, vb, batch, seq)
        oc = _dilated(qc, kc, vc, batch, seq)
        gain_f = norm_ffn[layer].reshape(1, d)
        rider = flat(w_moe_down[idx]) if is_moe else flat(w_moe_up[nxt]) if feeds_moe else None
        x2, *cast = _merge(x2, gain, oa, ob, oc, w_gates,
                           _pair_heads_by_group(w_br_a[layer], 0).astype(BF16),
                           w_br_b[layer].astype(BF16), w_br_c[layer].astype(BF16),
                           w_out[layer].astype(BF16), tm, rider)
        if cast:
            expert_bf16["down" if is_moe else "up"] = cast[0]

        if not is_moe:
            x2, *cast = _ffn(x2, gain_f, w_ff_gate[idx].astype(BF16), w_ff_up[idx].astype(BF16),
                             w_ff_down[idx].astype(BF16), tm,
                             flat(w_moe_gate[nxt]) if feeds_moe else None)
            if cast:
                expert_bf16["gate"] = cast[0]
        else:
            w_r = jnp.pad(w_router[idx], ((0, 0), (0, LANES - N_EXPERTS)))
            hb, rank_t, comb_t = _router(x2, gain_f, w_r, tm)
            weights = [expert_bf16.pop(k).reshape(w.shape) if k in expert_bf16 else w.astype(BF16)
                       for k, w in (("gate", w_moe_gate[idx]), ("up", w_moe_up[idx]),
                                    ("down", w_moe_down[idx]))]
            y = _experts(hb, rank_t, comb_t, *weights, 2)
            if layer == depth - 1:
                out = _rowwise(_add_norm_kernel, "add_norm", tm, x2, y, norm_final.reshape(1, d))
                return out.reshape(batch, seq, d)
            x2 = _rowwise(_add_kernel, "add", tm, x2, y)

    return _rowwise(_norm_kernel, "final_norm", tm, x2, norm_final.reshape(1, d)).reshape(batch, seq, d)
```

```python
import functools

import jax
import jax.numpy as jnp
import numpy as np
from jax import lax
from jax.experimental import pallas as pl
from jax.experimental.pallas import tpu as pltpu

F32 = jnp.float32
BF16 = jnp.bfloat16

HEAD_DIM = 64
ATTN_SCALE = HEAD_DIM ** -0.5
LOG2_E = 1.4426950408889634
ROPE_THETA = 10000.0
RMS_EPS = 1e-5
NEG_INF = -1e30
BAND = 128
SWA_Q_HEADS, SWA_KV_HEADS = 6, 2
SWA_GROUP = SWA_Q_HEADS // SWA_KV_HEADS
SWA_MAX_DIST = 127
MOBA_HEADS, MOBA_BLOCK, MOBA_TOPK = 4, 256, 3
DIL_HEADS = 6
DIL_PATTERNS = ((128, 1), (512, 4), (2048, 16))
DIL_MAX_DIST = 128
N_EXPERTS, TOP_K = 8, 2

LANES = 128
VMEM_LIMIT = 56 * 1024 * 1024

WIDTH_A = SWA_Q_HEADS * HEAD_DIM
KV_A = SWA_KV_HEADS * HEAD_DIM
WIDTH_B = MOBA_HEADS * HEAD_DIM
WIDTH_C = DIL_HEADS * HEAD_DIM
QKV_COLS = WIDTH_A + 2 * KV_A + 3 * WIDTH_B + 3 * WIDTH_C


def _params(*sem):
    return pltpu.CompilerParams(dimension_semantics=sem, vmem_limit_bytes=VMEM_LIMIT)


def _row_call(body, name, steps, in_specs, args, out_specs, out_shapes, rider=None):
    in_specs, out_specs, out_shapes, args = list(in_specs), list(out_specs), list(out_shapes), list(args)
    if rider is not None:
        rows, cols = rider.shape
        assert rows % steps == 0
        n_in, n_out, inner = len(in_specs), len(out_specs), body
        slab = pl.BlockSpec((rows // steps, cols), lambda i: (i, 0))

        def body(*refs):
            inner(*refs[:n_in], *refs[n_in + 1:n_in + 1 + n_out])
            refs[-1][...] = refs[n_in][...].astype(BF16)

        in_specs.append(slab)
        args.append(rider)
        out_specs.append(slab)
        out_shapes.append(jax.ShapeDtypeStruct((rows, cols), BF16))
    return pl.pallas_call(
        body, grid=(steps,), in_specs=in_specs, out_specs=out_specs, out_shape=out_shapes,
        compiler_params=_params("parallel"), name=name,
    )(*args)


def _rms(x, gain):
    ms = jnp.mean(x * x, axis=-1, keepdims=True)
    return x * lax.rsqrt(ms + RMS_EPS) * gain


def _dot(a, b):
    return jnp.dot(a, b, preferred_element_type=F32)


def _dot_t(a, b):
    return lax.dot_general(a, b, (((1,), (1,)), ((), ())), preferred_element_type=F32)


def _sigmoid(z):
    return 1.0 / (1.0 + jnp.exp(-z))


def _lane(shape):
    return lax.broadcasted_iota(jnp.int32, shape, len(shape) - 1)


def _inproj_kernel(x_ref, g_ref, w_ref, cos_ref, sin_ref,
                   qa_ref, ka_ref, va_ref, qb_ref, kb_ref, vb_ref, qc_ref, kc_ref, vc_ref):
    tm = x_ref.shape[0]
    h = _rms(x_ref[...], g_ref[...]).astype(BF16)
    cos = cos_ref[...]
    sin = sin_ref[...]
    first_half = (_lane((tm, LANES)) & (HEAD_DIM // 2)) == 0

    def rope(z):
        rot = jnp.where(first_half,
                        pltpu.roll(z, LANES - HEAD_DIM // 2, 1),
                        pltpu.roll(z, HEAD_DIM // 2, 1))
        return z * cos + rot * sin

    z_all = _dot(h, w_ref[...])

    def proj(c0, width):
        return z_all[:, c0:c0 + width]

    def store(ref, z, roped, scale, dtype):
        for t in range(z.shape[1] // LANES):
            zt = z[:, t * LANES:(t + 1) * LANES]
            if roped:
                zt = rope(zt)
            if scale is not None:
                zt = zt * scale
            ref[:, t * LANES:(t + 1) * LANES] = zt.astype(dtype)

    c = 0
    store(qa_ref, proj(c, WIDTH_A), True, ATTN_SCALE, BF16); c += WIDTH_A
    kv = proj(c, 2 * KV_A); c += 2 * KV_A
    store(ka_ref, kv[:, :KV_A], True, None, BF16)
    store(va_ref, kv[:, KV_A:], False, None, BF16)
    store(qb_ref, proj(c, WIDTH_B), True, ATTN_SCALE * LOG2_E, BF16); c += WIDTH_B
    store(kb_ref, proj(c, WIDTH_B), True, None, BF16); c += WIDTH_B
    store(vb_ref, proj(c, WIDTH_B), False, None, BF16); c += WIDTH_B
    store(qc_ref, proj(c, WIDTH_C), True, ATTN_SCALE, F32); c += WIDTH_C
    store(kc_ref, proj(c, WIDTH_C), True, None, F32); c += WIDTH_C
    store(vc_ref, proj(c, WIDTH_C), False, None, F32)


def _inproj(x2, gain, w_qkv, cos_t, sin_t, seq, tm, rider=None):
    n, d = x2.shape
    tiles_per_seq = seq // tm
    row = lambda i: (i, 0)
    widths = (WIDTH_A, KV_A, KV_A, WIDTH_B, WIDTH_B, WIDTH_B, WIDTH_C, WIDTH_C, WIDTH_C)
    dtypes = (BF16,) * 6 + (F32,) * 3
    return _row_call(
        _inproj_kernel, "inproj", n // tm,
        [
            pl.BlockSpec((tm, d), row),
            pl.BlockSpec((1, d), lambda i: (0, 0)),
            pl.BlockSpec((d, QKV_COLS), lambda i: (0, 0)),
            pl.BlockSpec((tm, LANES), lambda i: (i % tiles_per_seq, 0)),
            pl.BlockSpec((tm, LANES), lambda i: (i % tiles_per_seq, 0)),
        ],
        (x2, gain, w_qkv, cos_t, sin_t),
        [pl.BlockSpec((tm, w), row) for w in widths],
        [jax.ShapeDtypeStruct((n, w), dt) for w, dt in zip(widths, dtypes)],
        rider)


def _band_mask(max_dist, block_start, stack=1):
    qi = lax.broadcasted_iota(jnp.int32, (stack * BAND, 2 * BAND), 0) % BAND
    kj = lax.broadcasted_iota(jnp.int32, (stack * BAND, 2 * BAND), 1) - BAND
    diff = qi - kj
    return (diff >= 0) & (diff <= max_dist) & (kj + block_start >= 0)


def _split_heads(a):
    a32 = a.astype(F32)
    lo = _lane(a.shape) < HEAD_DIM
    return jnp.where(lo, a32, 0.0).astype(BF16), jnp.where(lo, 0.0, a32).astype(BF16)


def _pair_dots(q, k):
    return [_dot_t(q, kh) for kh in _split_heads(k)]


def _pair_softmax(scores, mask):
    out = []
    for s in scores:
        s = jnp.where(mask, s, NEG_INF)
        m = jnp.max(s, axis=-1, keepdims=True)
        p = jnp.exp(s - m)
        out.append((m, p, jnp.sum(p, axis=-1, keepdims=True)))
    return out


def _pair_pv(p0, p1, v):
    pc = jnp.concatenate([p0.astype(BF16), p1.astype(BF16)], axis=1)
    return _dot(pc, jnp.concatenate(_split_heads(v), axis=0))


def _by_head(lo_val, hi_val, shape):
    return jnp.where(_lane(shape) < HEAD_DIM, lo_val, hi_val)


def _swa_kernel(sink_ref, q_ref, k_ref, kp_ref, v_ref, vp_ref, o_ref):
    nblk = q_ref.shape[0] // BAND
    chunk_start = pl.program_id(1) * q_ref.shape[0]
    n_pairs = WIDTH_A // LANES
    def load(j):
        rows = slice(j * BAND, (j + 1) * BAND)
        prev = slice((j - 1) * BAND, j * BAND)
        q = q_ref[rows, :]
        qs = jnp.concatenate([q[:, t * LANES:(t + 1) * LANES] for t in range(n_pairs)], axis=0)
        if j == 0:
            k = jnp.concatenate([kp_ref[...], k_ref[rows, :]], axis=0)
            v = jnp.concatenate([vp_ref[...], v_ref[rows, :]], axis=0)
        else:
            k = jnp.concatenate([k_ref[prev, :], k_ref[rows, :]], axis=0)
            v = jnp.concatenate([v_ref[prev, :], v_ref[rows, :]], axis=0)
        return rows, v, _pair_dots(qs, k)

    blocks = [load(j) for j in range(nblk)]
    for j, (rows, v, scores) in enumerate(blocks):
        mask = _band_mask(SWA_MAX_DIST, chunk_start + j * BAND, n_pairs)
        stats = _pair_softmax(scores, mask)
        acc = _pair_pv(stats[0][1], stats[1][1], v)
        scales = []
        for g, (m, _, l) in enumerate(stats):
            sink = jnp.concatenate(
                [jnp.full((BAND, 1), sink_ref[SWA_GROUP * g + t], F32) for t in range(n_pairs)],
                axis=0)
            m_all = jnp.maximum(m, sink)
            keep = jnp.exp(m - m_all)
            den = l * keep + jnp.exp(sink - m_all)
            scales.append(keep / den)
        out = acc * _by_head(scales[0], scales[1], acc.shape)
        for t in range(n_pairs):
            o_ref[rows, t * LANES:(t + 1) * LANES] = out[t * BAND:(t + 1) * BAND, :].astype(o_ref.dtype)


def _swa(sinks, qa, ka, va, batch, seq, tq):
    n = qa.shape[0]
    chunks = seq // tq
    blk = tq // BAND
    cur = lambda b, c: (b * chunks + c, 0)
    prev = lambda b, c: (jnp.maximum((b * chunks + c) * blk - 1, 0), 0)
    return pl.pallas_call(
        _swa_kernel,
        grid=(batch, chunks),
        in_specs=[
            pl.BlockSpec(memory_space=pltpu.SMEM),
            pl.BlockSpec((tq, WIDTH_A), cur),
            pl.BlockSpec((tq, KV_A), cur),
            pl.BlockSpec((BAND, KV_A), prev),
            pl.BlockSpec((tq, KV_A), cur),
            pl.BlockSpec((BAND, KV_A), prev),
        ],
        out_specs=pl.BlockSpec((tq, WIDTH_A), cur),
        out_shape=jax.ShapeDtypeStruct((n, WIDTH_A), BF16),
        compiler_params=_params("parallel", "parallel"),
        name="swa",
    )(sinks, qa, ka, ka, va, va)


MOBA_UNROLL = 4
MOBA_VT_ROWS = HEAD_DIM + 16


def _moba_kernel(q_ref, qn_ref, k_ref, v_ref, o_ref,
                 kaug_ref, vt_ref, kmean_ref, acc_ref, m_ref, l_ref, s_ref, qa_ref):
    i = pl.program_id(1)
    nb = k_ref.shape[0] // MOBA_BLOCK
    n_pairs = q_ref.shape[1] // LANES
    heads = range(2 * n_pairs)
    pair_cols = [slice(p * LANES, (p + 1) * LANES) for p in range(n_pairs)]
    blk_shape = (MOBA_BLOCK, LANES)
    lane = _lane(blk_shape)
    lo = lane < HEAD_DIM

    @pl.when(i == 0)
    def _prepare():
        kmean_ref[...] = jnp.zeros_like(kmean_ref)

        def body(n, carry):
            rows = pl.ds(pl.multiple_of(n * MOBA_BLOCK, MOBA_BLOCK), MOBA_BLOCK)
            block_hot = jnp.where((lane == n) | (lane == HEAD_DIM + n), 1.0, 0.0)
            ones = jnp.ones((MOBA_VT_ROWS - HEAD_DIM, MOBA_BLOCK), F32)
            for p, cols in enumerate(pair_cols):
                kb = k_ref[rows, cols].astype(F32)
                kmean_ref[p, pl.ds(n, 1), :] = jnp.sum(kb, axis=0, keepdims=True) * (1.0 / MOBA_BLOCK)
                kaug_ref[2 * p, rows, :] = jnp.where(lo, kb, block_hot).astype(BF16)
                kaug_ref[2 * p + 1, rows, :] = jnp.where(lo, block_hot, kb).astype(BF16)
                vt = v_ref[rows, cols].astype(F32).T
                vt_ref[n, 2 * p * MOBA_VT_ROWS:(2 * p + 2) * MOBA_VT_ROWS, :] = jnp.concatenate(
                    [vt[:HEAD_DIM], ones, vt[HEAD_DIM:], ones], axis=0).astype(BF16)
            return carry

        lax.fori_loop(0, nb, body, 0)

    def select(q, own, p):
        kmean = kmean_ref[p]
        lo_m = _lane(kmean.shape) < HEAD_DIM
        sel_shape = (HEAD_DIM, MOBA_BLOCK)
        blk = lax.broadcasted_iota(jnp.int32, sel_shape, 0)
        blk_f = blk.astype(F32)
        pens = []
        for km in (jnp.where(lo_m, kmean, 0.0), jnp.where(lo_m, 0.0, kmean)):
            gate = _dot_t(km.astype(BF16), q)
            gate = jnp.where(blk < own, gate, -jnp.inf)
            sel = blk == own
            for _ in range(MOBA_TOPK):
                mx = jnp.max(gate, axis=0, keepdims=True)
                first = jnp.min(jnp.where(gate == mx, blk_f, float(HEAD_DIM)), axis=0, keepdims=True)
                pick = (blk_f == first) & (mx > -jnp.inf)
                sel = sel | pick
                gate = jnp.where(pick, -jnp.inf, gate)
            pens.append(jnp.where(sel, 0.0, NEG_INF))
        zeros = jnp.zeros(sel_shape, F32)
        q32 = q.astype(F32)
        return (jnp.where(lo, q32, jnp.concatenate([zeros, pens[0]], axis=0).T).astype(BF16),
                jnp.where(lo, jnp.concatenate([pens[1], zeros], axis=0).T, q32).astype(BF16))

    @pl.when(i == 0)
    def _first_block():
        for p, cols in enumerate(pair_cols):
            qa_ref[0, 2 * p], qa_ref[0, 2 * p + 1] = select(q_ref[:, cols], 0, p)

    ahead = [qa for p, cols in enumerate(pair_cols) for qa in select(qn_ref[:, cols], i + 1, p)]
    qas = [qa_ref[i % 2, hd] for hd in heads]
    for hd in heads:
        qa_ref[(i + 1) % 2, hd] = ahead[hd]

    acc_ref[...] = jnp.zeros_like(acc_ref)
    m_ref[...] = jnp.full_like(m_ref, NEG_INF)
    l_ref[...] = jnp.zeros_like(l_ref)

    n_trips = i // MOBA_UNROLL + 1

    def scores(t, n):
        j = jnp.minimum(t * MOBA_UNROLL + n, i)
        rows = pl.ds(pl.multiple_of(j * MOBA_BLOCK, MOBA_BLOCK), MOBA_BLOCK)
        return [_dot_t(kaug_ref[hd, rows, :], qas[hd]) for hd in heads]

    def update(j, blk_scores, masked):
        vt = vt_ref[jnp.minimum(j, i)]
        ps, alphas = [], []
        for hd, s in enumerate(blk_scores):
            if masked:
                kk = lax.broadcasted_iota(jnp.int32, s.shape, 0)
                qq = lax.broadcasted_iota(jnp.int32, s.shape, 1)
                s = jnp.where(kk <= qq, s, NEG_INF)
            m_old = m_ref[hd]
            m_new = jnp.maximum(m_old, jnp.max(s, axis=0, keepdims=True))
            alphas.append(jnp.exp2(m_old - m_new))
            ps.append(jnp.exp2(s - m_new).astype(BF16))
            m_ref[hd] = m_new
        for hd in heads:
            pv = _dot(vt[hd * MOBA_VT_ROWS:(hd + 1) * MOBA_VT_ROWS, :], ps[hd])
            acc_ref[hd] = acc_ref[hd] * alphas[hd] + pv[:HEAD_DIM, :]
            l_ref[hd] = l_ref[hd] * alphas[hd] + pv[HEAD_DIM:HEAD_DIM + 1, :]

    for n in range(MOBA_UNROLL):
        for hd, s in enumerate(scores(0, n)):
            s_ref[n, hd] = s

    def body(t, carry):
        for n in range(MOBA_UNROLL):
            cur = [s_ref[n, hd] for hd in heads]
            nxt = scores(t + 1, n)
            update(t * MOBA_UNROLL + n, cur, False)
            for hd in heads:
                s_ref[n, hd] = nxt[hd]
        return carry

    lax.fori_loop(0, n_trips - 1, body, 0)
    for last in range(MOBA_UNROLL):
        @pl.when(i % MOBA_UNROLL == last)
        def _(last=last):
            for n in range(last + 1):
                update(i - last + n, [s_ref[n, hd] for hd in heads], n == last)
    for p, cols in enumerate(pair_cols):
        out_t = jnp.concatenate(
            [acc_ref[hd] * (1.0 / l_ref[hd]) for hd in (2 * p, 2 * p + 1)], axis=0)
        o_ref[:, cols] = out_t.T.astype(o_ref.dtype)


def _moba(qb, kb, vb, batch, seq):
    n = qb.shape[0]
    nb = seq // MOBA_BLOCK
    assert nb <= HEAD_DIM, "block one-hot must fit in the other head's lanes"
    qmap = lambda b, i: (b * nb + i, 0)
    qnext = lambda b, i: (b * nb + jnp.minimum(i + 1, nb - 1), 0)
    kvmap = lambda b, i: (b, 0)
    return pl.pallas_call(
        _moba_kernel,
        grid=(batch, nb),
        in_specs=[
            pl.BlockSpec((MOBA_BLOCK, WIDTH_B), qmap),
            pl.BlockSpec((MOBA_BLOCK, WIDTH_B), qnext),
            pl.BlockSpec((seq, WIDTH_B), kvmap),
            pl.BlockSpec((seq, WIDTH_B), kvmap),
        ],
        out_specs=pl.BlockSpec((MOBA_BLOCK, WIDTH_B), qmap),
        out_shape=jax.ShapeDtypeStruct((n, WIDTH_B), BF16),
        scratch_shapes=[
            pltpu.VMEM((MOBA_HEADS, seq, LANES), BF16),
            pltpu.VMEM((nb, MOBA_HEADS * MOBA_VT_ROWS, MOBA_BLOCK), BF16),
            pltpu.VMEM((MOBA_HEADS // 2, HEAD_DIM, LANES), F32),
            pltpu.VMEM((MOBA_HEADS, HEAD_DIM, MOBA_BLOCK), F32),
            pltpu.VMEM((MOBA_HEADS, 1, MOBA_BLOCK), F32),
            pltpu.VMEM((MOBA_HEADS, 1, MOBA_BLOCK), F32),
            pltpu.VMEM((MOBA_UNROLL, MOBA_HEADS, MOBA_BLOCK, MOBA_BLOCK), F32),
            pltpu.VMEM((2, MOBA_HEADS, MOBA_BLOCK, LANES), BF16),
        ],
        compiler_params=_params("parallel", "arbitrary"),
        name="moba",
    )(qb, qb, kb, vb)


DIL_CHUNK = DIL_PATTERNS[-1][1] * BAND
DIL_UNROLL = {1: 8, 4: 8, 16: 4}


def _dil_kernel(q_ref, kp_ref, k_ref, vp_ref, v_ref, o_ref, kk_ref, vv_ref, acc_ref, m_ref, l_ref):
    c = pl.program_id(2)
    kk_ref[0:DIL_CHUNK, :] = kp_ref[...]
    kk_ref[DIL_CHUNK:, :] = k_ref[...]
    vv_ref[0:DIL_CHUNK, :] = vp_ref[...]
    vv_ref[DIL_CHUNK:, :] = v_ref[...]
    blocks = DIL_CHUNK // BAND
    shape = (BAND, LANES)

    for pi, (_, d) in enumerate(DIL_PATTERNS):
        per_class = blocks // d

        def load(t, d=d, per_class=per_class):
            r = t // per_class
            j = t % per_class
            q0 = r + BAND * d * j
            k0 = DIL_CHUNK + q0 - BAND * d
            if d == 1:
                qrows = pl.ds(pl.multiple_of(q0, BAND), BAND)
                krows = pl.ds(pl.multiple_of(k0, BAND), 2 * BAND)
            else:
                qrows = pl.ds(q0, BAND, stride=d)
                krows = pl.ds(k0, 2 * BAND, stride=d)
            scores = _pair_dots(q_ref[qrows, :].astype(BF16), kk_ref[krows, :])
            return qrows, krows, (c * per_class + j) * BAND, scores

        unroll = DIL_UNROLL[d]

        def body(t, carry, pi=pi, load=load, unroll=unroll):
            group = [load(t * unroll + u) for u in range(unroll)]
            for qrows, krows, block_start, scores in group:
                mask = _band_mask(DIL_MAX_DIST, block_start)
                (m0, p0, l0), (m1, p1, l1) = _pair_softmax(scores, mask)
                acc_ref[pi, qrows, :] = _pair_pv(p0, p1, vv_ref[krows, :])
                m_ref[pi, qrows, :] = _by_head(m0, m1, shape)
                l_ref[pi, qrows, :] = _by_head(l0, l1, shape)
            return carry

        lax.fori_loop(0, blocks // unroll, body, 0)

    def combine(t, carry):
        rows = pl.ds(pl.multiple_of(t * BAND, BAND), BAND)
        ms = [m_ref[pi, rows, :] for pi in range(len(DIL_PATTERNS))]
        m_max = functools.reduce(jnp.maximum, ms)
        num = jnp.zeros(shape, F32)
        den = jnp.zeros(shape, F32)
        for pi, m in enumerate(ms):
            e = jnp.exp(m - m_max)
            num = num + acc_ref[pi, rows, :] * e
            den = den + l_ref[pi, rows, :] * e
        o_ref[rows, :] = (num / den).astype(o_ref.dtype)
        return carry

    lax.fori_loop(0, blocks, combine, 0)


def _dilated(qc, kc, vc, batch, seq):
    n = qc.shape[0]
    assert seq % DIL_CHUNK == 0
    chunks = seq // DIL_CHUNK
    pairs = WIDTH_C // LANES
    cur = lambda b, p, c: (b * chunks + c, p)
    prev = lambda b, p, c: (b * chunks + jnp.maximum(c - 1, 0), p)
    blk = (DIL_CHUNK, LANES)
    n_pat = len(DIL_PATTERNS)
    return pl.pallas_call(
        _dil_kernel,
        grid=(batch, pairs, chunks),
        in_specs=[pl.BlockSpec(blk, cur), pl.BlockSpec(blk, prev), pl.BlockSpec(blk, cur),
                  pl.BlockSpec(blk, prev), pl.BlockSpec(blk, cur)],
        out_specs=pl.BlockSpec(blk, cur),
        out_shape=jax.ShapeDtypeStruct((n, WIDTH_C), BF16),
        scratch_shapes=[
            pltpu.VMEM((2 * DIL_CHUNK, LANES), F32),
            pltpu.VMEM((2 * DIL_CHUNK, LANES), F32),
            pltpu.VMEM((n_pat, DIL_CHUNK, LANES), F32),
            pltpu.VMEM((n_pat, DIL_CHUNK, LANES), F32),
            pltpu.VMEM((n_pat, DIL_CHUNK, LANES), F32),
        ],
        compiler_params=_params("parallel", "parallel", "parallel"),
        name="dilated",
    )(qc, kc, kc, vc, vc)


def _merged_residual(x_ref, g_ref, oa_ref, ob_ref, oc_ref, wg_ref, wa_ref, wb_ref, wc_ref, wo_ref):
    x = x_ref[...]
    d = x.shape[1]
    h = _rms(x, g_ref[...]).astype(BF16)
    merged = None
    for br, (o_ref, w_ref) in enumerate(((oa_ref, wa_ref), (ob_ref, wb_ref), (oc_ref, wc_ref))):
        gate = _sigmoid(_dot(h, wg_ref[:, br * d:(br + 1) * d]))
        y = gate * _dot(o_ref[...], w_ref[...])
        merged = y if merged is None else merged + y
    return x + _dot(merged.astype(BF16), wo_ref[...])


def _merge_kernel(*refs):
    y_ref = refs[-1]
    y_ref[...] = _merged_residual(*refs[:-1])


def _merge(x2, gain, oa, ob, oc, w_gates, w_a, w_b, w_c, w_o, tm, rider=None):
    n, d = x2.shape
    row = lambda i: (i, 0)
    full = lambda i: (0, 0)
    return _row_call(
        _merge_kernel, "merge", n // tm,
        [
            pl.BlockSpec((tm, d), row),
            pl.BlockSpec((1, d), full),
            pl.BlockSpec((tm, WIDTH_A), row),
            pl.BlockSpec((tm, WIDTH_B), row),
            pl.BlockSpec((tm, WIDTH_C), row),
            pl.BlockSpec(w_gates.shape, full),
            pl.BlockSpec(w_a.shape, full),
            pl.BlockSpec(w_b.shape, full),
            pl.BlockSpec(w_c.shape, full),
            pl.BlockSpec(w_o.shape, full),
        ],
        (x2, gain, oa, ob, oc, w_gates, w_a, w_b, w_c, w_o),
        [pl.BlockSpec((tm, d), row)],
        [jax.ShapeDtypeStruct((n, d), F32)],
        rider)


def _swiglu(h, wg, wu, wd):
    a = _dot(h, wg)
    u = _dot(h, wu)
    return _dot((a * _sigmoid(a) * u).astype(BF16), wd)


def _ffn_kernel(x_ref, g_ref, wg_ref, wu_ref, wd_ref, y_ref):
    x = x_ref[...]
    h = _rms(x, g_ref[...]).astype(BF16)
    y_ref[...] = x + _swiglu(h, wg_ref[...], wu_ref[...], wd_ref[...])


def _ffn(x2, gain, wg, wu, wd, tm, rider=None):
    n, d = x2.shape
    row = lambda i: (i, 0)
    full = lambda i: (0, 0)
    once = pl.Buffered(1)
    return _row_call(
        _ffn_kernel, "ffn", n // tm,
        [
            pl.BlockSpec((tm, d), row),
            pl.BlockSpec((1, d), full),
            pl.BlockSpec(wg.shape, full, pipeline_mode=once),
            pl.BlockSpec(wu.shape, full, pipeline_mode=once),
            pl.BlockSpec(wd.shape, full, pipeline_mode=once),
        ],
        (x2, gain, wg, wu, wd),
        [pl.BlockSpec((tm, d), row)],
        [jax.ShapeDtypeStruct((n, d), F32)],
        rider)


MOE_GROUP = 2048
MOE_PACK_BLOCK = 256
MOE_CHUNK = 128
ROUTER_ROWS = 16

def _split_bf16(a):
    hi = a.astype(BF16)
    return hi, (a - hi.astype(F32)).astype(BF16)


def _route_tile(x, g_ref, wr_ref, h_ref, rankt_ref, combt_ref, count_ref):
    tm = x.shape[0]
    tile_in_group = pl.program_id(0) % (MOE_GROUP // tm)
    h = _rms(x, g_ref[...])
    h_ref[...] = h.astype(BF16)
    h_hi, h_lo = _split_bf16(h)
    w_hi, w_lo = _split_bf16(wr_ref[...])
    logits = _dot_t(w_hi, h_hi) + (_dot_t(w_lo, h_hi) + _dot_t(w_hi, h_lo))
    row = lax.broadcasted_iota(jnp.int32, logits.shape, 0).astype(F32)
    logits = jnp.where(row < N_EXPERTS, logits, -jnp.inf)
    tops = []
    for _ in range(TOP_K):
        mx = jnp.max(logits, axis=0, keepdims=True)
        first = jnp.min(jnp.where(logits == mx, row, float(ROUTER_ROWS)), axis=0, keepdims=True)
        tops.append((mx, first))
        logits = jnp.where(row == first, -jnp.inf, logits)
    (v1, i1), (v2, i2) = tops
    e2 = jnp.exp(v2 - v1)
    w1 = 1.0 / (1.0 + e2)
    w2 = e2 / (1.0 + e2)
    sel = jnp.where(row == i1, 1.0, 0.0) + jnp.where(row == i2, 1.0, 0.0)
    comb = jnp.where(row == i1, w1, 0.0) + jnp.where(row == i2, w2, 0.0)
    combt_ref[0] = comb[:N_EXPERTS, :]

    @pl.when(tile_in_group == 0)
    def _():
        count_ref[...] = jnp.zeros_like(count_ref)

    r = lax.broadcasted_iota(jnp.int32, (tm, tm), 0)
    c = lax.broadcasted_iota(jnp.int32, (tm, tm), 1)
    before = _dot(sel.astype(BF16), jnp.where(r < c, 1.0, 0.0).astype(BF16)) + count_ref[:, :1]
    rank = jnp.where(sel > 0.0, before, -1.0)
    count_ref[...] = count_ref[...] + jnp.sum(sel, axis=1, keepdims=True)
    rankt_ref[0] = rank[:N_EXPERTS, :]


def _router_kernel(x_ref, g_ref, wr_ref, *out_and_scratch):
    _route_tile(x_ref[...], g_ref, wr_ref, *out_and_scratch)


def _router(x2, gain, w_router_pad, tm):
    n, d = x2.shape
    assert n % MOE_GROUP == 0 and MOE_GROUP % tm == 0
    per_group = MOE_GROUP // tm
    row = lambda i: (i, 0)
    return pl.pallas_call(
        _router_kernel,
        grid=(n // tm,),
        in_specs=[
            pl.BlockSpec((tm, d), row),
            pl.BlockSpec((1, d), lambda i: (0, 0)),
            pl.BlockSpec((ROUTER_ROWS, d), lambda i: (0, 0)),
        ],
        out_specs=[pl.BlockSpec((tm, d), row)]
                  + [pl.BlockSpec((1, N_EXPERTS, tm), lambda i: (i // per_group, 0, i % per_group))] * 2,
        out_shape=[jax.ShapeDtypeStruct((n, d), BF16)]
                  + [jax.ShapeDtypeStruct((n // MOE_GROUP, N_EXPERTS, MOE_GROUP), F32)] * 2,
        scratch_shapes=[pltpu.VMEM((ROUTER_ROWS, LANES), F32)],
        compiler_params=_params("arbitrary"),
        name="router",
    )(x2, gain, w_router_pad)


def _experts_kernel(h_ref, rankt_ref, combt_ref, wg_ref, wu_ref, wd_ref, y_ref,
                    xs_ref, acc_ref, col_ref, start_ref):
    e = pl.program_id(1)
    f = pl.program_id(2)
    group, d = h_ref.shape
    n_blocks = group // MOE_CHUNK
    per_pack = MOE_PACK_BLOCK // MOE_CHUNK
    blocks = [slice(b * MOE_CHUNK, (b + 1) * MOE_CHUNK) for b in range(n_blocks)]
    pack_blocks = [slice(b * MOE_PACK_BLOCK, (b + 1) * MOE_PACK_BLOCK) for b in range(group // MOE_PACK_BLOCK)]

    def chunk_rows(m, n=1):
        return pl.ds(pl.multiple_of(m * MOE_CHUNK, MOE_CHUNK), n * MOE_CHUNK)

    def n_chunks():
        return (start_ref[n_blocks] + MOE_CHUNK - 1) // MOE_CHUNK

    def overlaps(m, b):
        return ((start_ref[b * per_pack] < (m + 1) * MOE_CHUNK)
                & (start_ref[(b + 1) * per_pack] > m * MOE_CHUNK))

    @pl.when((e == 0) & (f == 0))
    def _():
        y_ref[...] = jnp.zeros_like(y_ref)

    @pl.when(f == 0)
    def _dispatch():
        rank_row = rankt_ref[0, pl.ds(e, 1), :]
        sub = lax.broadcasted_iota(jnp.int32, (LANES, LANES), 0)
        for which, row in enumerate((rank_row, combt_ref[0, pl.ds(e, 1), :])):
            stacked = jnp.zeros((LANES, LANES), F32)
            for b, tokens in enumerate(blocks):
                stacked = jnp.where(sub == b, row[:, tokens], stacked)
            col_ref[which] = stacked.T
        start_ref[0] = 0
        for b in range(n_blocks):
            last = jnp.max(rank_row[:, blocks[b]]).astype(jnp.int32) + 1
            start_ref[b + 1] = jnp.maximum(start_ref[b], last)

        def pack(m, carry):
            dest = (lax.broadcasted_iota(jnp.int32, (MOE_CHUNK, MOE_PACK_BLOCK), 0)
                    + m * MOE_CHUNK).astype(F32)
            xs_ref[chunk_rows(m), :] = jnp.zeros((MOE_CHUNK, d), BF16)
            acc_ref[chunk_rows(m), :] = jnp.zeros((MOE_CHUNK, d), F32)
            for b, tokens in enumerate(pack_blocks):
                @pl.when(overlaps(m, b))
                def _():
                    onehot = jnp.where(rank_row[:, tokens] == dest, 1.0, 0.0).astype(BF16)
                    rows = _dot(onehot, h_ref[tokens, :])
                    xs_ref[chunk_rows(m), :] = (xs_ref[chunk_rows(m), :].astype(F32) + rows).astype(BF16)
            return carry

        lax.fori_loop(0, n_chunks(), pack, 0)
        xs_ref[chunk_rows(n_chunks(), 2), :] = jnp.zeros((2 * MOE_CHUNK, d), BF16)
        acc_ref[chunk_rows(n_chunks(), 2), :] = jnp.zeros((2 * MOE_CHUNK, d), F32)

    last_slice = f == pl.num_programs(2) - 1

    @pl.when(jnp.logical_not(last_slice))
    def _():
        def expert(m, carry):
            xm = xs_ref[chunk_rows(m), :]
            acc_ref[chunk_rows(m), :] += _swiglu(xm, wg_ref[0], wu_ref[0], wd_ref[0])
            return carry

        lax.fori_loop(0, n_chunks(), expert, 0)

    @pl.when(last_slice)
    def _combine():
        def expert(m, carry):
            xm = xs_ref[chunk_rows(m), :]
            y = acc_ref[chunk_rows(m), :] + _swiglu(xm, wg_ref[0], wu_ref[0], wd_ref[0])
            y_hi, y_lo = _split_bf16(y)
            xs_ref[chunk_rows(m), :] = y_hi
            acc_ref[chunk_rows(m), :] = y_lo.astype(F32)
            return carry

        lax.fori_loop(0, n_chunks(), expert, 0)
        for b, tokens in enumerate(blocks):
            first = start_ref[b] // MOE_CHUNK
            y_hi = xs_ref[chunk_rows(first, 2), :]
            y_lo = acc_ref[chunk_rows(first, 2), :].astype(BF16)
            dest = (_lane((MOE_CHUNK, 2 * MOE_CHUNK)) + first * MOE_CHUNK).astype(F32)
            onehot = jnp.where(col_ref[0, :, b:b + 1] == dest, 1.0, 0.0).astype(BF16)
            y_ref[tokens, :] += col_ref[1, :, b:b + 1] * (_dot(onehot, y_hi) + _dot(onehot, y_lo))


def _experts(hb, rank_t, comb_t, wg, wu, wd, ff_split):
    n, d = hb.shape
    n_e, _, d_ff = wg.shape
    ff = d_ff // ff_split
    assert MOE_GROUP // MOE_CHUNK <= LANES
    group = lambda s, e, f: (s, 0)
    once = pl.Buffered(1)
    return pl.pallas_call(
        _experts_kernel,
        grid=(n // MOE_GROUP, n_e, ff_split),
        in_specs=[
            pl.BlockSpec((MOE_GROUP, d), group, pipeline_mode=once),
            pl.BlockSpec((1, N_EXPERTS, MOE_GROUP), lambda s, e, f: (s, 0, 0)),
            pl.BlockSpec((1, N_EXPERTS, MOE_GROUP), lambda s, e, f: (s, 0, 0)),
            pl.BlockSpec((1, d, ff), lambda s, e, f: (e, 0, f)),
            pl.BlockSpec((1, d, ff), lambda s, e, f: (e, 0, f)),
            pl.BlockSpec((1, ff, d), lambda s, e, f: (e, f, 0)),
        ],
        out_specs=pl.BlockSpec((MOE_GROUP, d), group, pipeline_mode=once),
        out_shape=jax.ShapeDtypeStruct((n, d), F32),
        scratch_shapes=[
            pltpu.VMEM((MOE_GROUP + 2 * MOE_CHUNK, d), BF16),
            pltpu.VMEM((MOE_GROUP + 2 * MOE_CHUNK, d), F32),
            pltpu.VMEM((2, LANES, LANES), F32),
            pltpu.SMEM((MOE_GROUP // MOE_CHUNK + 1,), jnp.int32),
        ],
        compiler_params=_params("parallel", "arbitrary", "arbitrary"),
        name="experts",
    )(hb, rank_t, comb_t, wg, wu, wd)


def _add_kernel(x_ref, y_ref, o_ref):
    o_ref[...] = x_ref[...] + y_ref[...]


def _add_norm_kernel(x_ref, y_ref, g_ref, o_ref):
    o_ref[...] = _rms(x_ref[...] + y_ref[...], g_ref[...])


def _norm_kernel(x_ref, g_ref, o_ref):
    o_ref[...] = _rms(x_ref[...], g_ref[...])


def _rowwise(body, name, tm, x2, *rest):
    n, d = x2.shape
    row = pl.BlockSpec((tm, d), lambda i: (i, 0))
    specs = [row if a.shape == (n, d) else pl.BlockSpec((1, d), lambda i: (0, 0)) for a in (x2,) + rest]
    return pl.pallas_call(
        body,
        grid=(n // tm,),
        in_specs=specs,
        out_specs=row,
        out_shape=jax.ShapeDtypeStruct((n, d), F32),
        compiler_params=_params("parallel"),
        name=name,
    )(x2, *rest)


def _rope_tables(seq):
    pos = np.arange(seq, dtype=np.float64)
    inv_freq = ROPE_THETA ** (-np.arange(0, HEAD_DIM, 2, dtype=np.float64) / HEAD_DIM)
    ang = pos[:, None] * inv_freq[None, :]
    cos, sin = np.cos(ang), np.sin(ang)
    reps = LANES // HEAD_DIM
    cos_t = np.tile(np.concatenate([cos, cos], axis=-1), (1, reps))
    sin_t = np.tile(np.concatenate([-sin, sin], axis=-1), (1, reps))
    return jnp.asarray(cos_t, F32), jnp.asarray(sin_t, F32)


def _pair_heads_by_group(w, axis):
    shape = w.shape
    split = shape[:axis] + (SWA_KV_HEADS, SWA_GROUP, HEAD_DIM) + shape[axis + 1:]
    return jnp.swapaxes(w.reshape(split), axis, axis + 1).reshape(shape)


def _row_tile(n, want):
    t = min(n, want)
    assert n % t == 0
    return t


def kernel(x, norm_mix, w_in, attn_sinks, w_br_a, w_br_b, w_br_c, w_out, norm_ffn, w_ff_gate, w_ff_up, w_ff_down, w_router, w_moe_gate, w_moe_up, w_moe_down, norm_final):
    batch, seq, d = x.shape
    n = batch * seq
    depth = norm_mix.shape[0]
    tm = _row_tile(seq, 512)
    cos_t, sin_t = _rope_tables(seq)
    x2 = x.reshape(n, d).astype(F32)
    expert_bf16 = {}

    for layer in range(depth):
        w_l = w_in[layer]
        w_qa = _pair_heads_by_group(w_l[:, :WIDTH_A], 1)
        w_qkv = jnp.concatenate([w_qa, w_l[:, WIDTH_A:QKV_COLS]], axis=1).astype(BF16)
        w_gates = w_l[:, QKV_COLS:].astype(BF16)
        gain = norm_mix[layer].reshape(1, d)

        idx = layer // 2
        is_moe = layer % 2 == 1
        feeds_moe = not is_moe and layer + 1 < depth
        nxt = (layer + 1) // 2
        flat = lambda w: w.reshape(-1, w.shape[-1])

        qa, ka, va, qb, kb, vb, qc, kc, vc = _inproj(x2, gain, w_qkv, cos_t, sin_t, seq, tm)
        oa = _swa(attn_sinks[layer], qa, ka, va, batch, seq, tm)
        ob = _moba(qb, kb, vb, batch, seq)
        oc = _dilated(qc, kc, vc, batch, seq)
        gain_f = norm_ffn[layer].reshape(1, d)
        rider = flat(w_moe_down[idx]) if is_moe else flat(w_moe_up[nxt]) if feeds_moe else None
        x2, *cast = _merge(x2, gain, oa, ob, oc, w_gates,
                           _pair_heads_by_group(w_br_a[layer], 0).astype(BF16),
                           w_br_b[layer].astype(BF16), w_br_c[layer].astype(BF16),
                           w_out[layer].astype(BF16), tm, rider)
        if cast:
            expert_bf16["down" if is_moe else "up"] = cast[0]

        if not is_moe:
            x2, *cast = _ffn(x2, gain_f, w_ff_gate[idx].astype(BF16), w_ff_up[idx].astype(BF16),
                             w_ff_down[idx].astype(BF16), tm,
                             flat(w_moe_gate[nxt]) if feeds_moe else None)
            if cast:
                expert_bf16["gate"] = cast[0]
        else:
            w_r = jnp.pad(w_router[idx].T, ((0, ROUTER_ROWS - N_EXPERTS), (0, 0)))
            hb, rank_t, comb_t = _router(x2, gain_f, w_r, tm)
            weights = [expert_bf16.pop(k).reshape(w.shape) if k in expert_bf16 else w.astype(BF16)
                       for k, w in (("gate", w_moe_gate[idx]), ("up", w_moe_up[idx]),
                                    ("down", w_moe_down[idx]))]
            y = _experts(hb, rank_t, comb_t, *weights, 2)
            if layer == depth - 1:
                out = _rowwise(_add_norm_kernel, "add_norm", tm, x2, y, norm_final.reshape(1, d))
                return out.reshape(batch, seq, d)
            x2 = _rowwise(_add_kernel, "add", tm, x2, y)

    return _rowwise(_norm_kernel, "final_norm", tm, x2, norm_final.reshape(1, d)).reshape(batch, seq, d)
```

```python
import functools

import jax
import jax.numpy as jnp
import numpy as np
from jax import lax
from jax.experimental import pallas as pl
from jax.experimental.pallas import tpu as pltpu

F32 = jnp.float32
BF16 = jnp.bfloat16

HEAD_DIM = 64
ATTN_SCALE = HEAD_DIM ** -0.5
LOG2_E = 1.4426950408889634
ROPE_THETA = 10000.0
RMS_EPS = 1e-5
NEG_INF = -1e30
BAND = 128
SWA_Q_HEADS, SWA_KV_HEADS = 6, 2
SWA_GROUP = SWA_Q_HEADS // SWA_KV_HEADS
SWA_MAX_DIST = 127
MOBA_HEADS, MOBA_BLOCK, MOBA_TOPK = 4, 256, 3
DIL_HEADS = 6
DIL_PATTERNS = ((128, 1), (512, 4), (2048, 16))
DIL_MAX_DIST = 128
N_EXPERTS, TOP_K = 8, 2

LANES = 128
VMEM_LIMIT = 56 * 1024 * 1024

WIDTH_A = SWA_Q_HEADS * HEAD_DIM
KV_A = SWA_KV_HEADS * HEAD_DIM
WIDTH_B = MOBA_HEADS * HEAD_DIM
WIDTH_C = DIL_HEADS * HEAD_DIM
QKV_COLS = WIDTH_A + 2 * KV_A + 3 * WIDTH_B + 3 * WIDTH_C


def _params(*sem):
    return pltpu.CompilerParams(dimension_semantics=sem, vmem_limit_bytes=VMEM_LIMIT)


def _row_call(body, name, steps, in_specs, args, out_specs, out_shapes, rider=None):
    in_specs, out_specs, out_shapes, args = list(in_specs), list(out_specs), list(out_shapes), list(args)
    if rider is not None:
        rows, cols = rider.shape
        assert rows % steps == 0
        n_in, n_out, inner = len(in_specs), len(out_specs), body
        slab = pl.BlockSpec((rows // steps, cols), lambda i: (i, 0))

        def body(*refs):
            inner(*refs[:n_in], *refs[n_in + 1:n_in + 1 + n_out])
            refs[-1][...] = refs[n_in][...].astype(BF16)

        in_specs.append(slab)
        args.append(rider)
        out_specs.append(slab)
        out_shapes.append(jax.ShapeDtypeStruct((rows, cols), BF16))
    return pl.pallas_call(
        body, grid=(steps,), in_specs=in_specs, out_specs=out_specs, out_shape=out_shapes,
        compiler_params=_params("parallel"), name=name,
    )(*args)


def _rms(x, gain):
    ms = jnp.mean(x * x, axis=-1, keepdims=True)
    return x * lax.rsqrt(ms + RMS_EPS) * gain


def _dot(a, b):
    return jnp.dot(a, b, preferred_element_type=F32)


def _dot_t(a, b):
    return lax.dot_general(a, b, (((1,), (1,)), ((), ())), preferred_element_type=F32)


def _sigmoid(z):
    return 1.0 / (1.0 + jnp.exp(-z))


def _lane(shape):
    return lax.broadcasted_iota(jnp.int32, shape, len(shape) - 1)


def _inproj_kernel(x_ref, g_ref, w_ref, cos_ref, sin_ref,
                   qa_ref, ka_ref, va_ref, qb_ref, kb_ref, vb_ref, qc_ref, kc_ref, vc_ref):
    tm = x_ref.shape[0]
    h = _rms(x_ref[...], g_ref[...]).astype(BF16)
    cos = cos_ref[...]
    sin = sin_ref[...]
    first_half = (_lane((tm, LANES)) & (HEAD_DIM // 2)) == 0

    def rope(z):
        rot = jnp.where(first_half,
                        pltpu.roll(z, LANES - HEAD_DIM // 2, 1),
                        pltpu.roll(z, HEAD_DIM // 2, 1))
        return z * cos + rot * sin

    z_all = _dot(h, w_ref[...])

    def proj(c0, width):
        return z_all[:, c0:c0 + width]

    def store(ref, z, roped, scale, dtype):
        for t in range(z.shape[1] // LANES):
            zt = z[:, t * LANES:(t + 1) * LANES]
            if roped:
                zt = rope(zt)
            if scale is not None:
                zt = zt * scale
            ref[:, t * LANES:(t + 1) * LANES] = zt.astype(dtype)

    c = 0
    store(qa_ref, proj(c, WIDTH_A), True, ATTN_SCALE, BF16); c += WIDTH_A
    kv = proj(c, 2 * KV_A); c += 2 * KV_A
    store(ka_ref, kv[:, :KV_A], True, None, BF16)
    store(va_ref, kv[:, KV_A:], False, None, BF16)
    store(qb_ref, proj(c, WIDTH_B), True, ATTN_SCALE * LOG2_E, BF16); c += WIDTH_B
    store(kb_ref, proj(c, WIDTH_B), True, None, BF16); c += WIDTH_B
    store(vb_ref, proj(c, WIDTH_B), False, None, BF16); c += WIDTH_B
    store(qc_ref, proj(c, WIDTH_C), True, ATTN_SCALE, F32); c += WIDTH_C
    store(kc_ref, proj(c, WIDTH_C), True, None, F32); c += WIDTH_C
    store(vc_ref, proj(c, WIDTH_C), False, None, F32)


def _inproj(x2, gain, w_qkv, cos_t, sin_t, seq, tm, rider=None):
    n, d = x2.shape
    tiles_per_seq = seq // tm
    row = lambda i: (i, 0)
    widths = (WIDTH_A, KV_A, KV_A, WIDTH_B, WIDTH_B, WIDTH_B, WIDTH_C, WIDTH_C, WIDTH_C)
    dtypes = (BF16,) * 6 + (F32,) * 3
    return _row_call(
        _inproj_kernel, "inproj", n // tm,
        [
            pl.BlockSpec((tm, d), row),
            pl.BlockSpec((1, d), lambda i: (0, 0)),
            pl.BlockSpec((d, QKV_COLS), lambda i: (0, 0)),
            pl.BlockSpec((tm, LANES), lambda i: (i % tiles_per_seq, 0)),
            pl.BlockSpec((tm, LANES), lambda i: (i % tiles_per_seq, 0)),
        ],
        (x2, gain, w_qkv, cos_t, sin_t),
        [pl.BlockSpec((tm, w), row) for w in widths],
        [jax.ShapeDtypeStruct((n, w), dt) for w, dt in zip(widths, dtypes)],
        rider)


def _band_mask(max_dist, block_start, stack=1):
    qi = lax.broadcasted_iota(jnp.int32, (stack * BAND, 2 * BAND), 0) % BAND
    kj = lax.broadcasted_iota(jnp.int32, (stack * BAND, 2 * BAND), 1) - BAND
    diff = qi - kj
    return (diff >= 0) & (diff <= max_dist) & (kj + block_start >= 0)


def _split_heads(a):
    a32 = a.astype(F32)
    lo = _lane(a.shape) < HEAD_DIM
    return jnp.where(lo, a32, 0.0).astype(BF16), jnp.where(lo, 0.0, a32).astype(BF16)


def _pair_dots(q, k):
    return [_dot_t(q, kh) for kh in _split_heads(k)]


def _pair_softmax(scores, mask):
    out = []
    for s in scores:
        s = jnp.where(mask, s, NEG_INF)
        m = jnp.max(s, axis=-1, keepdims=True)
        p = jnp.exp(s - m)
        out.append((m, p, jnp.sum(p, axis=-1, keepdims=True)))
    return out


def _pair_pv(p0, p1, v):
    pc = jnp.concatenate([p0.astype(BF16), p1.astype(BF16)], axis=1)
    return _dot(pc, jnp.concatenate(_split_heads(v), axis=0))


def _by_head(lo_val, hi_val, shape):
    return jnp.where(_lane(shape) < HEAD_DIM, lo_val, hi_val)


def _swa_kernel(sink_ref, q_ref, k_ref, kp_ref, v_ref, vp_ref, o_ref):
    nblk = q_ref.shape[0] // BAND
    chunk_start = pl.program_id(1) * q_ref.shape[0]
    n_pairs = WIDTH_A // LANES
    def load(j):
        rows = slice(j * BAND, (j + 1) * BAND)
        prev = slice((j - 1) * BAND, j * BAND)
        q = q_ref[rows, :]
        qs = jnp.concatenate([q[:, t * LANES:(t + 1) * LANES] for t in range(n_pairs)], axis=0)
        if j == 0:
            k = jnp.concatenate([kp_ref[...], k_ref[rows, :]], axis=0)
            v = jnp.concatenate([vp_ref[...], v_ref[rows, :]], axis=0)
        else:
            k = jnp.concatenate([k_ref[prev, :], k_ref[rows, :]], axis=0)
            v = jnp.concatenate([v_ref[prev, :], v_ref[rows, :]], axis=0)
        return rows, v, _pair_dots(qs, k)

    blocks = [load(j) for j in range(nblk)]
    for j, (rows, v, scores) in enumerate(blocks):
        mask = _band_mask(SWA_MAX_DIST, chunk_start + j * BAND, n_pairs)
        stats = _pair_softmax(scores, mask)
        acc = _pair_pv(stats[0][1], stats[1][1], v)
        scales = []
        for g, (m, _, l) in enumerate(stats):
            sink = jnp.concatenate(
                [jnp.full((BAND, 1), sink_ref[SWA_GROUP * g + t], F32) for t in range(n_pairs)],
                axis=0)
            m_all = jnp.maximum(m, sink)
            keep = jnp.exp(m - m_all)
            den = l * keep + jnp.exp(sink - m_all)
            scales.append(keep / den)
        out = acc * _by_head(scales[0], scales[1], acc.shape)
        for t in range(n_pairs):
            o_ref[rows, t * LANES:(t + 1) * LANES] = out[t * BAND:(t + 1) * BAND, :].astype(o_ref.dtype)


def _swa(sinks, qa, ka, va, batch, seq, tq):
    n = qa.shape[0]
    chunks = seq // tq
    blk = tq // BAND
    cur = lambda b, c: (b * chunks + c, 0)
    prev = lambda b, c: (jnp.maximum((b * chunks + c) * blk - 1, 0), 0)
    return pl.pallas_call(
        _swa_kernel,
        grid=(batch, chunks),
        in_specs=[
            pl.BlockSpec(memory_space=pltpu.SMEM),
            pl.BlockSpec((tq, WIDTH_A), cur),
            pl.BlockSpec((tq, KV_A), cur),
            pl.BlockSpec((BAND, KV_A), prev),
            pl.BlockSpec((tq, KV_A), cur),
            pl.BlockSpec((BAND, KV_A), prev),
        ],
        out_specs=pl.BlockSpec((tq, WIDTH_A), cur),
        out_shape=jax.ShapeDtypeStruct((n, WIDTH_A), BF16),
        compiler_params=_params("parallel", "parallel"),
        name="swa",
    )(sinks, qa, ka, ka, va, va)


MOBA_UNROLL = 4
MOBA_VT_ROWS = HEAD_DIM + 16


def _moba_kernel(q_ref, qn_ref, k_ref, v_ref, o_ref,
                 kaug_ref, vt_ref, kmean_ref, acc_ref, m_ref, l_ref, s_ref, qa_ref):
    i = pl.program_id(1)
    nb = k_ref.shape[0] // MOBA_BLOCK
    n_pairs = q_ref.shape[1] // LANES
    heads = range(2 * n_pairs)
    pair_cols = [slice(p * LANES, (p + 1) * LANES) for p in range(n_pairs)]
    blk_shape = (MOBA_BLOCK, LANES)
    lane = _lane(blk_shape)
    lo = lane < HEAD_DIM

    @pl.when(i == 0)
    def _prepare():
        kmean_ref[...] = jnp.zeros_like(kmean_ref)

        def body(n, carry):
            rows = pl.ds(pl.multiple_of(n * MOBA_BLOCK, MOBA_BLOCK), MOBA_BLOCK)
            block_hot = jnp.where((lane == n) | (lane == HEAD_DIM + n), 1.0, 0.0)
            ones = jnp.ones((MOBA_VT_ROWS - HEAD_DIM, MOBA_BLOCK), F32)
            for p, cols in enumerate(pair_cols):
                kb = k_ref[rows, cols].astype(F32)
                kmean_ref[p, pl.ds(n, 1), :] = jnp.sum(kb, axis=0, keepdims=True) * (1.0 / MOBA_BLOCK)
                kaug_ref[2 * p, rows, :] = jnp.where(lo, kb, block_hot).astype(BF16)
                kaug_ref[2 * p + 1, rows, :] = jnp.where(lo, block_hot, kb).astype(BF16)
                vt = v_ref[rows, cols].astype(F32).T
                vt_ref[n, 2 * p * MOBA_VT_ROWS:(2 * p + 2) * MOBA_VT_ROWS, :] = jnp.concatenate(
                    [vt[:HEAD_DIM], ones, vt[HEAD_DIM:], ones], axis=0).astype(BF16)
            return carry

        lax.fori_loop(0, nb, body, 0)

    def select(q, own, p):
        kmean = kmean_ref[p]
        lo_m = _lane(kmean.shape) < HEAD_DIM
        sel_shape = (HEAD_DIM, MOBA_BLOCK)
        blk = lax.broadcasted_iota(jnp.int32, sel_shape, 0)
        blk_f = blk.astype(F32)
        pens = []
        for km in (jnp.where(lo_m, kmean, 0.0), jnp.where(lo_m, 0.0, kmean)):
            gate = _dot_t(km.astype(BF16), q)
            gate = jnp.where(blk < own, gate, -jnp.inf)
            sel = blk == own
            for _ in range(MOBA_TOPK):
                mx = jnp.max(gate, axis=0, keepdims=True)
                first = jnp.min(jnp.where(gate == mx, blk_f, float(HEAD_DIM)), axis=0, keepdims=True)
                pick = (blk_f == first) & (mx > -jnp.inf)
                sel = sel | pick
                gate = jnp.where(pick, -jnp.inf, gate)
            pens.append(jnp.where(sel, 0.0, NEG_INF))
        zeros = jnp.zeros(sel_shape, F32)
        q32 = q.astype(F32)
        return (jnp.where(lo, q32, jnp.concatenate([zeros, pens[0]], axis=0).T).astype(BF16),
                jnp.where(lo, jnp.concatenate([pens[1], zeros], axis=0).T, q32).astype(BF16))

    @pl.when(i == 0)
    def _first_block():
        for p, cols in enumerate(pair_cols):
            qa_ref[0, 2 * p], qa_ref[0, 2 * p + 1] = select(q_ref[:, cols], 0, p)

    ahead = [qa for p, cols in enumerate(pair_cols) for qa in select(qn_ref[:, cols], i + 1, p)]
    qas = [qa_ref[i % 2, hd] for hd in heads]
    for hd in heads:
        qa_ref[(i + 1) % 2, hd] = ahead[hd]

    acc_ref[...] = jnp.zeros_like(acc_ref)
    m_ref[...] = jnp.full_like(m_ref, NEG_INF)
    l_ref[...] = jnp.zeros_like(l_ref)

    n_trips = i // MOBA_UNROLL + 1

    def scores(t, n):
        j = jnp.minimum(t * MOBA_UNROLL + n, i)
        rows = pl.ds(pl.multiple_of(j * MOBA_BLOCK, MOBA_BLOCK), MOBA_BLOCK)
        return [_dot_t(kaug_ref[hd, rows, :], qas[hd]) for hd in heads]

    def update(j, blk_scores, masked):
        vt = vt_ref[jnp.minimum(j, i)]
        ps, alphas = [], []
        for hd, s in enumerate(blk_scores):
            if masked:
                kk = lax.broadcasted_iota(jnp.int32, s.shape, 0)
                qq = lax.broadcasted_iota(jnp.int32, s.shape, 1)
                s = jnp.where(kk <= qq, s, NEG_INF)
            m_old = m_ref[hd]
            m_new = jnp.maximum(m_old, jnp.max(s, axis=0, keepdims=True))
            alphas.append(jnp.exp2(m_old - m_new))
            ps.append(jnp.exp2(s - m_new).astype(BF16))
            m_ref[hd] = m_new
        for hd in heads:
            pv = _dot(vt[hd * MOBA_VT_ROWS:(hd + 1) * MOBA_VT_ROWS, :], ps[hd])
            acc_ref[hd] = acc_ref[hd] * alphas[hd] + pv[:HEAD_DIM, :]
            l_ref[hd] = l_ref[hd] * alphas[hd] + pv[HEAD_DIM:HEAD_DIM + 1, :]

    for n in range(MOBA_UNROLL):
        for hd, s in enumerate(scores(0, n)):
            s_ref[n, hd] = s

    def body(t, carry):
        for n in range(MOBA_UNROLL):
            cur = [s_ref[n, hd] for hd in heads]
            nxt = scores(t + 1, n)
            update(t * MOBA_UNROLL + n, cur, False)
            for hd in heads:
                s_ref[n, hd] = nxt[hd]
        return carry

    lax.fori_loop(0, n_trips - 1, body, 0)
    for last in range(MOBA_UNROLL):
        @pl.when(i % MOBA_UNROLL == last)
        def _(last=last):
            for n in range(last + 1):
                update(i - last + n, [s_ref[n, hd] for hd in heads], n == last)
    for p, cols in enumerate(pair_cols):
        out_t = jnp.concatenate(
            [acc_ref[hd] * (1.0 / l_ref[hd]) for hd in (2 * p, 2 * p + 1)], axis=0)
        o_ref[:, cols] = out_t.T.astype(o_ref.dtype)


def _moba(qb, kb, vb, batch, seq):
    n = qb.shape[0]
    nb = seq // MOBA_BLOCK
    assert nb <= HEAD_DIM, "block one-hot must fit in the other head's lanes"
    qmap = lambda b, i: (b * nb + i, 0)
    qnext = lambda b, i: (b * nb + jnp.minimum(i + 1, nb - 1), 0)
    kvmap = lambda b, i: (b, 0)
    return pl.pallas_call(
        _moba_kernel,
        grid=(batch, nb),
        in_specs=[
            pl.BlockSpec((MOBA_BLOCK, WIDTH_B), qmap),
            pl.BlockSpec((MOBA_BLOCK, WIDTH_B), qnext),
            pl.BlockSpec((seq, WIDTH_B), kvmap),
            pl.BlockSpec((seq, WIDTH_B), kvmap),
        ],
        out_specs=pl.BlockSpec((MOBA_BLOCK, WIDTH_B), qmap),
        out_shape=jax.ShapeDtypeStruct((n, WIDTH_B), BF16),
        scratch_shapes=[
            pltpu.VMEM((MOBA_HEADS, seq, LANES), BF16),
            pltpu.VMEM((nb, MOBA_HEADS * MOBA_VT_ROWS, MOBA_BLOCK), BF16),
            pltpu.VMEM((MOBA_HEADS // 2, HEAD_DIM, LANES), F32),
            pltpu.VMEM((MOBA_HEADS, HEAD_DIM, MOBA_BLOCK), F32),
            pltpu.VMEM((MOBA_HEADS, 1, MOBA_BLOCK), F32),
            pltpu.VMEM((MOBA_HEADS, 1, MOBA_BLOCK), F32),
            pltpu.VMEM((MOBA_UNROLL, MOBA_HEADS, MOBA_BLOCK, MOBA_BLOCK), F32),
            pltpu.VMEM((2, MOBA_HEADS, MOBA_BLOCK, LANES), BF16),
        ],
        compiler_params=_params("parallel", "arbitrary"),
        name="moba",
    )(qb, qb, kb, vb)


DIL_CHUNK = DIL_PATTERNS[-1][1] * BAND
DIL_UNROLL = {1: 8, 4: 8, 16: 4}


def _dil_kernel(q_ref, kp_ref, k_ref, vp_ref, v_ref, o_ref, kk_ref, vv_ref, acc_ref, m_ref, l_ref):
    c = pl.program_id(2)
    kk_ref[0:DIL_CHUNK, :] = kp_ref[...]
    kk_ref[DIL_CHUNK:, :] = k_ref[...]
    vv_ref[0:DIL_CHUNK, :] = vp_ref[...]
    vv_ref[DIL_CHUNK:, :] = v_ref[...]
    blocks = DIL_CHUNK // BAND
    shape = (BAND, LANES)

    for pi, (_, d) in enumerate(DIL_PATTERNS):
        per_class = blocks // d

        def load(t, d=d, per_class=per_class):
            r = t // per_class
            j = t % per_class
            q0 = r + BAND * d * j
            k0 = DIL_CHUNK + q0 - BAND * d
            if d == 1:
                qrows = pl.ds(pl.multiple_of(q0, BAND), BAND)
                krows = pl.ds(pl.multiple_of(k0, BAND), 2 * BAND)
            else:
                qrows = pl.ds(q0, BAND, stride=d)
                krows = pl.ds(k0, 2 * BAND, stride=d)
            scores = _pair_dots(q_ref[qrows, :].astype(BF16), kk_ref[krows, :])
            return qrows, krows, (c * per_class + j) * BAND, scores

        unroll = DIL_UNROLL[d]

        def body(t, carry, pi=pi, load=load, unroll=unroll):
            group = [load(t * unroll + u) for u in range(unroll)]
            for qrows, krows, block_start, scores in group:
                mask = _band_mask(DIL_MAX_DIST, block_start)
                (m0, p0, l0), (m1, p1, l1) = _pair_softmax(scores, mask)
                acc_ref[pi, qrows, :] = _pair_pv(p0, p1, vv_ref[krows, :])
                m_ref[pi, qrows, :] = _by_head(m0, m1, shape)
                l_ref[pi, qrows, :] = _by_head(l0, l1, shape)
            return carry

        lax.fori_loop(0, blocks // unroll, body, 0)

    def combine(t, carry):
        rows = pl.ds(pl.multiple_of(t * BAND, BAND), BAND)
        ms = [m_ref[pi, rows, :] for pi in range(len(DIL_PATTERNS))]
        m_max = functools.reduce(jnp.maximum, ms)
        num = jnp.zeros(shape, F32)
        den = jnp.zeros(shape, F32)
        for pi, m in enumerate(ms):
            e = jnp.exp(m - m_max)
            num = num + acc_ref[pi, rows, :] * e
            den = den + l_ref[pi, rows, :] * e
        o_ref[rows, :] = (num / den).astype(o_ref.dtype)
        return carry

    lax.fori_loop(0, blocks, combine, 0)


def _dilated(qc, kc, vc, batch, seq):
    n = qc.shape[0]
    assert seq % DIL_CHUNK == 0
    chunks = seq // DIL_CHUNK
    pairs = WIDTH_C // LANES
    cur = lambda b, p, c: (b * chunks + c, p)
    prev = lambda b, p, c: (b * chunks + jnp.maximum(c - 1, 0), p)
    blk = (DIL_CHUNK, LANES)
    n_pat = len(DIL_PATTERNS)
    return pl.pallas_call(
        _dil_kernel,
        grid=(batch, pairs, chunks),
        in_specs=[pl.BlockSpec(blk, cur), pl.BlockSpec(blk, prev), pl.BlockSpec(blk, cur),
                  pl.BlockSpec(blk, prev), pl.BlockSpec(blk, cur)],
        out_specs=pl.BlockSpec(blk, cur),
        out_shape=jax.ShapeDtypeStruct((n, WIDTH_C), BF16),
        scratch_shapes=[
            pltpu.VMEM((2 * DIL_CHUNK, LANES), F32),
            pltpu.VMEM((2 * DIL_CHUNK, LANES), F32),
            pltpu.VMEM((n_pat, DIL_CHUNK, LANES), F32),
            pltpu.VMEM((n_pat, DIL_CHUNK, LANES), F32),
            pltpu.VMEM((n_pat, DIL_CHUNK, LANES), F32),
        ],
        compiler_params=_params("parallel", "parallel", "parallel"),
        name="dilated",
    )(qc, kc, kc, vc, vc)


def _merged_residual(x_ref, g_ref, oa_ref, ob_ref, oc_ref, wg_ref, wa_ref, wb_ref, wc_ref, wo_ref):
    x = x_ref[...]
    d = x.shape[1]
    h = _rms(x, g_ref[...]).astype(BF16)
    merged = None
    for br, (o_ref, w_ref) in enumerate(((oa_ref, wa_ref), (ob_ref, wb_ref), (oc_ref, wc_ref))):
        gate = _sigmoid(_dot(h, wg_ref[:, br * d:(br + 1) * d]))
        y = gate * _dot(o_ref[...], w_ref[...])
        merged = y if merged is None else merged + y
    return x + _dot(merged.astype(BF16), wo_ref[...])


def _merge_kernel(*refs):
    y_ref = refs[-1]
    y_ref[...] = _merged_residual(*refs[:-1])


def _merge(x2, gain, oa, ob, oc, w_gates, w_a, w_b, w_c, w_o, tm, rider=None):
    n, d = x2.shape
    row = lambda i: (i, 0)
    full = lambda i: (0, 0)
    return _row_call(
        _merge_kernel, "merge", n // tm,
        [
            pl.BlockSpec((tm, d), row),
            pl.BlockSpec((1, d), full),
            pl.BlockSpec((tm, WIDTH_A), row),
            pl.BlockSpec((tm, WIDTH_B), row),
            pl.BlockSpec((tm, WIDTH_C), row),
            pl.BlockSpec(w_gates.shape, full),
            pl.BlockSpec(w_a.shape, full),
            pl.BlockSpec(w_b.shape, full),
            pl.BlockSpec(w_c.shape, full),
            pl.BlockSpec(w_o.shape, full),
        ],
        (x2, gain, oa, ob, oc, w_gates, w_a, w_b, w_c, w_o),
        [pl.BlockSpec((tm, d), row)],
        [jax.ShapeDtypeStruct((n, d), F32)],
        rider)


def _swiglu(h, wg, wu, wd):
    a = _dot(h, wg)
    u = _dot(h, wu)
    return _dot((a * _sigmoid(a) * u).astype(BF16), wd)


def _ffn_kernel(x_ref, g_ref, wg_ref, wu_ref, wd_ref, y_ref):
    x = x_ref[...]
    h = _rms(x, g_ref[...]).astype(BF16)
    y_ref[...] = x + _swiglu(h, wg_ref[...], wu_ref[...], wd_ref[...])


def _ffn(x2, gain, wg, wu, wd, tm, rider=None):
    n, d = x2.shape
    row = lambda i: (i, 0)
    full = lambda i: (0, 0)
    once = pl.Buffered(1)
    return _row_call(
        _ffn_kernel, "ffn", n // tm,
        [
            pl.BlockSpec((tm, d), row),
            pl.BlockSpec((1, d), full),
            pl.BlockSpec(wg.shape, full, pipeline_mode=once),
            pl.BlockSpec(wu.shape, full, pipeline_mode=once),
            pl.BlockSpec(wd.shape, full, pipeline_mode=once),
        ],
        (x2, gain, wg, wu, wd),
        [pl.BlockSpec((tm, d), row)],
        [jax.ShapeDtypeStruct((n, d), F32)],
        rider)


MOE_GROUP = 2048
MOE_PACK_BLOCK = 512
MOE_CHUNK = 128
ROUTER_ROWS = 16

def _split_bf16(a):
    hi = a.astype(BF16)
    return hi, (a - hi.astype(F32)).astype(BF16)


def _route_tile(x, g_ref, wr_ref, h_ref, rankt_ref, combt_ref, count_ref):
    tm = x.shape[0]
    tile_in_group = pl.program_id(0) % (MOE_GROUP // tm)
    h = _rms(x, g_ref[...])
    h_ref[...] = h.astype(BF16)
    h_hi, h_lo = _split_bf16(h)
    w_hi, w_lo = _split_bf16(wr_ref[...])
    logits = _dot_t(w_hi, h_hi) + (_dot_t(w_lo, h_hi) + _dot_t(w_hi, h_lo))
    row = lax.broadcasted_iota(jnp.int32, logits.shape, 0).astype(F32)
    logits = jnp.where(row < N_EXPERTS, logits, -jnp.inf)
    tops = []
    for _ in range(TOP_K):
        mx = jnp.max(logits, axis=0, keepdims=True)
        first = jnp.min(jnp.where(logits == mx, row, float(ROUTER_ROWS)), axis=0, keepdims=True)
        tops.append((mx, first))
        logits = jnp.where(row == first, -jnp.inf, logits)
    (v1, i1), (v2, i2) = tops
    e2 = jnp.exp(v2 - v1)
    w1 = 1.0 / (1.0 + e2)
    w2 = e2 / (1.0 + e2)
    sel = jnp.where(row == i1, 1.0, 0.0) + jnp.where(row == i2, 1.0, 0.0)
    comb = jnp.where(row == i1, w1, 0.0) + jnp.where(row == i2, w2, 0.0)
    combt_ref[0] = comb[:N_EXPERTS, :]

    @pl.when(tile_in_group == 0)
    def _():
        count_ref[...] = jnp.zeros_like(count_ref)

    r = lax.broadcasted_iota(jnp.int32, (tm, tm), 0)
    c = lax.broadcasted_iota(jnp.int32, (tm, tm), 1)
    before = _dot(sel.astype(BF16), jnp.where(r < c, 1.0, 0.0).astype(BF16)) + count_ref[:, :1]
    rank = jnp.where(sel > 0.0, before, -1.0)
    count_ref[...] = count_ref[...] + jnp.sum(sel, axis=1, keepdims=True)
    rankt_ref[0] = rank[:N_EXPERTS, :]


def _router_kernel(x_ref, g_ref, wr_ref, *out_and_scratch):
    _route_tile(x_ref[...], g_ref, wr_ref, *out_and_scratch)


def _router(x2, gain, w_router_pad, tm):
    n, d = x2.shape
    assert n % MOE_GROUP == 0 and MOE_GROUP % tm == 0
    per_group = MOE_GROUP // tm
    row = lambda i: (i, 0)
    return pl.pallas_call(
        _router_kernel,
        grid=(n // tm,),
        in_specs=[
            pl.BlockSpec((tm, d), row),
            pl.BlockSpec((1, d), lambda i: (0, 0)),
            pl.BlockSpec((ROUTER_ROWS, d), lambda i: (0, 0)),
        ],
        out_specs=[pl.BlockSpec((tm, d), row)]
                  + [pl.BlockSpec((1, N_EXPERTS, tm), lambda i: (i // per_group, 0, i % per_group))] * 2,
        out_shape=[jax.ShapeDtypeStruct((n, d), BF16)]
                  + [jax.ShapeDtypeStruct((n // MOE_GROUP, N_EXPERTS, MOE_GROUP), F32)] * 2,
        scratch_shapes=[pltpu.VMEM((ROUTER_ROWS, LANES), F32)],
        compiler_params=_params("arbitrary"),
        name="router",
    )(x2, gain, w_router_pad)


def _experts_kernel(h_ref, rankt_ref, combt_ref, wg_ref, wu_ref, wd_ref, y_ref,
                    xs_ref, acc_ref, col_ref, start_ref):
    e = pl.program_id(1)
    f = pl.program_id(2)
    group, d = h_ref.shape
    n_blocks = group // MOE_CHUNK
    per_pack = MOE_PACK_BLOCK // MOE_CHUNK
    blocks = [slice(b * MOE_CHUNK, (b + 1) * MOE_CHUNK) for b in range(n_blocks)]
    pack_blocks = [slice(b * MOE_PACK_BLOCK, (b + 1) * MOE_PACK_BLOCK) for b in range(group // MOE_PACK_BLOCK)]

    def chunk_rows(m, n=1):
        return pl.ds(pl.multiple_of(m * MOE_CHUNK, MOE_CHUNK), n * MOE_CHUNK)

    def n_chunks():
        return (start_ref[n_blocks] + MOE_CHUNK - 1) // MOE_CHUNK

    def overlaps(m, b):
        return ((start_ref[b * per_pack] < (m + 1) * MOE_CHUNK)
                & (start_ref[(b + 1) * per_pack] > m * MOE_CHUNK))

    @pl.when((e == 0) & (f == 0))
    def _():
        y_ref[...] = jnp.zeros_like(y_ref)

    @pl.when(f == 0)
    def _dispatch():
        rank_row = rankt_ref[0, pl.ds(e, 1), :]
        sub = lax.broadcasted_iota(jnp.int32, (LANES, LANES), 0)
        for which, row in enumerate((rank_row, combt_ref[0, pl.ds(e, 1), :])):
            stacked = jnp.zeros((LANES, LANES), F32)
            for b, tokens in enumerate(blocks):
                stacked = jnp.where(sub == b, row[:, tokens], stacked)
            col_ref[which] = stacked.T
        start_ref[0] = 0
        for b in range(n_blocks):
            last = jnp.max(rank_row[:, blocks[b]]).astype(jnp.int32) + 1
            start_ref[b + 1] = jnp.maximum(start_ref[b], last)

        def pack(m, carry):
            dest = (lax.broadcasted_iota(jnp.int32, (MOE_CHUNK, MOE_PACK_BLOCK), 0)
                    + m * MOE_CHUNK).astype(F32)
            xs_ref[chunk_rows(m), :] = jnp.zeros((MOE_CHUNK, d), BF16)
            acc_ref[chunk_rows(m), :] = jnp.zeros((MOE_CHUNK, d), F32)
            for b, tokens in enumerate(pack_blocks):
                @pl.when(overlaps(m, b))
                def _():
                    onehot = jnp.where(rank_row[:, tokens] == dest, 1.0, 0.0).astype(BF16)
                    rows = _dot(onehot, h_ref[tokens, :])
                    xs_ref[chunk_rows(m), :] = (xs_ref[chunk_rows(m), :].astype(F32) + rows).astype(BF16)
            return carry

        lax.fori_loop(0, n_chunks(), pack, 0)
        xs_ref[chunk_rows(n_chunks(), 2), :] = jnp.zeros((2 * MOE_CHUNK, d), BF16)
        acc_ref[chunk_rows(n_chunks(), 2), :] = jnp.zeros((2 * MOE_CHUNK, d), F32)

    last_slice = f == pl.num_programs(2) - 1

    @pl.when(jnp.logical_not(last_slice))
    def _():
        def expert(m, carry):
            xm = xs_ref[chunk_rows(m), :]
            acc_ref[chunk_rows(m), :] += _swiglu(xm, wg_ref[0], wu_ref[0], wd_ref[0])
            return carry

        lax.fori_loop(0, n_chunks(), expert, 0)

    @pl.when(last_slice)
    def _combine():
        def expert(m, carry):
            xm = xs_ref[chunk_rows(m), :]
            y = acc_ref[chunk_rows(m), :] + _swiglu(xm, wg_ref[0], wu_ref[0], wd_ref[0])
            y_hi, y_lo = _split_bf16(y)
            xs_ref[chunk_rows(m), :] = y_hi
            acc_ref[chunk_rows(m), :] = y_lo.astype(F32)
            return carry

        lax.fori_loop(0, n_chunks(), expert, 0)
        for b, tokens in enumerate(blocks):
            first = start_ref[b] // MOE_CHUNK
            y_hi = xs_ref[chunk_rows(first, 2), :]
            y_lo = acc_ref[chunk_rows(first, 2), :].astype(BF16)
            dest = (_lane((MOE_CHUNK, 2 * MOE_CHUNK)) + first * MOE_CHUNK).astype(F32)
            onehot = jnp.where(col_ref[0, :, b:b + 1] == dest, 1.0, 0.0).astype(BF16)
            y_ref[tokens, :] += col_ref[1, :, b:b + 1] * (_dot(onehot, y_hi) + _dot(onehot, y_lo))


def _experts(hb, rank_t, comb_t, wg, wu, wd, ff_split):
    n, d = hb.shape
    n_e, _, d_ff = wg.shape
    ff = d_ff // ff_split
    assert MOE_GROUP // MOE_CHUNK <= LANES
    group = lambda s, e, f: (s, 0)
    once = pl.Buffered(1)
    return pl.pallas_call(
        _experts_kernel,
        grid=(n // MOE_GROUP, n_e, ff_split),
        in_specs=[
            pl.BlockSpec((MOE_GROUP, d), group, pipeline_mode=once),
            pl.BlockSpec((1, N_EXPERTS, MOE_GROUP), lambda s, e, f: (s, 0, 0)),
            pl.BlockSpec((1, N_EXPERTS, MOE_GROUP), lambda s, e, f: (s, 0, 0)),
            pl.BlockSpec((1, d, ff), lambda s, e, f: (e, 0, f)),
            pl.BlockSpec((1, d, ff), lambda s, e, f: (e, 0, f)),
            pl.BlockSpec((1, ff, d), lambda s, e, f: (e, f, 0)),
        ],
        out_specs=pl.BlockSpec((MOE_GROUP, d), group, pipeline_mode=once),
        out_shape=jax.ShapeDtypeStruct((n, d), F32),
        scratch_shapes=[
            pltpu.VMEM((MOE_GROUP + 2 * MOE_CHUNK, d), BF16),
            pltpu.VMEM((MOE_GROUP + 2 * MOE_CHUNK, d), F32),
            pltpu.VMEM((2, LANES, LANES), F32),
            pltpu.SMEM((MOE_GROUP // MOE_CHUNK + 1,), jnp.int32),
        ],
        compiler_params=_params("parallel", "arbitrary", "arbitrary"),
        name="experts",
    )(hb, rank_t, comb_t, wg, wu, wd)


def _add_kernel(x_ref, y_ref, o_ref):
    o_ref[...] = x_ref[...] + y_ref[...]


def _add_norm_kernel(x_ref, y_ref, g_ref, o_ref):
    o_ref[...] = _rms(x_ref[...] + y_ref[...], g_ref[...])


def _norm_kernel(x_ref, g_ref, o_ref):
    o_ref[...] = _rms(x_ref[...], g_ref[...])


def _rowwise(body, name, tm, x2, *rest):
    n, d = x2.shape
    row = pl.BlockSpec((tm, d), lambda i: (i, 0))
    specs = [row if a.shape == (n, d) else pl.BlockSpec((1, d), lambda i: (0, 0)) for a in (x2,) + rest]
    return pl.pallas_call(
        body,
        grid=(n // tm,),
        in_specs=specs,
        out_specs=row,
        out_shape=jax.ShapeDtypeStruct((n, d), F32),
        compiler_params=_params("parallel"),
        name=name,
    )(x2, *rest)


def _rope_tables(seq):
    pos = np.arange(seq, dtype=np.float64)
    inv_freq = ROPE_THETA ** (-np.arange(0, HEAD_DIM, 2, dtype=np.float64) / HEAD_DIM)
    ang = pos[:, None] * inv_freq[None, :]
    cos, sin = np.cos(ang), np.sin(ang)
    reps = LANES // HEAD_DIM
    cos_t = np.tile(np.concatenate([cos, cos], axis=-1), (1, reps))
    sin_t = np.tile(np.concatenate([-sin, sin], axis=-1), (1, reps))
    return jnp.asarray(cos_t, F32), jnp.asarray(sin_t, F32)


def _pair_heads_by_group(w, axis):
    shape = w.shape
    split = shape[:axis] + (SWA_KV_HEADS, SWA_GROUP, HEAD_DIM) + shape[axis + 1:]
    return jnp.swapaxes(w.reshape(split), axis, axis + 1).reshape(shape)


def _row_tile(n, want):
    t = min(n, want)
    assert n % t == 0
    return t


def kernel(x, norm_mix, w_in, attn_sinks, w_br_a, w_br_b, w_br_c, w_out, norm_ffn, w_ff_gate, w_ff_up, w_ff_down, w_router, w_moe_gate, w_moe_up, w_moe_down, norm_final):
    batch, seq, d = x.shape
    n = batch * seq
    depth = norm_mix.shape[0]
    tm = _row_tile(seq, 512)
    cos_t, sin_t = _rope_tables(seq)
    x2 = x.reshape(n, d).astype(F32)
    expert_bf16 = {}

    for layer in range(depth):
        w_l = w_in[layer]
        w_qa = _pair_heads_by_group(w_l[:, :WIDTH_A], 1)
        w_qkv = jnp.concatenate([w_qa, w_l[:, WIDTH_A:QKV_COLS]], axis=1).astype(BF16)
        w_gates = w_l[:, QKV_COLS:].astype(BF16)
        gain = norm_mix[layer].reshape(1, d)

        idx = layer // 2
        is_moe = layer % 2 == 1
        feeds_moe = not is_moe and layer + 1 < depth
        nxt = (layer + 1) // 2
        flat = lambda w: w.reshape(-1, w.shape[-1])

        qa, ka, va, qb, kb, vb, qc, kc, vc = _inproj(x2, gain, w_qkv, cos_t, sin_t, seq, tm)
        oa = _swa(attn_sinks[layer], qa, ka, va, batch, seq, tm)
        ob = _moba(qb, kb, vb, batch, seq)
        oc = _dilated(qc, kc, vc, batch, seq)
        gain_f = norm_ffn[layer].reshape(1, d)
        rider = flat(w_moe_down[idx]) if is_moe else flat(w_moe_up[nxt]) if feeds_moe else None
        x2, *cast = _merge(x2, gain, oa, ob, oc, w_gates,
                           _pair_heads_by_group(w_br_a[layer], 0).astype(BF16),
                           w_br_b[layer].astype(BF16), w_br_c[layer].astype(BF16),
                           w_out[layer].astype(BF16), tm, rider)
        if cast:
            expert_bf16["down" if is_moe else "up"] = cast[0]

        if not is_moe:
            x2, *cast = _ffn(x2, gain_f, w_ff_gate[idx].astype(BF16), w_ff_up[idx].astype(BF16),
                             w_ff_down[idx].astype(BF16), tm,
                             flat(w_moe_gate[nxt]) if feeds_moe else None)
            if cast:
                expert_bf16["gate"] = cast[0]
        else:
            w_r = jnp.pad(w_router[idx].T, ((0, ROUTER_ROWS - N_EXPERTS), (0, 0)))
            hb, rank_t, comb_t = _router(x2, gain_f, w_r, tm)
            weights = [expert_bf16.pop(k).reshape(w.shape) if k in expert_bf16 else w.astype(BF16)
                       for k, w in (("gate", w_moe_gate[idx]), ("up", w_moe_up[idx]),
                                    ("down", w_moe_down[idx]))]
            y = _experts(hb, rank_t, comb_t, *weights, 2)
            if layer == depth - 1:
                out = _rowwise(_add_norm_kernel, "add_norm", tm, x2, y, norm_final.reshape(1, d))
                return out.reshape(batch, seq, d)
            x2 = _rowwise(_add_kernel, "add", tm, x2, y)

    return _rowwise(_norm_kernel, "final_norm", tm, x2, norm_final.reshape(1, d)).reshape(batch, seq, d)
```

```python
import functools

import jax
import jax.numpy as jnp
import numpy as np
from jax import lax
from jax.experimental import pallas as pl
from jax.experimental.pallas import tpu as pltpu

F32 = jnp.float32
BF16 = jnp.bfloat16

HEAD_DIM = 64
ATTN_SCALE = HEAD_DIM ** -0.5
LOG2_E = 1.4426950408889634
ROPE_THETA = 10000.0
RMS_EPS = 1e-5
NEG_INF = -1e30
BAND = 128
SWA_Q_HEADS, SWA_KV_HEADS = 6, 2
SWA_GROUP = SWA_Q_HEADS // SWA_KV_HEADS
SWA_MAX_DIST = 127
MOBA_HEADS, MOBA_BLOCK, MOBA_TOPK = 4, 256, 3
DIL_HEADS = 6
DIL_PATTERNS = ((128, 1), (512, 4), (2048, 16))
DIL_MAX_DIST = 128
N_EXPERTS, TOP_K = 8, 2

LANES = 128
VMEM_LIMIT = 56 * 1024 * 1024

WIDTH_A = SWA_Q_HEADS * HEAD_DIM
KV_A = SWA_KV_HEADS * HEAD_DIM
WIDTH_B = MOBA_HEADS * HEAD_DIM
WIDTH_C = DIL_HEADS * HEAD_DIM
QKV_COLS = WIDTH_A + 2 * KV_A + 3 * WIDTH_B + 3 * WIDTH_C


def _params(*sem):
    return pltpu.CompilerParams(dimension_semantics=sem, vmem_limit_bytes=VMEM_LIMIT)


def _row_call(body, name, steps, in_specs, args, out_specs, out_shapes, rider=None):
    in_specs, out_specs, out_shapes, args = list(in_specs), list(out_specs), list(out_shapes), list(args)
    if rider is not None:
        rows, cols = rider.shape
        assert rows % steps == 0
        n_in, n_out, inner = len(in_specs), len(out_specs), body
        slab = pl.BlockSpec((rows // steps, cols), lambda i: (i, 0))

        def body(*refs):
            inner(*refs[:n_in], *refs[n_in + 1:n_in + 1 + n_out])
            refs[-1][...] = refs[n_in][...].astype(BF16)

        in_specs.append(slab)
        args.append(rider)
        out_specs.append(slab)
        out_shapes.append(jax.ShapeDtypeStruct((rows, cols), BF16))
    return pl.pallas_call(
        body, grid=(steps,), in_specs=in_specs, out_specs=out_specs, out_shape=out_shapes,
        compiler_params=_params("parallel"), name=name,
    )(*args)


def _rms(x, gain):
    ms = jnp.mean(x * x, axis=-1, keepdims=True)
    return x * lax.rsqrt(ms + RMS_EPS) * gain


def _dot(a, b):
    return jnp.dot(a, b, preferred_element_type=F32)


def _dot_t(a, b):
    return lax.dot_general(a, b, (((1,), (1,)), ((), ())), preferred_element_type=F32)


def _sigmoid(z):
    return 1.0 / (1.0 + jnp.exp(-z))


def _lane(shape):
    return lax.broadcasted_iota(jnp.int32, shape, len(shape) - 1)


def _inproj_kernel(x_ref, g_ref, w_ref, cos_ref, sin_ref,
                   qa_ref, ka_ref, va_ref, qb_ref, kb_ref, vb_ref, qc_ref, kc_ref, vc_ref):
    tm = x_ref.shape[0]
    h = _rms(x_ref[...], g_ref[...]).astype(BF16)
    cos = cos_ref[...]
    sin = sin_ref[...]
    first_half = (_lane((tm, LANES)) & (HEAD_DIM // 2)) == 0

    def rope(z):
        rot = jnp.where(first_half,
                        pltpu.roll(z, LANES - HEAD_DIM // 2, 1),
                        pltpu.roll(z, HEAD_DIM // 2, 1))
        return z * cos + rot * sin

    z_all = _dot(h, w_ref[...])

    def proj(c0, width):
        return z_all[:, c0:c0 + width]

    def store(ref, z, roped, scale, dtype):
        for t in range(z.shape[1] // LANES):
            zt = z[:, t * LANES:(t + 1) * LANES]
            if roped:
                zt = rope(zt)
            if scale is not None:
                zt = zt * scale
            ref[:, t * LANES:(t + 1) * LANES] = zt.astype(dtype)

    c = 0
    store(qa_ref, proj(c, WIDTH_A), True, ATTN_SCALE, BF16); c += WIDTH_A
    kv = proj(c, 2 * KV_A); c += 2 * KV_A
    store(ka_ref, kv[:, :KV_A], True, None, BF16)
    store(va_ref, kv[:, KV_A:], False, None, BF16)
    store(qb_ref, proj(c, WIDTH_B), True, ATTN_SCALE * LOG2_E, BF16); c += WIDTH_B
    store(kb_ref, proj(c, WIDTH_B), True, None, BF16); c += WIDTH_B
    store(vb_ref, proj(c, WIDTH_B), False, None, BF16); c += WIDTH_B
    store(qc_ref, proj(c, WIDTH_C), True, ATTN_SCALE, F32); c += WIDTH_C
    store(kc_ref, proj(c, WIDTH_C), True, None, F32); c += WIDTH_C
    store(vc_ref, proj(c, WIDTH_C), False, None, F32)


def _inproj(x2, gain, w_qkv, cos_t, sin_t, seq, tm, rider=None):
    n, d = x2.shape
    tiles_per_seq = seq // tm
    row = lambda i: (i, 0)
    widths = (WIDTH_A, KV_A, KV_A, WIDTH_B, WIDTH_B, WIDTH_B, WIDTH_C, WIDTH_C, WIDTH_C)
    dtypes = (BF16,) * 6 + (F32,) * 3
    return _row_call(
        _inproj_kernel, "inproj", n // tm,
        [
            pl.BlockSpec((tm, d), row),
            pl.BlockSpec((1, d), lambda i: (0, 0)),
            pl.BlockSpec((d, QKV_COLS), lambda i: (0, 0)),
            pl.BlockSpec((tm, LANES), lambda i: (i % tiles_per_seq, 0)),
            pl.BlockSpec((tm, LANES), lambda i: (i % tiles_per_seq, 0)),
        ],
        (x2, gain, w_qkv, cos_t, sin_t),
        [pl.BlockSpec((tm, w), row) for w in widths],
        [jax.ShapeDtypeStruct((n, w), dt) for w, dt in zip(widths, dtypes)],
        rider)


def _band_mask(max_dist, block_start, stack=1):
    qi = lax.broadcasted_iota(jnp.int32, (stack * BAND, 2 * BAND), 0) % BAND
    kj = lax.broadcasted_iota(jnp.int32, (stack * BAND, 2 * BAND), 1) - BAND
    diff = qi - kj
    return (diff >= 0) & (diff <= max_dist) & (kj + block_start >= 0)


def _split_heads(a):
    a32 = a.astype(F32)
    lo = _lane(a.shape) < HEAD_DIM
    return jnp.where(lo, a32, 0.0).astype(BF16), jnp.where(lo, 0.0, a32).astype(BF16)


def _pair_dots(q, k):
    return [_dot_t(q, kh) for kh in _split_heads(k)]


def _pair_softmax(scores, mask):
    out = []
    for s in scores:
        s = jnp.where(mask, s, NEG_INF)
        m = jnp.max(s, axis=-1, keepdims=True)
        p = jnp.exp(s - m)
        out.append((m, p, jnp.sum(p, axis=-1, keepdims=True)))
    return out


def _pair_pv(p0, p1, v):
    pc = jnp.concatenate([p0.astype(BF16), p1.astype(BF16)], axis=1)
    return _dot(pc, jnp.concatenate(_split_heads(v), axis=0))


def _by_head(lo_val, hi_val, shape):
    return jnp.where(_lane(shape) < HEAD_DIM, lo_val, hi_val)


def _swa_kernel(sink_ref, q_ref, k_ref, kp_ref, v_ref, vp_ref, o_ref):
    nblk = q_ref.shape[0] // BAND
    chunk_start = pl.program_id(1) * q_ref.shape[0]
    n_pairs = WIDTH_A // LANES
    def load(j):
        rows = slice(j * BAND, (j + 1) * BAND)
        prev = slice((j - 1) * BAND, j * BAND)
        q = q_ref[rows, :]
        qs = jnp.concatenate([q[:, t * LANES:(t + 1) * LANES] for t in range(n_pairs)], axis=0)
        if j == 0:
            k = jnp.concatenate([kp_ref[...], k_ref[rows, :]], axis=0)
            v = jnp.concatenate([vp_ref[...], v_ref[rows, :]], axis=0)
        else:
            k = jnp.concatenate([k_ref[prev, :], k_ref[rows, :]], axis=0)
            v = jnp.concatenate([v_ref[prev, :], v_ref[rows, :]], axis=0)
        return rows, v, _pair_dots(qs, k)

    blocks = [load(j) for j in range(nblk)]
    for j, (rows, v, scores) in enumerate(blocks):
        mask = _band_mask(SWA_MAX_DIST, chunk_start + j * BAND, n_pairs)
        stats = _pair_softmax(scores, mask)
        acc = _pair_pv(stats[0][1], stats[1][1], v)
        scales = []
        for g, (m, _, l) in enumerate(stats):
            sink = jnp.concatenate(
                [jnp.full((BAND, 1), sink_ref[SWA_GROUP * g + t], F32) for t in range(n_pairs)],
                axis=0)
            m_all = jnp.maximum(m, sink)
            keep = jnp.exp(m - m_all)
            den = l * keep + jnp.exp(sink - m_all)
            scales.append(keep / den)
        out = acc * _by_head(scales[0], scales[1], acc.shape)
        for t in range(n_pairs):
            o_ref[rows, t * LANES:(t + 1) * LANES] = out[t * BAND:(t + 1) * BAND, :].astype(o_ref.dtype)


def _swa(sinks, qa, ka, va, batch, seq, tq):
    n = qa.shape[0]
    chunks = seq // tq
    blk = tq // BAND
    cur = lambda b, c: (b * chunks + c, 0)
    prev = lambda b, c: (jnp.maximum((b * chunks + c) * blk - 1, 0), 0)
    return pl.pallas_call(
        _swa_kernel,
        grid=(batch, chunks),
        in_specs=[
            pl.BlockSpec(memory_space=pltpu.SMEM),
            pl.BlockSpec((tq, WIDTH_A), cur),
            pl.BlockSpec((tq, KV_A), cur),
            pl.BlockSpec((BAND, KV_A), prev),
            pl.BlockSpec((tq, KV_A), cur),
            pl.BlockSpec((BAND, KV_A), prev),
        ],
        out_specs=pl.BlockSpec((tq, WIDTH_A), cur),
        out_shape=jax.ShapeDtypeStruct((n, WIDTH_A), BF16),
        compiler_params=_params("parallel", "parallel"),
        name="swa",
    )(sinks, qa, ka, ka, va, va)


MOBA_UNROLL = 4
MOBA_VT_ROWS = HEAD_DIM + 16


def _moba_kernel(q_ref, qn_ref, k_ref, v_ref, o_ref,
                 kaug_ref, vt_ref, kmean_ref, acc_ref, m_ref, l_ref, s_ref, qa_ref):
    i = pl.program_id(1)
    nb = k_ref.shape[0] // MOBA_BLOCK
    n_pairs = q_ref.shape[1] // LANES
    heads = range(2 * n_pairs)
    pair_cols = [slice(p * LANES, (p + 1) * LANES) for p in range(n_pairs)]
    blk_shape = (MOBA_BLOCK, LANES)
    lane = _lane(blk_shape)
    lo = lane < HEAD_DIM

    @pl.when(i == 0)
    def _prepare():
        kmean_ref[...] = jnp.zeros_like(kmean_ref)

        def body(n, carry):
            rows = pl.ds(pl.multiple_of(n * MOBA_BLOCK, MOBA_BLOCK), MOBA_BLOCK)
            block_hot = jnp.where((lane == n) | (lane == HEAD_DIM + n), 1.0, 0.0)
            ones = jnp.ones((MOBA_VT_ROWS - HEAD_DIM, MOBA_BLOCK), F32)
            for p, cols in enumerate(pair_cols):
                kb = k_ref[rows, cols].astype(F32)
                kmean_ref[p, pl.ds(n, 1), :] = jnp.sum(kb, axis=0, keepdims=True) * (1.0 / MOBA_BLOCK)
                kaug_ref[2 * p, rows, :] = jnp.where(lo, kb, block_hot).astype(BF16)
                kaug_ref[2 * p + 1, rows, :] = jnp.where(lo, block_hot, kb).astype(BF16)
                vt = v_ref[rows, cols].astype(F32).T
                vt_ref[n, 2 * p * MOBA_VT_ROWS:(2 * p + 2) * MOBA_VT_ROWS, :] = jnp.concatenate(
                    [vt[:HEAD_DIM], ones, vt[HEAD_DIM:], ones], axis=0).astype(BF16)
            return carry

        lax.fori_loop(0, nb, body, 0)

    def select(q, own, p):
        kmean = kmean_ref[p]
        lo_m = _lane(kmean.shape) < HEAD_DIM
        sel_shape = (HEAD_DIM, MOBA_BLOCK)
        blk = lax.broadcasted_iota(jnp.int32, sel_shape, 0)
        blk_f = blk.astype(F32)
        pens = []
        for km in (jnp.where(lo_m, kmean, 0.0), jnp.where(lo_m, 0.0, kmean)):
            gate = _dot_t(km.astype(BF16), q)
            gate = jnp.where(blk < own, gate, -jnp.inf)
            sel = blk == own
            for _ in range(MOBA_TOPK):
                mx = jnp.max(gate, axis=0, keepdims=True)
                first = jnp.min(jnp.where(gate == mx, blk_f, float(HEAD_DIM)), axis=0, keepdims=True)
                pick = (blk_f == first) & (mx > -jnp.inf)
                sel = sel | pick
                gate = jnp.where(pick, -jnp.inf, gate)
            pens.append(jnp.where(sel, 0.0, NEG_INF))
        zeros = jnp.zeros(sel_shape, F32)
        q32 = q.astype(F32)
        return (jnp.where(lo, q32, jnp.concatenate([zeros, pens[0]], axis=0).T).astype(BF16),
                jnp.where(lo, jnp.concatenate([pens[1], zeros], axis=0).T, q32).astype(BF16))

    @pl.when(i == 0)
    def _first_block():
        for p, cols in enumerate(pair_cols):
            qa_ref[0, 2 * p], qa_ref[0, 2 * p + 1] = select(q_ref[:, cols], 0, p)

    ahead = [qa for p, cols in enumerate(pair_cols) for qa in select(qn_ref[:, cols], i + 1, p)]
    qas = [qa_ref[i % 2, hd] for hd in heads]
    for hd in heads:
        qa_ref[(i + 1) % 2, hd] = ahead[hd]

    acc_ref[...] = jnp.zeros_like(acc_ref)
    m_ref[...] = jnp.full_like(m_ref, NEG_INF)
    l_ref[...] = jnp.zeros_like(l_ref)

    n_trips = i // MOBA_UNROLL + 1

    def scores(t, n):
        j = jnp.minimum(t * MOBA_UNROLL + n, i)
        rows = pl.ds(pl.multiple_of(j * MOBA_BLOCK, MOBA_BLOCK), MOBA_BLOCK)
        return [_dot_t(kaug_ref[hd, rows, :], qas[hd]) for hd in heads]

    def update(j, blk_scores, masked):
        vt = vt_ref[jnp.minimum(j, i)]
        ps, alphas = [], []
        for hd, s in enumerate(blk_scores):
            if masked:
                kk = lax.broadcasted_iota(jnp.int32, s.shape, 0)
                qq = lax.broadcasted_iota(jnp.int32, s.shape, 1)
                s = jnp.where(kk <= qq, s, NEG_INF)
            m_old = m_ref[hd]
            m_new = jnp.maximum(m_old, jnp.max(s, axis=0, keepdims=True))
            alphas.append(jnp.exp2(m_old - m_new))
            ps.append(jnp.exp2(s - m_new).astype(BF16))
            m_ref[hd] = m_new
        for hd in heads:
            pv = _dot(vt[hd * MOBA_VT_ROWS:(hd + 1) * MOBA_VT_ROWS, :], ps[hd])
            acc_ref[hd] = acc_ref[hd] * alphas[hd] + pv[:HEAD_DIM, :]
            l_ref[hd] = l_ref[hd] * alphas[hd] + pv[HEAD_DIM:HEAD_DIM + 1, :]

    for n in range(MOBA_UNROLL):
        for hd, s in enumerate(scores(0, n)):
            s_ref[n, hd] = s

    def body(t, carry):
        for n in range(MOBA_UNROLL):
            cur = [s_ref[n, hd] for hd in heads]
            nxt = scores(t + 1, n)
            update(t * MOBA_UNROLL + n, cur, False)
            for hd in heads:
                s_ref[n, hd] = nxt[hd]
        return carry

    lax.fori_loop(0, n_trips - 1, body, 0)
    for last in range(MOBA_UNROLL):
        @pl.when(i % MOBA_UNROLL == last)
        def _(last=last):
            for n in range(last + 1):
                update(i - last + n, [s_ref[n, hd] for hd in heads], n == last)
    for p, cols in enumerate(pair_cols):
        out_t = jnp.concatenate(
            [acc_ref[hd] * (1.0 / l_ref[hd]) for hd in (2 * p, 2 * p + 1)], axis=0)
        o_ref[:, cols] = out_t.T.astype(o_ref.dtype)


def _moba(qb, kb, vb, batch, seq):
    n = qb.shape[0]
    nb = seq // MOBA_BLOCK
    assert nb <= HEAD_DIM, "block one-hot must fit in the other head's lanes"
    qmap = lambda b, i: (b * nb + i, 0)
    qnext = lambda b, i: (b * nb + jnp.minimum(i + 1, nb - 1), 0)
    kvmap = lambda b, i: (b, 0)
    return pl.pallas_call(
        _moba_kernel,
        grid=(batch, nb),
        in_specs=[
            pl.BlockSpec((MOBA_BLOCK, WIDTH_B), qmap),
            pl.BlockSpec((MOBA_BLOCK, WIDTH_B), qnext),
            pl.BlockSpec((seq, WIDTH_B), kvmap),
            pl.BlockSpec((seq, WIDTH_B), kvmap),
        ],
        out_specs=pl.BlockSpec((MOBA_BLOCK, WIDTH_B), qmap),
        out_shape=jax.ShapeDtypeStruct((n, WIDTH_B), BF16),
        scratch_shapes=[
            pltpu.VMEM((MOBA_HEADS, seq, LANES), BF16),
            pltpu.VMEM((nb, MOBA_HEADS * MOBA_VT_ROWS, MOBA_BLOCK), BF16),
            pltpu.VMEM((MOBA_HEADS // 2, HEAD_DIM, LANES), F32),
            pltpu.VMEM((MOBA_HEADS, HEAD_DIM, MOBA_BLOCK), F32),
            pltpu.VMEM((MOBA_HEADS, 1, MOBA_BLOCK), F32),
            pltpu.VMEM((MOBA_HEADS, 1, MOBA_BLOCK), F32),
            pltpu.VMEM((MOBA_UNROLL, MOBA_HEADS, MOBA_BLOCK, MOBA_BLOCK), F32),
            pltpu.VMEM((2, MOBA_HEADS, MOBA_BLOCK, LANES), BF16),
        ],
        compiler_params=_params("parallel", "arbitrary"),
        name="moba",
    )(qb, qb, kb, vb)


DIL_CHUNK = DIL_PATTERNS[-1][1] * BAND
DIL_UNROLL = {1: 8, 4: 8, 16: 4}


def _dil_kernel(q_ref, kp_ref, k_ref, vp_ref, v_ref, o_ref, kk_ref, vv_ref, acc_ref, m_ref, l_ref):
    c = pl.program_id(2)
    kk_ref[0:DIL_CHUNK, :] = kp_ref[...]
    kk_ref[DIL_CHUNK:, :] = k_ref[...]
    vv_ref[0:DIL_CHUNK, :] = vp_ref[...]
    vv_ref[DIL_CHUNK:, :] = v_ref[...]
    blocks = DIL_CHUNK // BAND
    shape = (BAND, LANES)

    for pi, (_, d) in enumerate(DIL_PATTERNS):
        per_class = blocks // d

        def load(t, d=d, per_class=per_class):
            r = t // per_class
            j = t % per_class
            q0 = r + BAND * d * j
            k0 = DIL_CHUNK + q0 - BAND * d
            if d == 1:
                qrows = pl.ds(pl.multiple_of(q0, BAND), BAND)
                krows = pl.ds(pl.multiple_of(k0, BAND), 2 * BAND)
            else:
                qrows = pl.ds(q0, BAND, stride=d)
                krows = pl.ds(k0, 2 * BAND, stride=d)
            scores = _pair_dots(q_ref[qrows, :].astype(BF16), kk_ref[krows, :])
            return qrows, krows, (c * per_class + j) * BAND, scores

        unroll = DIL_UNROLL[d]

        def body(t, carry, pi=pi, load=load, unroll=unroll):
            group = [load(t * unroll + u) for u in range(unroll)]
            for qrows, krows, block_start, scores in group:
                mask = _band_mask(DIL_MAX_DIST, block_start)
                (m0, p0, l0), (m1, p1, l1) = _pair_softmax(scores, mask)
                acc_ref[pi, qrows, :] = _pair_pv(p0, p1, vv_ref[krows, :])
                m_ref[pi, qrows, :] = _by_head(m0, m1, shape)
                l_ref[pi, qrows, :] = _by_head(l0, l1, shape)
            return carry

        lax.fori_loop(0, blocks // unroll, body, 0)

    def combine(t, carry):
        rows = pl.ds(pl.multiple_of(t * BAND, BAND), BAND)
        ms = [m_ref[pi, rows, :] for pi in range(len(DIL_PATTERNS))]
        m_max = functools.reduce(jnp.maximum, ms)
        num = jnp.zeros(shape, F32)
        den = jnp.zeros(shape, F32)
        for pi, m in enumerate(ms):
            e = jnp.exp(m - m_max)
            num = num + acc_ref[pi, rows, :] * e
            den = den + l_ref[pi, rows, :] * e
        o_ref[rows, :] = (num / den).astype(o_ref.dtype)
        return carry

    lax.fori_loop(0, blocks, combine, 0)


def _dilated(qc, kc, vc, batch, seq):
    n = qc.shape[0]
    assert seq % DIL_CHUNK == 0
    chunks = seq // DIL_CHUNK
    pairs = WIDTH_C // LANES
    cur = lambda b, p, c: (b * chunks + c, p)
    prev = lambda b, p, c: (b * chunks + jnp.maximum(c - 1, 0), p)
    blk = (DIL_CHUNK, LANES)
    n_pat = len(DIL_PATTERNS)
    return pl.pallas_call(
        _dil_kernel,
        grid=(batch, pairs, chunks),
        in_specs=[pl.BlockSpec(blk, cur), pl.BlockSpec(blk, prev), pl.BlockSpec(blk, cur),
                  pl.BlockSpec(blk, prev), pl.BlockSpec(blk, cur)],
        out_specs=pl.BlockSpec(blk, cur),
        out_shape=jax.ShapeDtypeStruct((n, WIDTH_C), BF16),
        scratch_shapes=[
            pltpu.VMEM((2 * DIL_CHUNK, LANES), F32),
            pltpu.VMEM((2 * DIL_CHUNK, LANES), F32),
            pltpu.VMEM((n_pat, DIL_CHUNK, LANES), F32),
            pltpu.VMEM((n_pat, DIL_CHUNK, LANES), F32),
            pltpu.VMEM((n_pat, DIL_CHUNK, LANES), F32),
        ],
        compiler_params=_params("parallel", "parallel", "parallel"),
        name="dilated",
    )(qc, kc, kc, vc, vc)


def _merged_residual(x_ref, g_ref, oa_ref, ob_ref, oc_ref, wg_ref, wa_ref, wb_ref, wc_ref, wo_ref):
    x = x_ref[...]
    d = x.shape[1]
    h = _rms(x, g_ref[...]).astype(BF16)
    merged = None
    for br, (o_ref, w_ref) in enumerate(((oa_ref, wa_ref), (ob_ref, wb_ref), (oc_ref, wc_ref))):
        gate = _sigmoid(_dot(h, wg_ref[:, br * d:(br + 1) * d]))
        y = gate * _dot(o_ref[...], w_ref[...])
        merged = y if merged is None else merged + y
    return x + _dot(merged.astype(BF16), wo_ref[...])


def _merge_kernel(*refs):
    y_ref = refs[-1]
    y_ref[...] = _merged_residual(*refs[:-1])


def _merge(x2, gain, oa, ob, oc, w_gates, w_a, w_b, w_c, w_o, tm, rider=None):
    n, d = x2.shape
    row = lambda i: (i, 0)
    full = lambda i: (0, 0)
    return _row_call(
        _merge_kernel, "merge", n // tm,
        [
            pl.BlockSpec((tm, d), row),
            pl.BlockSpec((1, d), full),
            pl.BlockSpec((tm, WIDTH_A), row),
            pl.BlockSpec((tm, WIDTH_B), row),
            pl.BlockSpec((tm, WIDTH_C), row),
            pl.BlockSpec(w_gates.shape, full),
            pl.BlockSpec(w_a.shape, full),
            pl.BlockSpec(w_b.shape, full),
            pl.BlockSpec(w_c.shape, full),
            pl.BlockSpec(w_o.shape, full),
        ],
        (x2, gain, oa, ob, oc, w_gates, w_a, w_b, w_c, w_o),
        [pl.BlockSpec((tm, d), row)],
        [jax.ShapeDtypeStruct((n, d), F32)],
        rider)


def _swiglu(h, wg, wu, wd):
    a = _dot(h, wg)
    u = _dot(h, wu)
    return _dot((a * _sigmoid(a) * u).astype(BF16), wd)


def _ffn_kernel(x_ref, g_ref, wg_ref, wu_ref, wd_ref, y_ref):
    x = x_ref[...]
    h = _rms(x, g_ref[...]).astype(BF16)
    y_ref[...] = x + _swiglu(h, wg_ref[...], wu_ref[...], wd_ref[...])


def _ffn(x2, gain, wg, wu, wd, tm, rider=None):
    n, d = x2.shape
    row = lambda i: (i, 0)
    full = lambda i: (0, 0)
    once = pl.Buffered(1)
    return _row_call(
        _ffn_kernel, "ffn", n // tm,
        [
            pl.BlockSpec((tm, d), row),
            pl.BlockSpec((1, d), full),
            pl.BlockSpec(wg.shape, full, pipeline_mode=once),
            pl.BlockSpec(wu.shape, full, pipeline_mode=once),
            pl.BlockSpec(wd.shape, full, pipeline_mode=once),
        ],
        (x2, gain, wg, wu, wd),
        [pl.BlockSpec((tm, d), row)],
        [jax.ShapeDtypeStruct((n, d), F32)],
        rider)


MOE_GROUP = 2048
MOE_PACK_BLOCK = 512
MOE_CHUNK = 128
ROUTER_ROWS = 16

def _split_bf16(a):
    hi = a.astype(BF16)
    return hi, (a - hi.astype(F32)).astype(BF16)


def _route_tile(x, g_ref, wr_ref, h_ref, rankt_ref, combt_ref, count_ref):
    tm = x.shape[0]
    tile_in_group = pl.program_id(0) % (MOE_GROUP // tm)
    h = _rms(x, g_ref[...])
    h_ref[...] = h.astype(BF16)
    h_hi, h_lo = _split_bf16(h)
    w_hi, w_lo = _split_bf16(wr_ref[...])
    logits = _dot_t(w_hi, h_hi) + (_dot_t(w_lo, h_hi) + _dot_t(w_hi, h_lo))
    row = lax.broadcasted_iota(jnp.int32, logits.shape, 0).astype(F32)
    logits = jnp.where(row < N_EXPERTS, logits, -jnp.inf)
    tops = []
    for _ in range(TOP_K):
        mx = jnp.max(logits, axis=0, keepdims=True)
        first = jnp.min(jnp.where(logits == mx, row, float(ROUTER_ROWS)), axis=0, keepdims=True)
        tops.append((mx, first))
        logits = jnp.where(row == first, -jnp.inf, logits)
    (v1, i1), (v2, i2) = tops
    e2 = jnp.exp(v2 - v1)
    w1 = 1.0 / (1.0 + e2)
    w2 = e2 / (1.0 + e2)
    sel = jnp.where(row == i1, 1.0, 0.0) + jnp.where(row == i2, 1.0, 0.0)
    comb = jnp.where(row == i1, w1, 0.0) + jnp.where(row == i2, w2, 0.0)
    combt_ref[0] = comb[:N_EXPERTS, :]

    @pl.when(tile_in_group == 0)
    def _():
        count_ref[...] = jnp.zeros_like(count_ref)

    r = lax.broadcasted_iota(jnp.int32, (tm, tm), 0)
    c = lax.broadcasted_iota(jnp.int32, (tm, tm), 1)
    before = _dot(sel.astype(BF16), jnp.where(r < c, 1.0, 0.0).astype(BF16)) + count_ref[:, :1]
    rank = jnp.where(sel > 0.0, before, -1.0)
    count_ref[...] = count_ref[...] + jnp.sum(sel, axis=1, keepdims=True)
    rankt_ref[0] = rank[:N_EXPERTS, :]


def _router_kernel(x_ref, g_ref, wr_ref, *out_and_scratch):
    _route_tile(x_ref[...], g_ref, wr_ref, *out_and_scratch)


def _router(x2, gain, w_router_pad, tm):
    n, d = x2.shape
    assert n % MOE_GROUP == 0 and MOE_GROUP % tm == 0
    per_group = MOE_GROUP // tm
    row = lambda i: (i, 0)
    return pl.pallas_call(
        _router_kernel,
        grid=(n // tm,),
        in_specs=[
            pl.BlockSpec((tm, d), row),
            pl.BlockSpec((1, d), lambda i: (0, 0)),
            pl.BlockSpec((ROUTER_ROWS, d), lambda i: (0, 0)),
        ],
        out_specs=[pl.BlockSpec((tm, d), row)]
                  + [pl.BlockSpec((1, N_EXPERTS, tm), lambda i: (i // per_group, 0, i % per_group))] * 2,
        out_shape=[jax.ShapeDtypeStruct((n, d), BF16)]
                  + [jax.ShapeDtypeStruct((n // MOE_GROUP, N_EXPERTS, MOE_GROUP), F32)] * 2,
        scratch_shapes=[pltpu.VMEM((ROUTER_ROWS, LANES), F32)],
        compiler_params=_params("arbitrary"),
        name="router",
    )(x2, gain, w_router_pad)


def _experts_kernel(x_hbm, h_ref, rankt_ref, combt_ref, wg_ref, wu_ref, wd_ref, y_ref,
                    xs_ref, acc_ref, col_ref, start_ref, sem):
    e = pl.program_id(1)
    f = pl.program_id(2)
    group, d = h_ref.shape
    n_blocks = group // MOE_CHUNK
    per_pack = MOE_PACK_BLOCK // MOE_CHUNK
    blocks = [slice(b * MOE_CHUNK, (b + 1) * MOE_CHUNK) for b in range(n_blocks)]
    pack_blocks = [slice(b * MOE_PACK_BLOCK, (b + 1) * MOE_PACK_BLOCK) for b in range(group // MOE_PACK_BLOCK)]

    def chunk_rows(m, n=1):
        return pl.ds(pl.multiple_of(m * MOE_CHUNK, MOE_CHUNK), n * MOE_CHUNK)

    def n_chunks():
        return (start_ref[n_blocks] + MOE_CHUNK - 1) // MOE_CHUNK

    def overlaps(m, b):
        return ((start_ref[b * per_pack] < (m + 1) * MOE_CHUNK)
                & (start_ref[(b + 1) * per_pack] > m * MOE_CHUNK))

    def residual_copy():
        rows = pl.ds(pl.multiple_of(pl.program_id(0) * group, group), group)
        return pltpu.make_async_copy(x_hbm.at[rows, :], y_ref, sem)

    @pl.when((e == 0) & (f == 0))
    def _():
        residual_copy().start()

    @pl.when(f == 0)
    def _dispatch():
        rank_row = rankt_ref[0, pl.ds(e, 1), :]
        sub = lax.broadcasted_iota(jnp.int32, (LANES, LANES), 0)
        for which, row in enumerate((rank_row, combt_ref[0, pl.ds(e, 1), :])):
            stacked = jnp.zeros((LANES, LANES), F32)
            for b, tokens in enumerate(blocks):
                stacked = jnp.where(sub == b, row[:, tokens], stacked)
            col_ref[which] = stacked.T
        start_ref[0] = 0
        for b in range(n_blocks):
            last = jnp.max(rank_row[:, blocks[b]]).astype(jnp.int32) + 1
            start_ref[b + 1] = jnp.maximum(start_ref[b], last)

        def pack(m, carry):
            dest = (lax.broadcasted_iota(jnp.int32, (MOE_CHUNK, MOE_PACK_BLOCK), 0)
                    + m * MOE_CHUNK).astype(F32)
            xs_ref[chunk_rows(m), :] = jnp.zeros((MOE_CHUNK, d), BF16)
            acc_ref[chunk_rows(m), :] = jnp.zeros((MOE_CHUNK, d), F32)
            for b, tokens in enumerate(pack_blocks):
                @pl.when(overlaps(m, b))
                def _():
                    onehot = jnp.where(rank_row[:, tokens] == dest, 1.0, 0.0).astype(BF16)
                    rows = _dot(onehot, h_ref[tokens, :])
                    xs_ref[chunk_rows(m), :] = (xs_ref[chunk_rows(m), :].astype(F32) + rows).astype(BF16)
            return carry

        lax.fori_loop(0, n_chunks(), pack, 0)
        xs_ref[chunk_rows(n_chunks(), 2), :] = jnp.zeros((2 * MOE_CHUNK, d), BF16)
        acc_ref[chunk_rows(n_chunks(), 2), :] = jnp.zeros((2 * MOE_CHUNK, d), F32)

    last_slice = f == pl.num_programs(2) - 1

    @pl.when(jnp.logical_not(last_slice))
    def _():
        def expert(m, carry):
            xm = xs_ref[chunk_rows(m), :]
            acc_ref[chunk_rows(m), :] += _swiglu(xm, wg_ref[0], wu_ref[0], wd_ref[0])
            return carry

        lax.fori_loop(0, n_chunks(), expert, 0)

    @pl.when(last_slice)
    def _combine():
        def expert(m, carry):
            xm = xs_ref[chunk_rows(m), :]
            y = acc_ref[chunk_rows(m), :] + _swiglu(xm, wg_ref[0], wu_ref[0], wd_ref[0])
            y_hi, y_lo = _split_bf16(y)
            xs_ref[chunk_rows(m), :] = y_hi
            acc_ref[chunk_rows(m), :] = y_lo.astype(F32)
            return carry

        lax.fori_loop(0, n_chunks(), expert, 0)

        @pl.when(e == 0)
        def _():
            residual_copy().wait()

        for b, tokens in enumerate(blocks):
            first = start_ref[b] // MOE_CHUNK
            y_hi = xs_ref[chunk_rows(first, 2), :]
            y_lo = acc_ref[chunk_rows(first, 2), :].astype(BF16)
            dest = (_lane((MOE_CHUNK, 2 * MOE_CHUNK)) + first * MOE_CHUNK).astype(F32)
            onehot = jnp.where(col_ref[0, :, b:b + 1] == dest, 1.0, 0.0).astype(BF16)
            y_ref[tokens, :] += col_ref[1, :, b:b + 1] * (_dot(onehot, y_hi) + _dot(onehot, y_lo))


def _experts(x2, hb, rank_t, comb_t, wg, wu, wd, ff_split):
    n, d = hb.shape
    n_e, _, d_ff = wg.shape
    ff = d_ff // ff_split
    assert MOE_GROUP // MOE_CHUNK <= LANES
    group = lambda s, e, f: (s, 0)
    once = pl.Buffered(1)
    return pl.pallas_call(
        _experts_kernel,
        grid=(n // MOE_GROUP, n_e, ff_split),
        in_specs=[
            pl.BlockSpec(memory_space=pl.ANY),
            pl.BlockSpec((MOE_GROUP, d), group, pipeline_mode=once),
            pl.BlockSpec((1, N_EXPERTS, MOE_GROUP), lambda s, e, f: (s, 0, 0)),
            pl.BlockSpec((1, N_EXPERTS, MOE_GROUP), lambda s, e, f: (s, 0, 0)),
            pl.BlockSpec((1, d, ff), lambda s, e, f: (e, 0, f)),
            pl.BlockSpec((1, d, ff), lambda s, e, f: (e, 0, f)),
            pl.BlockSpec((1, ff, d), lambda s, e, f: (e, f, 0)),
        ],
        out_specs=pl.BlockSpec((MOE_GROUP, d), group, pipeline_mode=once),
        out_shape=jax.ShapeDtypeStruct((n, d), F32),
        scratch_shapes=[
            pltpu.VMEM((MOE_GROUP + 2 * MOE_CHUNK, d), BF16),
            pltpu.VMEM((MOE_GROUP + 2 * MOE_CHUNK, d), F32),
            pltpu.VMEM((2, LANES, LANES), F32),
            pltpu.SMEM((MOE_GROUP // MOE_CHUNK + 1,), jnp.int32),
            pltpu.SemaphoreType.DMA(()),
        ],
        compiler_params=_params("arbitrary", "arbitrary", "arbitrary"),
        name="experts",
    )(x2, hb, rank_t, comb_t, wg, wu, wd)


def _norm_kernel(x_ref, g_ref, o_ref):
    o_ref[...] = _rms(x_ref[...], g_ref[...])


def _rowwise(body, name, tm, x2, *rest):
    n, d = x2.shape
    row = pl.BlockSpec((tm, d), lambda i: (i, 0))
    specs = [row if a.shape == (n, d) else pl.BlockSpec((1, d), lambda i: (0, 0)) for a in (x2,) + rest]
    return pl.pallas_call(
        body,
        grid=(n // tm,),
        in_specs=specs,
        out_specs=row,
        out_shape=jax.ShapeDtypeStruct((n, d), F32),
        compiler_params=_params("parallel"),
        name=name,
    )(x2, *rest)


def _rope_tables(seq):
    pos = np.arange(seq, dtype=np.float64)
    inv_freq = ROPE_THETA ** (-np.arange(0, HEAD_DIM, 2, dtype=np.float64) / HEAD_DIM)
    ang = pos[:, None] * inv_freq[None, :]
    cos, sin = np.cos(ang), np.sin(ang)
    reps = LANES // HEAD_DIM
    cos_t = np.tile(np.concatenate([cos, cos], axis=-1), (1, reps))
    sin_t = np.tile(np.concatenate([-sin, sin], axis=-1), (1, reps))
    return jnp.asarray(cos_t, F32), jnp.asarray(sin_t, F32)


def _pair_heads_by_group(w, axis):
    shape = w.shape
    split = shape[:axis] + (SWA_KV_HEADS, SWA_GROUP, HEAD_DIM) + shape[axis + 1:]
    return jnp.swapaxes(w.reshape(split), axis, axis + 1).reshape(shape)


def _row_tile(n, want):
    t = min(n, want)
    assert n % t == 0
    return t


def kernel(x, norm_mix, w_in, attn_sinks, w_br_a, w_br_b, w_br_c, w_out, norm_ffn, w_ff_gate, w_ff_up, w_ff_down, w_router, w_moe_gate, w_moe_up, w_moe_down, norm_final):
    batch, seq, d = x.shape
    n = batch * seq
    depth = norm_mix.shape[0]
    tm = _row_tile(seq, 512)
    cos_t, sin_t = _rope_tables(seq)
    x2 = x.reshape(n, d).astype(F32)
    expert_bf16 = {}

    for layer in range(depth):
        w_l = w_in[layer]
        w_qa = _pair_heads_by_group(w_l[:, :WIDTH_A], 1)
        w_qkv = jnp.concatenate([w_qa, w_l[:, WIDTH_A:QKV_COLS]], axis=1).astype(BF16)
        w_gates = w_l[:, QKV_COLS:].astype(BF16)
        gain = norm_mix[layer].reshape(1, d)

        idx = layer // 2
        is_moe = layer % 2 == 1
        feeds_moe = not is_moe and layer + 1 < depth
        nxt = (layer + 1) // 2
        flat = lambda w: w.reshape(-1, w.shape[-1])

        qa, ka, va, qb, kb, vb, qc, kc, vc = _inproj(x2, gain, w_qkv, cos_t, sin_t, seq, tm)
        oa = _swa(attn_sinks[layer], qa, ka, va, batch, seq, tm)
        ob = _moba(qb, kb, vb, batch, seq)
        oc = _dilated(qc, kc, vc, batch, seq)
        gain_f = norm_ffn[layer].reshape(1, d)
        rider = flat(w_moe_down[idx]) if is_moe else flat(w_moe_up[nxt]) if feeds_moe else None
        x2, *cast = _merge(x2, gain, oa, ob, oc, w_gates,
                           _pair_heads_by_group(w_br_a[layer], 0).astype(BF16),
                           w_br_b[layer].astype(BF16), w_br_c[layer].astype(BF16),
                           w_out[layer].astype(BF16), tm, rider)
        if cast:
            expert_bf16["down" if is_moe else "up"] = cast[0]

        if not is_moe:
            x2, *cast = _ffn(x2, gain_f, w_ff_gate[idx].astype(BF16), w_ff_up[idx].astype(BF16),
                             w_ff_down[idx].astype(BF16), tm,
                             flat(w_moe_gate[nxt]) if feeds_moe else None)
            if cast:
                expert_bf16["gate"] = cast[0]
        else:
            w_r = jnp.pad(w_router[idx].T, ((0, ROUTER_ROWS - N_EXPERTS), (0, 0)))
            hb, rank_t, comb_t = _router(x2, gain_f, w_r, tm)
            weights = [expert_bf16.pop(k).reshape(w.shape) if k in expert_bf16 else w.astype(BF16)
                       for k, w in (("gate", w_moe_gate[idx]), ("up", w_moe_up[idx]),
                                    ("down", w_moe_down[idx]))]
            x2 = _experts(x2, hb, rank_t, comb_t, *weights, 2)

    return _rowwise(_norm_kernel, "final_norm", tm, x2, norm_final.reshape(1, d)).reshape(batch, seq, d)
```
